```python
import math
import jax, jax.numpy as jnp
from jax import lax
import numpy as np

D_MODEL = 1024
BATCH = 8
SEQ = 2048
DEPTH = 2
DEC_BATCH = 128
DEC_SEQ = 1
PAST_LEN = 16384
PAGE_SIZE = 128

HG_HEADS = 8
HG_DK = 128
HG_DV = 128
HG_WIDTH = HG_HEADS * HG_DK
CHUNK = 64
LB_FLOOR = 1e-30
LRU_WIDTH = D_MODEL
LRU_BLOCKS = 8
LRU_BS = LRU_WIDTH // LRU_BLOCKS
LRU_C = 8.0
CONV_W = 4
D_FF = 4 * D_MODEL
EPS = 1e-6
COL_SIZES = [HG_WIDTH, HG_WIDTH, HG_HEADS * HG_DV, HG_HEADS * HG_DV,
             LRU_WIDTH, LRU_WIDTH, D_MODEL, D_MODEL]
N_IN = sum(COL_SIZES)

kernel_name = "hgrn2_rglru_gated_parallel_decoder_step"

F32 = jnp.float32


def _rmsnorm(x, g):
    xf = x.astype(F32)
    return xf * lax.rsqrt(jnp.mean(xf * xf, axis=-1, keepdims=True) + EPS) * g.astype(F32)


def _split_cols(u):
    idx = [int(v) for v in np.cumsum(COL_SIZES)[:-1]]
    return jnp.split(u, idx, axis=-1)


def _hgrn2(q, k, v, logf, S0):
    B, L, H, DK = q.shape
    DV = v.shape[-1]
    C = math.gcd(L, CHUNK)
    N = L // C

    def to_chunks(t):
        return t.reshape(B, N, C, H, t.shape[-1]).transpose(1, 0, 3, 2, 4)

    xs = tuple(to_chunks(t.astype(F32)) for t in (q, k, v, logf))
    mask = jnp.tril(jnp.ones((C, C), dtype=bool))[None, None, :, :, None]

    def step(S, inp):
        qc, kc, vc, gc = inp
        G = jnp.cumsum(gc, axis=2)
        o_inter = jnp.einsum('bhtk,bhkv->bhtv', qc * jnp.exp(G), S)
        diff = G[:, :, :, None, :] - G[:, :, None, :, :]
        decay = jnp.where(mask, jnp.exp(jnp.where(mask, diff, 0.0)), 0.0)
        A = jnp.einsum('bhtk,bhsk,bhtsk->bhts', qc, kc, decay)
        o = o_inter + jnp.einsum('bhts,bhsv->bhtv', A, vc)
        G_last = G[:, :, -1:, :]
        S_new = (jnp.exp(G_last[:, :, 0, :])[..., None] * S
                 + jnp.einsum('bhsk,bhsv->bhkv', kc * jnp.exp(G_last - G), vc))
        return S_new, o

    S_fin, o = lax.scan(step, S0, xs)
    o = o.transpose(1, 0, 3, 2, 4).reshape(B, L, H, DV)
    return o, S_fin


def _causal_conv(x, buf, w, b):
    L = x.shape[1]
    full = jnp.concatenate([buf.astype(F32), x.astype(F32)], axis=1)
    out = b.astype(F32) + sum(full[:, j:j + L] * w[j].astype(F32) for j in range(CONV_W))
    return out, full[:, -(CONV_W - 1):]


def _blockdiag(x, w):
    B, L, _ = x.shape
    xb = x.reshape(B, L, LRU_BLOCKS, LRU_BS)
    return jnp.einsum('blnd,nde->blne', xb, w.astype(F32)).reshape(B, L, LRU_WIDTH)


def _rglru(xc, r, i, lam, h0):
    log_a = -LRU_C * r * jax.nn.softplus(-lam.astype(F32))
    a = jnp.exp(log_a)
    b = jnp.sqrt(-jnp.expm1(2.0 * log_a)) * (i * xc)
    b = b.at[:, 0].add(a[:, 0] * h0)

    def comb(left, right):
        a1, b1 = left
        a2, b2 = right
        return a1 * a2, a2 * b1 + b2

    _, h = lax.associative_scan(comb, (a, b), axis=1)
    return h, h[:, -1]


def _trunk(x, c, st_hg, st_lru, st_conv, lb, weights):
    (w_mod, b_mod, norm1_g, norm2_g, w_in, hg_onorm_g, lru_conv_w, lru_conv_b,
     lru_wa, lru_ba, lru_wx, lru_bx, lru_lambda, w_branch_a, w_branch_b, w_out,
     w_ff1, w_ff2, final_norm_g) = weights
    dt = x.dtype
    B, L, _ = x.shape
    cs = jax.nn.silu(c.astype(F32))
    new_hg, new_lru, new_conv = [], [], []
    for l in range(DEPTH):
        mod = cs @ w_mod[l].astype(F32) + b_mod[l].astype(F32)
        sh1, sc1, g1, sh2, sc2, g2 = [m[:, None, :] for m in jnp.split(mod, 6, axis=-1)]

        h = _rmsnorm(x, norm1_g[l]) * (1.0 + sc1) + sh1
        u = h @ w_in[l].astype(F32)
        qa, fa, ia, oga, xb, yb, ga, gb = _split_cols(u)

        lbl = lb[l]
        q = jax.nn.silu(qa).reshape(B, L, HG_HEADS, HG_DK)
        logf = jnp.logaddexp(jnp.log(jnp.maximum(lbl, LB_FLOOR)),
                             jnp.log1p(-lbl) + jax.nn.log_sigmoid(fa))
        k = (1.0 - lbl) * jax.nn.sigmoid(-fa)
        o, S_new = _hgrn2(q, k.reshape(B, L, HG_HEADS, HG_DK), ia.reshape(B, L, HG_HEADS, HG_DV),
                          logf.reshape(B, L, HG_HEADS, HG_DK), st_hg[l].astype(F32))
        o = _rmsnorm(o, hg_onorm_g[l]) * jax.nn.silu(oga.reshape(B, L, HG_HEADS, HG_DV))
        o_a = o.reshape(B, L, HG_HEADS * HG_DV)

        xc, buf = _causal_conv(xb, st_conv[l], lru_conv_w[l], lru_conv_b[l])
        r = jax.nn.sigmoid(_blockdiag(xc, lru_wa[l]) + lru_ba[l].astype(F32))
        ig = jax.nn.sigmoid(_blockdiag(xc, lru_wx[l]) + lru_bx[l].astype(F32))
        hseq, h_last = _rglru(xc, r, ig, lru_lambda[l], st_lru[l].astype(F32))
        o_b = hseq * jax.nn.gelu(yb)

        merged = (jax.nn.sigmoid(ga) * (o_a @ w_branch_a[l].astype(F32))
                  + jax.nn.sigmoid(gb) * (o_b @ w_branch_b[l].astype(F32)))
        x = (x.astype(F32) + g1 * (merged @ w_out[l].astype(F32))).astype(dt)

        h2 = _rmsnorm(x, norm2_g[l]) * (1.0 + sc2) + sh2
        ff = jnp.square(jax.nn.relu(h2 @ w_ff1[l].astype(F32))) @ w_ff2[l].astype(F32)
        x = (x.astype(F32) + g2 * ff).astype(dt)

        new_hg.append(S_new)
        new_lru.append(h_last)
        new_conv.append(buf)
    y = _rmsnorm(x, final_norm_g).astype(dt)
    return (y, jnp.stack(new_hg).astype(st_hg.dtype), jnp.stack(new_lru).astype(st_lru.dtype),
            jnp.stack(new_conv).astype(st_conv.dtype))


def setup_inputs(seed: int = 0) -> dict:
    key = jax.random.key(seed)
    ks = jax.random.split(key, 32)
    nrm = jax.random.normal
    D = D_MODEL
    u = jax.random.uniform(ks[20], (DEPTH, LRU_WIDTH), minval=0.9, maxval=0.999)
    a0 = u ** (1.0 / LRU_C)
    lam = jnp.log(a0) - jnp.log1p(-a0)
    return {
        "x_prompt": nrm(ks[0], (BATCH, SEQ, D), F32),
        "x_sample": nrm(ks[1], (DEC_BATCH, DEC_SEQ, D), F32),
        "state_hgrn": 0.5 * nrm(ks[2], (DEPTH, DEC_BATCH, HG_HEADS, HG_DK, HG_DV), F32),
        "state_rglru": nrm(ks[3], (DEPTH, DEC_BATCH, LRU_WIDTH), F32),
        "state_conv": nrm(ks[4], (DEPTH, DEC_BATCH, CONV_W - 1, LRU_WIDTH), F32),
        "c_prompt": nrm(ks[5], (BATCH, D), F32),
        "c_sample": nrm(ks[6], (DEC_BATCH, D), F32),
        "w_mod": 0.5 * D ** -0.5 * nrm(ks[7], (DEPTH, D, 6 * D), F32),
        "b_mod": 0.02 * nrm(ks[8], (DEPTH, 6 * D), F32),
        "norm1_g": 1.0 + 0.02 * nrm(ks[9], (DEPTH, D), F32),
        "norm2_g": 1.0 + 0.02 * nrm(ks[10], (DEPTH, D), F32),
        "w_in": D ** -0.5 * nrm(ks[11], (DEPTH, D, N_IN), F32),
        "hg_lower": nrm(ks[12], (DEPTH, HG_WIDTH), F32),
        "hg_onorm_g": 1.0 + 0.02 * nrm(ks[13], (DEPTH, HG_DV), F32),
        "lru_conv_w": CONV_W ** -0.5 * nrm(ks[14], (DEPTH, CONV_W, LRU_WIDTH), F32),
        "lru_conv_b": 0.02 * nrm(ks[15], (DEPTH, LRU_WIDTH), F32),
        "lru_wa": LRU_BS ** -0.5 * nrm(ks[16], (DEPTH, LRU_BLOCKS, LRU_BS, LRU_BS), F32),
        "lru_ba": 0.02 * nrm(ks[17], (DEPTH, LRU_WIDTH), F32),
        "lru_wx": LRU_BS ** -0.5 * nrm(ks[18], (DEPTH, LRU_BLOCKS, LRU_BS, LRU_BS), F32),
        "lru_bx": 0.02 * nrm(ks[19], (DEPTH, LRU_WIDTH), F32),
        "lru_lambda": lam,
        "w_branch_a": HG_WIDTH ** -0.5 * nrm(ks[21], (DEPTH, HG_HEADS * HG_DV, D), F32),
        "w_branch_b": LRU_WIDTH ** -0.5 * nrm(ks[22], (DEPTH, LRU_WIDTH, D), F32),
        "w_out": D ** -0.5 * nrm(ks[23], (DEPTH, D, D), F32),
        "w_ff1": D ** -0.5 * nrm(ks[24], (DEPTH, D, D_FF), F32),
        "w_ff2": D_FF ** -0.5 * nrm(ks[25], (DEPTH, D_FF, D), F32),
        "final_norm_g": 1.0 + 0.02 * nrm(ks[26], (D,), F32),
    }


def reference(x_prompt, x_sample, state_hgrn, state_rglru, state_conv, c_prompt, c_sample,
              w_mod, b_mod, norm1_g, norm2_g, w_in, hg_lower, hg_onorm_g, lru_conv_w, lru_conv_b,
              lru_wa, lru_ba, lru_wx, lru_bx, lru_lambda, w_branch_a, w_branch_b, w_out,
              w_ff1, w_ff2, final_norm_g):
    p = jax.nn.softmax(hg_lower.astype(F32), axis=0)
    lb = jnp.cumsum(p, axis=0) - p[0:1]
    weights = (w_mod, b_mod, norm1_g, norm2_g, w_in, hg_onorm_g, lru_conv_w, lru_conv_b,
               lru_wa, lru_ba, lru_wx, lru_bx, lru_lambda, w_branch_a, w_branch_b, w_out,
               w_ff1, w_ff2, final_norm_g)
    Bp = x_prompt.shape[0]
    zero_hg = jnp.zeros((DEPTH, Bp, HG_HEADS, HG_DK, HG_DV), state_hgrn.dtype)
    zero_lru = jnp.zeros((DEPTH, Bp, LRU_WIDTH), state_rglru.dtype)
    zero_conv = jnp.zeros((DEPTH, Bp, CONV_W - 1, LRU_WIDTH), state_conv.dtype)
    y_prompt, hg_p, lru_p, conv_p = _trunk(x_prompt, c_prompt, zero_hg, zero_lru, zero_conv, lb, weights)
    y_sample, hg_s, lru_s, conv_s = _trunk(x_sample, c_sample, state_hgrn, state_rglru, state_conv, lb, weights)
    return (y_prompt, y_sample, hg_p, lru_p, conv_p, hg_s, lru_s, conv_s)
```

```python
import functools

import numpy as np
import jax
import jax.numpy as jnp
from jax import lax
from jax.experimental import pallas as pl
from jax.experimental.pallas import tpu as pltpu

F32 = jnp.float32
BF16 = jnp.bfloat16

D_MODEL = 1024
DEPTH = 2
HG_HEADS = 8
HG_DK = 128
HG_DV = 128
LB_FLOOR = 1e-30
LRU_BLOCKS = 8
LRU_BS = 128
LRU_C = 8.0
CONV_W = 4
D_FF = 4 * D_MODEL
EPS = 1e-6
N_IN = 8 * D_MODEL

LANES = 128
SUBLANES = 8
HG_CHUNK = 128
HG_LEVELS = 7
MIX_TL = 256
FFN_TM = 512
SAMPLE_GROUP = 8
VMEM_LIMIT = 56 * 1024 * 1024


def _sigmoid(x):
    return 1.0 / (1.0 + jnp.exp(-x))


def _silu(x):
    return x * _sigmoid(x)


def _gelu_tanh(x):
    return 0.5 * x * (1.0 + jnp.tanh(0.7978845608028654 * (x + 0.044715 * (x * x * x))))


def _log1p(x):
    u = 1.0 + x
    return jnp.where(u == 1.0, x, jnp.log(u) * (x / (u - 1.0)))


def _softplus(z):
    return jnp.maximum(z, 0.0) + _log1p(jnp.exp(-jnp.abs(z)))


def _rmsnorm(x, g):
    ms = jnp.mean(x * x, axis=-1, keepdims=True)
    return x * lax.rsqrt(ms + EPS) * g


def _dot(a, b):
    return jnp.dot(a, b, preferred_element_type=F32)


def _dot_nt(a, b):
    return lax.dot_general(a, b, (((1,), (1,)), ((), ())), preferred_element_type=F32)


def _dot_tn(a, b):
    return lax.dot_general(a, b, (((0,), (0,)), ((), ())), preferred_element_type=F32)


def _lower_bound(hgl, layer):
    m = jnp.max(hgl, axis=0, keepdims=True)
    e = jnp.exp(hgl - m)
    p = e / jnp.sum(e, axis=0, keepdims=True)
    cum = p[0:1]
    for j in range(1, layer + 1):
        cum = cum + p[j:j + 1]
    return cum - p[0:1]


def _forget_key(fa, lb):
    e = jnp.exp(-jnp.abs(fa))
    r = 1.0 / (1.0 + e)
    er = e * r
    pos = fa >= 0.0
    sig_p = jnp.where(pos, r, er)
    sig_n = jnp.where(pos, er, r)
    f = jnp.maximum(lb, LB_FLOOR) + (1.0 - lb) * sig_p
    k = (1.0 - lb) * sig_n
    return f, k


def _lru_coeffs(xc, gates, ba, bx, sp):
    r = _sigmoid(gates[:, :LRU_BS] + ba)
    ig = _sigmoid(gates[:, LRU_BS:] + bx)
    log_a = (-LRU_C) * r * sp
    a = jnp.exp(log_a)
    th = jnp.tanh(log_a)
    one_m_a2 = (-2.0 * th) / (1.0 - th)
    return a, jnp.sqrt(one_m_a2) * (ig * xc)


def _mod_kernel(c_ref, w_ref, b_ref, o_ref):
    cs = _silu(c_ref[...]).astype(BF16)
    o_ref[0] = _dot(cs, w_ref[0].astype(BF16)) + b_ref[0]


def _mod_call(c_all, w_mod, b_mod):
    n_rows = c_all.shape[0]
    tn = 1536
    return pl.pallas_call(
        _mod_kernel,
        out_shape=jax.ShapeDtypeStruct((DEPTH, n_rows, 6 * D_MODEL), F32),
        grid=(DEPTH, 6 * D_MODEL // tn),
        in_specs=[
            pl.BlockSpec((n_rows, D_MODEL), lambda l, j: (0, 0)),
            pl.BlockSpec((1, D_MODEL, tn), lambda l, j: (l, 0, j)),
            pl.BlockSpec((1, 1, tn), lambda l, j: (l, 0, j)),
        ],
        out_specs=pl.BlockSpec((1, n_rows, tn), lambda l, j: (l, 0, j)),
        compiler_params=pltpu.CompilerParams(
            dimension_semantics=("arbitrary", "arbitrary"), vmem_limit_bytes=VMEM_LIMIT),
        name="mod",
    )(c_all, w_mod, b_mod.reshape(DEPTH, 1, 6 * D_MODEL))


def _hg_tables():
    t = np.arange(HG_CHUNK)
    x = t[:, None] ^ t[None, :]
    hb = np.floor(np.log2(np.maximum(x, 1))).astype(np.int32)
    lvl = np.where(t[:, None] > t[None, :], hb + 1, np.where(t[:, None] == t[None, :], 0, -1)).astype(np.int32)
    qsel = np.stack([np.broadcast_to(((t >> i) & 1)[:, None], (HG_CHUNK, HG_DK)) for i in range(HG_LEVELS)])
    return jnp.asarray(lvl), jnp.asarray(qsel.astype(np.float32))


def _pivot_rows(g, level):
    n = g.shape[0]
    b = 1 << level
    if 2 * b >= SUBLANES:
        g3 = g.reshape(n // (2 * b), 2 * b, LANES)
        return jnp.broadcast_to(g3[:, b - 1:b, :], g3.shape).reshape(n, LANES)
    g3 = g.reshape(n // SUBLANES, SUBLANES, LANES)
    sub = lax.broadcasted_iota(jnp.int32, g3.shape, 1)

    def row(i):
        return jnp.broadcast_to(g3[:, i:i + 1, :], g3.shape)

    if b == 2:
        r = jnp.where(sub < 4, row(1), row(5))
    else:
        r = jnp.where(sub < 2, row(0), jnp.where(sub < 4, row(2), jnp.where(sub < 6, row(4), row(6))))
    return r.reshape(n, LANES)


def _hg_chunk(q, k, v, g, st, lvl_ref, qsel_ref):
    lvl = lvl_ref[...]
    a = jnp.where(lvl == 0, _dot_nt(q.astype(BF16), k.astype(BF16)), 0.0)
    for i in range(HG_LEVELS):
        e = jnp.exp(-jnp.abs(g - _pivot_rows(g, i)))
        x = (jnp.where(qsel_ref[i] > 0.5, q, k) * e).astype(BF16)
        a = jnp.where(lvl == i + 1, _dot_nt(x, x), a)
    g_last = g[HG_CHUNK - 1:HG_CHUNK, :]
    qg = (q * jnp.exp(g)).astype(BF16)
    vb = v.astype(BF16)
    o = _dot_nt(qg, st.astype(BF16)) + _dot(a.astype(BF16), vb)
    kd = (k * jnp.exp(g_last - g)).astype(BF16)
    st_new = st * jnp.exp(g_last) + _dot_tn(vb, kd)
    return o, st_new


def _chunk_cumsum(y, rowmod):
    d = 1
    while d < HG_CHUNK:
        y = y + jnp.where(rowmod >= d, pltpu.roll(y, d, axis=0), 0.0)
        d *= 2
    return y


def _lru_scan(a, b, h0):
    t = a.shape[0]
    a3 = a.reshape(t // SUBLANES, SUBLANES, LANES)
    b3 = b.reshape(t // SUBLANES, SUBLANES, LANES)
    sub = lax.broadcasted_iota(jnp.int32, a3.shape, 1)
    d = 1
    while d < SUBLANES:
        m = sub >= d
        b3 = a3 * jnp.where(m, pltpu.roll(b3, d, axis=1), 0.0) + b3
        a3 = a3 * jnp.where(m, pltpu.roll(a3, d, axis=1), 1.0)
        d *= 2
    hs = []
    h = h0
    for gidx in range(t // SUBLANES):
        hg = a3[gidx] * h + b3[gidx]
        hs.append(hg)
        h = hg[SUBLANES - 1:SUBLANES, :]
    return jnp.concatenate(hs, axis=0), h


def _prompt_mixer_kernel(layer, n_t,
                         x_ref, mod_ref, hgl_ref, n1g_ref, win_ref, ong_ref, cw_ref, cb_ref, wax_ref,
                         ba_ref, bx_ref, lam_ref, wba_ref, wbb_ref, wout_ref, lvl_ref, qsel_ref,
                         xo_ref, hg_ref, lru_ref, conv_ref,
                         q_s, k_s, v_s, g_s, og_s, oa_s, ob_s, xpad_s, st_s, hl_s):
    tl = MIX_TL
    d = D_MODEL
    t_idx = pl.program_id(1)

    @pl.when(t_idx == 0)
    def _():
        st_s[...] = jnp.zeros(st_s.shape, F32)
        hl_s[...] = jnp.zeros(hl_s.shape, F32)
        xpad_s[0:SUBLANES, :] = jnp.zeros((SUBLANES, d), F32)

    x = x_ref[0]
    mod = mod_ref[0]
    sh1, sc1, g1 = mod[:, 0:d], mod[:, d:2 * d], mod[:, 2 * d:3 * d]
    hb = (_rmsnorm(x, n1g_ref[...]) * (1.0 + sc1) + sh1).astype(BF16)

    def proj(j):
        return _dot(hb, win_ref[:, j * d:(j + 1) * d])

    def heads(val, dst):
        for h in range(HG_HEADS):
            dst[h] = val[:, h * LANES:(h + 1) * LANES]

    heads(_silu(proj(0)), q_s)
    lb = _lower_bound(hgl_ref[...], layer)
    f, k = _forget_key(proj(1), lb)
    heads(k, k_s)
    logf = jnp.minimum(jnp.log(f), 0.0)
    rowmod = lax.broadcasted_iota(jnp.int32, (tl, LANES), 0) % HG_CHUNK
    for h in range(HG_HEADS):
        g_s[h] = _chunk_cumsum(logf[:, h * LANES:(h + 1) * LANES], rowmod)
    heads(proj(2), v_s)
    heads(_silu(proj(3)), og_s)

    ong = ong_ref[...]

    def hg_body(idx, carry):
        c = idx // HG_HEADS
        h = idx % HG_HEADS
        rows = pl.ds(pl.multiple_of(c * HG_CHUNK, HG_CHUNK), HG_CHUNK)
        o, st_new = _hg_chunk(q_s[h, rows, :], k_s[h, rows, :], v_s[h, rows, :], g_s[h, rows, :],
                              st_s[h], lvl_ref, qsel_ref)
        st_s[h] = st_new
        oa_s[h, rows, :] = _rmsnorm(o, ong) * og_s[h, rows, :]
        return carry

    lax.fori_loop(0, (tl // HG_CHUNK) * HG_HEADS, hg_body, 0)

    xb = proj(4)
    xpad_s[SUBLANES:SUBLANES + tl, :] = xb
    cw = cw_ref[...]
    xc = cb_ref[...] + cw[3:4] * xb
    for j in range(CONV_W - 1):
        xc = xc + cw[j:j + 1] * xpad_s[pl.ds(SUBLANES - (CONV_W - 1) + j, tl), :]
    tail = xb[tl - (CONV_W - 1):tl, :]
    xpad_s[SUBLANES - (CONV_W - 1):SUBLANES, :] = tail
    conv_ref[0] = tail
    yb = _gelu_tanh(proj(5))
    sp = _softplus(-lam_ref[...])
    for n in range(LRU_BLOCKS):
        cols = slice(n * LRU_BS, (n + 1) * LRU_BS)
        xcn = xc[:, cols]
        gates = _dot(xcn.astype(BF16), wax_ref[n])
        a, bt = _lru_coeffs(xcn, gates, ba_ref[:, cols], bx_ref[:, cols], sp[:, cols])
        hseq, h_last = _lru_scan(a, bt, hl_s[:, cols])
        hl_s[:, cols] = h_last
        ob_s[:, cols] = hseq * yb[:, cols]
    lru_ref[0] = hl_s[...]

    o_a = jnp.concatenate([oa_s[h] for h in range(HG_HEADS)], axis=1).astype(BF16)
    merged = (_sigmoid(proj(6)) * _dot(o_a, wba_ref[...])
              + _sigmoid(proj(7)) * _dot(ob_s[...].astype(BF16), wbb_ref[...]))
    xo_ref[0] = x + g1 * _dot(merged.astype(BF16), wout_ref[...])

    @pl.when(t_idx == n_t - 1)
    def _():
        for h in range(HG_HEADS):
            hg_ref[0, h] = st_s[h].T


def _const_spec(shape):
    nd = len(shape)
    return pl.BlockSpec(shape, lambda *_: (0,) * nd, pipeline_mode=pl.Buffered(1))


def _prompt_mixer_call(layer, x, mod_p, hg_lower, wl, lvl, qsel):
    bsz, seq, d = x.shape
    tl = MIX_TL
    n_t = seq // tl
    hshape = (HG_HEADS, tl, LANES)
    out_shapes = (
        jax.ShapeDtypeStruct((bsz, seq, d), F32),
        jax.ShapeDtypeStruct((bsz, HG_HEADS, HG_DK, HG_DV), F32),
        jax.ShapeDtypeStruct((bsz, 1, d), F32),
        jax.ShapeDtypeStruct((bsz, CONV_W - 1, d), F32),
    )
    in_specs = [
        pl.BlockSpec((1, tl, d), lambda b, t: (b, t, 0)),
        pl.BlockSpec((1, 1, 6 * d), lambda b, t: (b, 0, 0)),
        _const_spec((DEPTH, d)),
        _const_spec((1, d)),
        _const_spec((d, N_IN)),
        _const_spec((1, HG_DV)),
        _const_spec((CONV_W, d)),
        _const_spec((1, d)),
        _const_spec((LRU_BLOCKS, LRU_BS, 2 * LRU_BS)),
        _const_spec((1, d)),
        _const_spec((1, d)),
        _const_spec((1, d)),
        _const_spec((d, d)),
        _const_spec((d, d)),
        _const_spec((d, d)),
        _const_spec((HG_CHUNK, HG_CHUNK)),
        _const_spec((HG_LEVELS, HG_CHUNK, HG_DK)),
    ]
    out_specs = (
        pl.BlockSpec((1, tl, d), lambda b, t: (b, t, 0)),
        pl.BlockSpec((1, HG_HEADS, HG_DK, HG_DV), lambda b, t: (b, 0, 0, 0)),
        pl.BlockSpec((1, 1, d), lambda b, t: (b, 0, 0)),
        pl.BlockSpec((1, CONV_W - 1, d), lambda b, t: (b, 0, 0)),
    )
    scratch = [
        pltpu.VMEM(hshape, F32),
        pltpu.VMEM(hshape, F32),
        pltpu.VMEM(hshape, F32),
        pltpu.VMEM(hshape, F32),
        pltpu.VMEM(hshape, F32),
        pltpu.VMEM(hshape, F32),
        pltpu.VMEM((tl, d), F32),
        pltpu.VMEM((tl + SUBLANES, d), F32),
        pltpu.VMEM((HG_HEADS, HG_DV, HG_DK), F32),
        pltpu.VMEM((1, d), F32),
    ]
    return pl.pallas_call(
        functools.partial(_prompt_mixer_kernel, layer, n_t),
        out_shape=out_shapes,
        grid=(bsz, n_t),
        in_specs=in_specs,
        out_specs=out_specs,
        scratch_shapes=scratch,
        compiler_params=pltpu.CompilerParams(
            dimension_semantics=("arbitrary", "arbitrary"), vmem_limit_bytes=VMEM_LIMIT),
        name=f"prompt_mixer_{layer}",
    )(x, mod_p, hg_lower, wl["n1g"], wl["w_in"], wl["ong"], wl["cw"], wl["cb"], wl["wax"],
      wl["ba"], wl["bx"], wl["lam"], wl["wba"], wl["wbb"], wl["wout"], lvl, qsel)


def _ffn_kernel(final, x_ref, mod_ref, n2g_ref, w1_ref, w2_ref, fg_ref, o_ref):
    d = D_MODEL
    x = x_ref[0]
    mod = mod_ref[0]
    sh2, sc2, g2 = mod[:, 3 * d:4 * d], mod[:, 4 * d:5 * d], mod[:, 5 * d:6 * d]
    hb = (_rmsnorm(x, n2g_ref[...]) * (1.0 + sc2) + sh2).astype(BF16)
    acc = jnp.zeros(x.shape, F32)
    for c in range(D_FF // d):
        h1 = jnp.maximum(_dot(hb, w1_ref[:, c * d:(c + 1) * d]), 0.0)
        acc = acc + _dot((h1 * h1).astype(BF16), w2_ref[c * d:(c + 1) * d, :])
    xn = x + g2 * acc
    if final:
        xn = _rmsnorm(xn, fg_ref[...])
    o_ref[0] = xn


def _ffn_call(final, x, mod, wl, final_g, tm, name):
    bsz, seq, d = x.shape
    mrows = mod.shape[1]
    mod_spec = (pl.BlockSpec((1, 1, 6 * d), lambda b, t: (b, 0, 0)) if mrows == 1
                else pl.BlockSpec((1, tm, 6 * d), lambda b, t: (b, t, 0)))
    return pl.pallas_call(
        functools.partial(_ffn_kernel, final),
        out_shape=jax.ShapeDtypeStruct((bsz, seq, d), F32),
        grid=(bsz, seq // tm),
        in_specs=[
            pl.BlockSpec((1, tm, d), lambda b, t: (b, t, 0)),
            mod_spec,
            _const_spec((1, d)),
            _const_spec((d, D_FF)),
            _const_spec((D_FF, d)),
            _const_spec((1, d)),
        ],
        out_specs=pl.BlockSpec((1, tm, d), lambda b, t: (b, t, 0)),
        compiler_params=pltpu.CompilerParams(
            dimension_semantics=("arbitrary", "arbitrary"), vmem_limit_bytes=VMEM_LIMIT),
        name=name,
    )(x, mod, wl["n2g"], wl["w1"], wl["w2"], final_g)


def _sample_inproj_kernel(x_ref, mod_ref, n1g_ref, w_ref, o_ref):
    d = D_MODEL
    mod = mod_ref[...]
    hb = (_rmsnorm(x_ref[...], n1g_ref[...]) * (1.0 + mod[:, d:2 * d]) + mod[:, 0:d]).astype(BF16)
    o_ref[...] = _dot(hb, w_ref[...])


def _sample_inproj_call(layer, x, mod_s, wl):
    n, d = x.shape
    return pl.pallas_call(
        _sample_inproj_kernel,
        out_shape=jax.ShapeDtypeStruct((n, N_IN), F32),
        grid=(N_IN // d,),
        in_specs=[
            pl.BlockSpec((n, d), lambda j: (0, 0)),
            pl.BlockSpec((n, 6 * d), lambda j: (0, 0)),
            pl.BlockSpec((1, d), lambda j: (0, 0)),
            pl.BlockSpec((d, d), lambda j: (0, j)),
        ],
        out_specs=pl.BlockSpec((n, d), lambda j: (0, j)),
        compiler_params=pltpu.CompilerParams(
            dimension_semantics=("arbitrary",), vmem_limit_bytes=VMEM_LIMIT),
        name=f"sample_inproj_{layer}",
    )(x, mod_s, wl["n1g"], wl["w_in"])


def _sample_mixer_kernel(layer, n_groups,
                         u_ref, x_ref, mod_ref, hg_ref, lru_ref, conv_ref, hgl_ref, ong_ref, cw_ref, cb_ref,
                         wax_ref, ba_ref, bx_ref, lam_ref, wba_ref, wbb_ref, wout_ref,
                         xo_ref, hgo_ref, lruo_ref, convo_ref,
                         ft_s, kt_s, qt_s, v_s, o_s, ob_s):
    d = D_MODEL
    n = x_ref.shape[0]
    i = pl.program_id(0)

    @pl.when(i == 0)
    def _():
        q = _silu(u_ref[:, 0:d])
        lb = _lower_bound(hgl_ref[...], layer)
        f, k = _forget_key(u_ref[:, d:2 * d], lb)
        for h in range(HG_HEADS):
            cols = slice(h * LANES, (h + 1) * LANES)
            ft_s[h] = f[:, cols].T
            kt_s[h] = k[:, cols].T
            qt_s[h] = q[:, cols].T
        v_s[...] = u_ref[:, 2 * d:3 * d]
        xb = u_ref[:, 4 * d:5 * d]
        cw = cw_ref[...]
        xc = cb_ref[...] + cw[3:4] * xb
        for j in range(CONV_W - 1):
            xc = xc + cw[j:j + 1] * conv_ref[j]
        convo_ref[0] = conv_ref[1]
        convo_ref[1] = conv_ref[2]
        convo_ref[2] = xb
        yb = _gelu_tanh(u_ref[:, 5 * d:6 * d])
        sp = _softplus(-lam_ref[...])
        for blk in range(LRU_BLOCKS):
            cols = slice(blk * LRU_BS, (blk + 1) * LRU_BS)
            xcn = xc[:, cols]
            gates = _dot(xcn.astype(BF16), wax_ref[blk])
            a, bt = _lru_coeffs(xcn, gates, ba_ref[:, cols], bx_ref[:, cols], sp[:, cols])
            hn = a * lru_ref[:, cols] + bt
            lruo_ref[:, cols] = hn
            ob_s[:, cols] = hn * yb[:, cols]

    shift = (n - i * SAMPLE_GROUP) % n
    rows = pl.ds(pl.multiple_of(i * SAMPLE_GROUP, SAMPLE_GROUP), SAMPLE_GROUP)
    for h in range(HG_HEADS):
        cols = slice(h * LANES, (h + 1) * LANES)
        fr = pltpu.roll(ft_s[h], shift, axis=1)
        kr = pltpu.roll(kt_s[h], shift, axis=1)
        qr = pltpu.roll(qt_s[h], shift, axis=1)
        vg = v_s[rows, cols]
        o_rows = []
        for j in range(SAMPLE_GROUP):
            s_new = fr[:, j:j + 1] * hg_ref[j, h] + kr[:, j:j + 1] * vg[j:j + 1, :]
            hgo_ref[j, h] = s_new
            o_rows.append(jnp.sum(qr[:, j:j + 1] * s_new, axis=0, keepdims=True))
        o_s[rows, cols] = jnp.concatenate(o_rows, axis=0)

    @pl.when(i == n_groups - 1)
    def _():
        ong = ong_ref[...]
        parts = []
        for h in range(HG_HEADS):
            cols = slice(h * LANES, (h + 1) * LANES)
            parts.append(_rmsnorm(o_s[:, cols], ong) * _silu(u_ref[:, 3 * d + h * LANES:3 * d + (h + 1) * LANES]))
        o_a = jnp.concatenate(parts, axis=1).astype(BF16)
        merged = (_sigmoid(u_ref[:, 6 * d:7 * d]) * _dot(o_a, wba_ref[...])
                  + _sigmoid(u_ref[:, 7 * d:8 * d]) * _dot(ob_s[...].astype(BF16), wbb_ref[...]))
        g1 = mod_ref[:, 2 * d:3 * d]
        xo_ref[...] = x_ref[...] + g1 * _dot(merged.astype(BF16), wout_ref[...])


def _sample_mixer_call(layer, u, x, mod_s, st_hg, st_lru, st_conv_t, hg_lower, wl):
    n, d = x.shape
    grp = SAMPLE_GROUP
    n_groups = n // grp

    def full(shape):
        nd = len(shape)
        return pl.BlockSpec(shape, lambda i: (0,) * nd)

    st_spec = pl.BlockSpec((grp, HG_HEADS, HG_DK, HG_DV), lambda i: (i, 0, 0, 0))
    out_shapes = (
        jax.ShapeDtypeStruct((n, d), F32),
        jax.ShapeDtypeStruct((n, HG_HEADS, HG_DK, HG_DV), F32),
        jax.ShapeDtypeStruct((n, d), F32),
        jax.ShapeDtypeStruct((CONV_W - 1, n, d), F32),
    )
    in_specs = [
        full((n, N_IN)), full((n, d)), full((n, 6 * d)), st_spec, full((n, d)), full((CONV_W - 1, n, d)),
        full((DEPTH, d)), full((1, HG_DV)), full((CONV_W, d)), full((1, d)),
        full((LRU_BLOCKS, LRU_BS, 2 * LRU_BS)), full((1, d)), full((1, d)), full((1, d)),
        full((d, d)), full((d, d)), full((d, d)),
    ]
    out_specs = (full((n, d)), st_spec, full((n, d)), full((CONV_W - 1, n, d)))
    scratch = [
        pltpu.VMEM((HG_HEADS, HG_DK, n), F32),
        pltpu.VMEM((HG_HEADS, HG_DK, n), F32),
        pltpu.VMEM((HG_HEADS, HG_DK, n), F32),
        pltpu.VMEM((n, d), F32),
        pltpu.VMEM((n, d), F32),
        pltpu.VMEM((n, d), F32),
    ]
    return pl.pallas_call(
        functools.partial(_sample_mixer_kernel, layer, n_groups),
        out_shape=out_shapes,
        grid=(n_groups,),
        in_specs=in_specs,
        out_specs=out_specs,
        scratch_shapes=scratch,
        compiler_params=pltpu.CompilerParams(
            dimension_semantics=("arbitrary",), vmem_limit_bytes=VMEM_LIMIT),
        name=f"sample_mixer_{layer}",
    )(u, x, mod_s, st_hg, st_lru, st_conv_t, hg_lower, wl["ong"], wl["cw"], wl["cb"], wl["wax"],
      wl["ba"], wl["bx"], wl["lam"], wl["wba"], wl["wbb"], wl["wout"])


def _layer_weights(l, norm1_g, norm2_g, w_in, hg_onorm_g, lru_conv_w, lru_conv_b, lru_wa, lru_ba, lru_wx,
                   lru_bx, lru_lambda, w_branch_a, w_branch_b, w_out, w_ff1, w_ff2):
    d = D_MODEL
    return {
        "n1g": norm1_g[l].reshape(1, d),
        "n2g": norm2_g[l].reshape(1, d),
        "w_in": w_in[l].astype(BF16),
        "ong": hg_onorm_g[l].reshape(1, HG_DV),
        "cw": lru_conv_w[l],
        "cb": lru_conv_b[l].reshape(1, d),
        "wax": jnp.concatenate([lru_wa[l], lru_wx[l]], axis=-1).astype(BF16),
        "ba": lru_ba[l].reshape(1, d),
        "bx": lru_bx[l].reshape(1, d),
        "lam": lru_lambda[l].reshape(1, d),
        "wba": w_branch_a[l].astype(BF16),
        "wbb": w_branch_b[l].astype(BF16),
        "wout": w_out[l].astype(BF16),
        "w1": w_ff1[l].astype(BF16),
        "w2": w_ff2[l].astype(BF16),
    }


def kernel(x_prompt, x_sample, state_hgrn, state_rglru, state_conv, c_prompt, c_sample, w_mod, b_mod, norm1_g, norm2_g, w_in, hg_lower, hg_onorm_g, lru_conv_w, lru_conv_b, lru_wa, lru_ba, lru_wx, lru_bx, lru_lambda, w_branch_a, w_branch_b, w_out, w_ff1, w_ff2, final_norm_g):
    bsz, seq, d = x_prompt.shape
    n_s = x_sample.shape[0]
    assert d == D_MODEL and x_sample.shape[1] == 1 and seq % MIX_TL == 0 and seq % FFN_TM == 0
    assert n_s % SAMPLE_GROUP == 0 and n_s == LANES

    mod = _mod_call(jnp.concatenate([c_prompt, c_sample], axis=0), w_mod, b_mod)
    lvl, qsel = _hg_tables()
    final_g = final_norm_g.reshape(1, d)

    xp = x_prompt
    xs = x_sample.reshape(n_s, d)
    hg_p, lru_p, conv_p, hg_s, lru_s, conv_s = [], [], [], [], [], []
    for l in range(DEPTH):
        wl = _layer_weights(l, norm1_g, norm2_g, w_in, hg_onorm_g, lru_conv_w, lru_conv_b, lru_wa, lru_ba,
                            lru_wx, lru_bx, lru_lambda, w_branch_a, w_branch_b, w_out, w_ff1, w_ff2)
        final = l == DEPTH - 1
        mod_p = mod[l, :bsz].reshape(bsz, 1, 6 * d)
        mod_s = mod[l, bsz:]

        xp, hg, lru, conv = _prompt_mixer_call(l, xp, mod_p, hg_lower, wl, lvl, qsel)
        xp = _ffn_call(final, xp, mod_p, wl, final_g, FFN_TM, f"prompt_ffn_{l}")
        hg_p.append(hg)
        lru_p.append(lru.reshape(bsz, d))
        conv_p.append(conv)

        u = _sample_inproj_call(l, xs, mod_s, wl)
        xs, hg, lru, conv = _sample_mixer_call(l, u, xs, mod_s, state_hgrn[l], state_rglru[l],
                                               jnp.swapaxes(state_conv[l], 0, 1), hg_lower, wl)
        xs = _ffn_call(final, xs.reshape(1, n_s, d), mod_s.reshape(1, n_s, 6 * d), wl, final_g, n_s,
                       f"sample_ffn_{l}").reshape(n_s, d)
        hg_s.append(hg)
        lru_s.append(lru)
        conv_s.append(jnp.swapaxes(conv, 0, 1))

    return (xp, xs.reshape(n_s, 1, d),
            jnp.stack(hg_p), jnp.stack(lru_p), jnp.stack(conv_p),
            jnp.stack(hg_s), jnp.stack(lru_s), jnp.stack(conv_s))
```

```python
import functools

import numpy as np
import jax
import jax.numpy as jnp
from jax import lax
from jax.experimental import pallas as pl
from jax.experimental.pallas import tpu as pltpu

F32 = jnp.float32
BF16 = jnp.bfloat16

D_MODEL = 1024
DEPTH = 2
HG_HEADS = 8
HG_DK = 128
HG_DV = 128
LB_FLOOR = 1e-30
LRU_BLOCKS = 8
LRU_BS = 128
LRU_C = 8.0
CONV_W = 4
D_FF = 4 * D_MODEL
EPS = 1e-6
N_IN = 8 * D_MODEL

LANES = 128
SUBLANES = 8
HG_CHUNK = 128
HG_LEVELS = 7
MIX_TL = 256
FFN_TM = 512
SAMPLE_GROUP = 8
VMEM_LIMIT = 56 * 1024 * 1024


def _sigmoid(x):
    return 0.5 * jnp.tanh(0.5 * x) + 0.5


def _silu(x):
    h = 0.5 * x
    return h * jnp.tanh(h) + h


def _gelu_tanh(x):
    return 0.5 * x * (1.0 + jnp.tanh(0.7978845608028654 * (x + 0.044715 * (x * x * x))))


def _log1p(x):
    u = 1.0 + x
    return jnp.where(u == 1.0, x, jnp.log(u) * (x / (u - 1.0)))


def _softplus(z):
    return jnp.maximum(z, 0.0) + _log1p(jnp.exp(-jnp.abs(z)))


def _rmsnorm(x, g):
    ms = jnp.mean(x * x, axis=-1, keepdims=True)
    return x * lax.rsqrt(ms + EPS) * g


def _dot(a, b):
    return jnp.dot(a, b, preferred_element_type=F32)


def _dot_nt(a, b):
    return lax.dot_general(a, b, (((1,), (1,)), ((), ())), preferred_element_type=F32)


def _dot_tn(a, b):
    return lax.dot_general(a, b, (((0,), (0,)), ((), ())), preferred_element_type=F32)


def _lower_bound(hgl, layer):
    m = jnp.max(hgl, axis=0, keepdims=True)
    e = jnp.exp(hgl - m)
    p = e / jnp.sum(e, axis=0, keepdims=True)
    cum = p[0:1]
    for j in range(1, layer + 1):
        cum = cum + p[j:j + 1]
    return cum - p[0:1]


def _forget_key(fa, lb):
    e = jnp.exp(-jnp.abs(fa))
    r = 1.0 / (1.0 + e)
    er = e * r
    pos = fa >= 0.0
    sig_p = jnp.where(pos, r, er)
    sig_n = jnp.where(pos, er, r)
    f = jnp.maximum(lb, LB_FLOOR) + (1.0 - lb) * sig_p
    k = (1.0 - lb) * sig_n
    return f, k


def _lru_coeffs(xc, gates, ba, bx, sp):
    r = _sigmoid(gates[:, :LRU_BS] + ba)
    ig = _sigmoid(gates[:, LRU_BS:] + bx)
    log_a = (-LRU_C) * r * sp
    a = jnp.exp(log_a)
    th = jnp.tanh(log_a)
    one_m_a2 = (-2.0 * th) / (1.0 - th)
    return a, jnp.sqrt(one_m_a2) * (ig * xc)


def _mod_kernel(c_ref, w_ref, b_ref, o_ref):
    cs = _silu(c_ref[...]).astype(BF16)
    o_ref[0] = _dot(cs, w_ref[0].astype(BF16)) + b_ref[0]


def _mod_call(c_all, w_mod, b_mod):
    n_rows = c_all.shape[0]
    tn = 1536
    return pl.pallas_call(
        _mod_kernel,
        out_shape=jax.ShapeDtypeStruct((DEPTH, n_rows, 6 * D_MODEL), F32),
        grid=(DEPTH, 6 * D_MODEL // tn),
        in_specs=[
            pl.BlockSpec((n_rows, D_MODEL), lambda l, j: (0, 0)),
            pl.BlockSpec((1, D_MODEL, tn), lambda l, j: (l, 0, j)),
            pl.BlockSpec((1, 1, tn), lambda l, j: (l, 0, j)),
        ],
        out_specs=pl.BlockSpec((1, n_rows, tn), lambda l, j: (l, 0, j)),
        compiler_params=pltpu.CompilerParams(
            dimension_semantics=("arbitrary", "arbitrary"), vmem_limit_bytes=VMEM_LIMIT),
        name="mod",
    )(c_all, w_mod, b_mod.reshape(DEPTH, 1, 6 * D_MODEL))


def _hg_tables():
    t = np.arange(HG_CHUNK)
    x = t[:, None] ^ t[None, :]
    hb = np.floor(np.log2(np.maximum(x, 1))).astype(np.int32)
    lvl = np.where(t[:, None] > t[None, :], hb + 1, np.where(t[:, None] == t[None, :], 0, -1)).astype(np.int32)
    return jnp.asarray(lvl)


def _neg_abs(d):
    sign = jnp.uint32(0x80000000)
    return lax.bitcast_convert_type(lax.bitcast_convert_type(d, jnp.uint32) | sign, F32)


def _x_within_group(q, k, g2, level):
    n = q.shape[0]
    shp = (n // SUBLANES, SUBLANES, LANES)
    q3, k3, g3 = q.reshape(shp), k.reshape(shp), g2.reshape(shp)
    sub = lax.broadcasted_iota(jnp.int32, (1, SUBLANES, LANES), 1)
    qside = ((sub >> level) & 1) == 1
    if level == 0:
        scaled = jnp.where(qside, q3 * jnp.exp2(g3 - pltpu.roll(g3, 1, axis=1)), k3)
        return scaled.reshape(n, LANES)

    def row(i):
        return jnp.broadcast_to(g3[:, i:i + 1, :], shp)

    piv = jnp.where(sub < 4, row(1), row(5)) if level == 1 else row(3)
    e = jnp.exp2(_neg_abs(g3 - piv))
    return (jnp.where(qside, q3, k3) * e).reshape(n, LANES)


def _x_across_groups(q, k, g2, level):
    n = q.shape[0]
    b = 1 << level
    blocks = []
    for base in range(0, n, 2 * b):
        lo, mid, hi = base, base + b, base + 2 * b
        gp = g2[mid - 1:mid, :]
        blocks.append(k[lo:mid] * jnp.exp2(gp - g2[lo:mid]))
        blocks.append(q[mid:hi] * jnp.exp2(g2[mid:hi] - gp))
    return jnp.concatenate(blocks, axis=0)


HG_FULL_LEVELS = 4


def _hg_chunk(q, k, v, g2, st, lvl_ref):
    c = HG_CHUNK
    prods = [_dot_nt(q.astype(BF16), k.astype(BF16))]
    for level in range(HG_LEVELS):
        b = 1 << level
        if b < SUBLANES:
            x = _x_within_group(q, k, g2, level).astype(BF16)
        else:
            x = _x_across_groups(q, k, g2, level).astype(BF16)
        if level < HG_FULL_LEVELS:
            lhs = x
        else:
            lhs = jnp.concatenate([x[base + b:base + 2 * b] for base in range(0, c, 2 * b)], axis=0)
        prods.append(_dot_nt(lhs, x))
    groups = []
    for gi in range(c // SUBLANES):
        r0 = gi * SUBLANES
        lv = lvl_ref[r0:r0 + SUBLANES, :]
        a_g = jnp.where(lv == 0, prods[0][r0:r0 + SUBLANES], 0.0)
        for level in range(HG_LEVELS):
            b = 1 << level
            if level < HG_FULL_LEVELS:
                src = r0
            elif r0 & b:
                src = (r0 >> (level + 1)) * b + (r0 & (b - 1))
            else:
                continue
            a_g = jnp.where(lv == level + 1, prods[level + 1][src:src + SUBLANES], a_g)
        groups.append(a_g)
    a = jnp.concatenate(groups, axis=0).astype(BF16)
    g_last = g2[c - 1:c, :]
    qg = (q * jnp.exp2(g2)).astype(BF16)
    vb = v.astype(BF16)
    o = _dot_nt(qg, st.astype(BF16)) + _dot(a, vb)
    kd = (k * jnp.exp2(g_last - g2)).astype(BF16)
    st_new = st * jnp.exp2(g_last) + _dot_tn(vb, kd)
    return o, st_new


def _chunk_cumsum(y):
    t = y.shape[0]
    y3 = y.reshape(t // SUBLANES, SUBLANES, LANES)
    sub = lax.broadcasted_iota(jnp.int32, (1, SUBLANES, LANES), 1)
    d = 1
    while d < SUBLANES:
        y3 = y3 + jnp.where(sub >= d, pltpu.roll(y3, d, axis=1), 0.0)
        d *= 2
    per_chunk = HG_CHUNK // SUBLANES
    out = []
    for gi in range(t // SUBLANES):
        blk = y3[gi]
        if gi % per_chunk:
            blk = blk + out[-1][SUBLANES - 1:SUBLANES, :]
        out.append(blk)
    return jnp.concatenate(out, axis=0)


def _lru_scan(a, b, h0):
    t = a.shape[0]
    a3 = a.reshape(t // SUBLANES, SUBLANES, LANES)
    b3 = b.reshape(t // SUBLANES, SUBLANES, LANES)
    sub = lax.broadcasted_iota(jnp.int32, a3.shape, 1)
    d = 1
    while d < SUBLANES:
        m = sub >= d
        b3 = a3 * jnp.where(m, pltpu.roll(b3, d, axis=1), 0.0) + b3
        a3 = a3 * jnp.where(m, pltpu.roll(a3, d, axis=1), 1.0)
        d *= 2
    hs = []
    h = h0
    for gidx in range(t // SUBLANES):
        hg = a3[gidx] * h + b3[gidx]
        hs.append(hg)
        h = hg[SUBLANES - 1:SUBLANES, :]
    return jnp.concatenate(hs, axis=0), h


def _prompt_mixer_kernel(layer, n_t,
                         x_ref, mod_ref, hgl_ref, n1g_ref, win_ref, ong_ref, cw_ref, cb_ref, wax_ref,
                         ba_ref, bx_ref, lam_ref, wba_ref, wbb_ref, wout_ref, lvl_ref,
                         xo_ref, hg_ref, lru_ref, conv_ref,
                         q_s, k_s, v_s, g_s, og_s, oa_s, ob_s, xpad_s, st_s, hl_s):
    tl = MIX_TL
    d = D_MODEL
    t_idx = pl.program_id(1)

    @pl.when(t_idx == 0)
    def _():
        st_s[...] = jnp.zeros(st_s.shape, F32)
        hl_s[...] = jnp.zeros(hl_s.shape, F32)
        xpad_s[0:SUBLANES, :] = jnp.zeros((SUBLANES, d), F32)

    x = x_ref[0]
    mod = mod_ref[0]
    sh1, sc1, g1 = mod[:, 0:d], mod[:, d:2 * d], mod[:, 2 * d:3 * d]
    hb = (_rmsnorm(x, n1g_ref[...]) * (1.0 + sc1) + sh1).astype(BF16)

    def proj(j):
        return _dot(hb, win_ref[:, j * d:(j + 1) * d])

    def heads(val, dst):
        for h in range(HG_HEADS):
            dst[h] = val[:, h * LANES:(h + 1) * LANES]

    heads(_silu(proj(0)), q_s)
    lb = _lower_bound(hgl_ref[...], layer)
    f, k = _forget_key(proj(1), lb)
    heads(k, k_s)
    log2f = jnp.minimum(jnp.log2(f), 0.0)
    for h in range(HG_HEADS):
        g_s[h] = _chunk_cumsum(log2f[:, h * LANES:(h + 1) * LANES])
    heads(proj(2), v_s)
    heads(_silu(proj(3)), og_s)

    ong = ong_ref[...]

    def hg_body(c, carry):
        rows = pl.ds(pl.multiple_of(c * HG_CHUNK, HG_CHUNK), HG_CHUNK)
        for h in range(HG_HEADS):
            o, st_new = _hg_chunk(q_s[h, rows, :], k_s[h, rows, :], v_s[h, rows, :], g_s[h, rows, :],
                                  st_s[h], lvl_ref)
            st_s[h] = st_new
            oa_s[h, rows, :] = _rmsnorm(o, ong) * og_s[h, rows, :]
        return carry

    lax.fori_loop(0, tl // HG_CHUNK, hg_body, 0)

    xb = proj(4)
    xpad_s[SUBLANES:SUBLANES + tl, :] = xb
    cw = cw_ref[...]
    xc = cb_ref[...] + cw[3:4] * xb
    for j in range(CONV_W - 1):
        xc = xc + cw[j:j + 1] * xpad_s[pl.ds(SUBLANES - (CONV_W - 1) + j, tl), :]
    tail = xb[tl - (CONV_W - 1):tl, :]
    xpad_s[SUBLANES - (CONV_W - 1):SUBLANES, :] = tail
    conv_ref[0] = tail
    yb = _gelu_tanh(proj(5))
    sp = _softplus(-lam_ref[...])
    for n in range(LRU_BLOCKS):
        cols = slice(n * LRU_BS, (n + 1) * LRU_BS)
        xcn = xc[:, cols]
        gates = _dot(xcn.astype(BF16), wax_ref[n])
        a, bt = _lru_coeffs(xcn, gates, ba_ref[:, cols], bx_ref[:, cols], sp[:, cols])
        hseq, h_last = _lru_scan(a, bt, hl_s[:, cols])
        hl_s[:, cols] = h_last
        ob_s[:, cols] = hseq * yb[:, cols]
    lru_ref[0] = hl_s[...]

    o_a = jnp.concatenate([oa_s[h] for h in range(HG_HEADS)], axis=1).astype(BF16)
    merged = (_sigmoid(proj(6)) * _dot(o_a, wba_ref[...])
              + _sigmoid(proj(7)) * _dot(ob_s[...].astype(BF16), wbb_ref[...]))
    xo_ref[0] = x + g1 * _dot(merged.astype(BF16), wout_ref[...])

    @pl.when(t_idx == n_t - 1)
    def _():
        for h in range(HG_HEADS):
            hg_ref[0, h] = st_s[h].T


def _const_spec(shape):
    nd = len(shape)
    return pl.BlockSpec(shape, lambda *_: (0,) * nd, pipeline_mode=pl.Buffered(1))


def _prompt_mixer_call(layer, x, mod_p, hg_lower, wl, lvl):
    bsz, seq, d = x.shape
    tl = MIX_TL
    n_t = seq // tl
    hshape = (HG_HEADS, tl, LANES)
    out_shapes = (
        jax.ShapeDtypeStruct((bsz, seq, d), F32),
        jax.ShapeDtypeStruct((bsz, HG_HEADS, HG_DK, HG_DV), F32),
        jax.ShapeDtypeStruct((bsz, 1, d), F32),
        jax.ShapeDtypeStruct((bsz, CONV_W - 1, d), F32),
    )
    in_specs = [
        pl.BlockSpec((1, tl, d), lambda b, t: (b, t, 0)),
        pl.BlockSpec((1, 1, 6 * d), lambda b, t: (b, 0, 0)),
        _const_spec((DEPTH, d)),
        _const_spec((1, d)),
        _const_spec((d, N_IN)),
        _const_spec((1, HG_DV)),
        _const_spec((CONV_W, d)),
        _const_spec((1, d)),
        _const_spec((LRU_BLOCKS, LRU_BS, 2 * LRU_BS)),
        _const_spec((1, d)),
        _const_spec((1, d)),
        _const_spec((1, d)),
        _const_spec((d, d)),
        _const_spec((d, d)),
        _const_spec((d, d)),
        _const_spec((HG_CHUNK, HG_CHUNK)),
    ]
    out_specs = (
        pl.BlockSpec((1, tl, d), lambda b, t: (b, t, 0)),
        pl.BlockSpec((1, HG_HEADS, HG_DK, HG_DV), lambda b, t: (b, 0, 0, 0)),
        pl.BlockSpec((1, 1, d), lambda b, t: (b, 0, 0)),
        pl.BlockSpec((1, CONV_W - 1, d), lambda b, t: (b, 0, 0)),
    )
    scratch = [
        pltpu.VMEM(hshape, F32),
        pltpu.VMEM(hshape, F32),
        pltpu.VMEM(hshape, F32),
        pltpu.VMEM(hshape, F32),
        pltpu.VMEM(hshape, F32),
        pltpu.VMEM(hshape, F32),
        pltpu.VMEM((tl, d), F32),
        pltpu.VMEM((tl + SUBLANES, d), F32),
        pltpu.VMEM((HG_HEADS, HG_DV, HG_DK), F32),
        pltpu.VMEM((1, d), F32),
    ]
    return pl.pallas_call(
        functools.partial(_prompt_mixer_kernel, layer, n_t),
        out_shape=out_shapes,
        grid=(bsz, n_t),
        in_specs=in_specs,
        out_specs=out_specs,
        scratch_shapes=scratch,
        compiler_params=pltpu.CompilerParams(
            dimension_semantics=("arbitrary", "arbitrary"), vmem_limit_bytes=VMEM_LIMIT),
        name=f"prompt_mixer_{layer}",
    )(x, mod_p, hg_lower, wl["n1g"], wl["w_in"], wl["ong"], wl["cw"], wl["cb"], wl["wax"],
      wl["ba"], wl["bx"], wl["lam"], wl["wba"], wl["wbb"], wl["wout"], lvl)


def _ffn_kernel(final, x_ref, mod_ref, n2g_ref, w1_ref, w2_ref, fg_ref, o_ref):
    d = D_MODEL
    x = x_ref[0]
    mod = mod_ref[0]
    sh2, sc2, g2 = mod[:, 3 * d:4 * d], mod[:, 4 * d:5 * d], mod[:, 5 * d:6 * d]
    hb = (_rmsnorm(x, n2g_ref[...]) * (1.0 + sc2) + sh2).astype(BF16)
    acc = jnp.zeros(x.shape, F32)
    for c in range(D_FF // d):
        h1 = jnp.maximum(_dot(hb, w1_ref[:, c * d:(c + 1) * d]), 0.0)
        acc = acc + _dot((h1 * h1).astype(BF16), w2_ref[c * d:(c + 1) * d, :])
    xn = x + g2 * acc
    if final:
        xn = _rmsnorm(xn, fg_ref[...])
    o_ref[0] = xn


def _ffn_call(final, x, mod, wl, final_g, tm, name):
    bsz, seq, d = x.shape
    mrows = mod.shape[1]
    mod_spec = (pl.BlockSpec((1, 1, 6 * d), lambda b, t: (b, 0, 0)) if mrows == 1
                else pl.BlockSpec((1, tm, 6 * d), lambda b, t: (b, t, 0)))
    return pl.pallas_call(
        functools.partial(_ffn_kernel, final),
        out_shape=jax.ShapeDtypeStruct((bsz, seq, d), F32),
        grid=(bsz, seq // tm),
        in_specs=[
            pl.BlockSpec((1, tm, d), lambda b, t: (b, t, 0)),
            mod_spec,
            _const_spec((1, d)),
            _const_spec((d, D_FF)),
            _const_spec((D_FF, d)),
            _const_spec((1, d)),
        ],
        out_specs=pl.BlockSpec((1, tm, d), lambda b, t: (b, t, 0)),
        compiler_params=pltpu.CompilerParams(
            dimension_semantics=("arbitrary", "arbitrary"), vmem_limit_bytes=VMEM_LIMIT),
        name=name,
    )(x, mod, wl["n2g"], wl["w1"], wl["w2"], final_g)


def _sample_inproj_kernel(x_ref, mod_ref, n1g_ref, w_ref, o_ref):
    d = D_MODEL
    mod = mod_ref[...]
    hb = (_rmsnorm(x_ref[...], n1g_ref[...]) * (1.0 + mod[:, d:2 * d]) + mod[:, 0:d]).astype(BF16)
    o_ref[...] = _dot(hb, w_ref[...])


def _sample_inproj_call(layer, x, mod_s, wl):
    n, d = x.shape
    return pl.pallas_call(
        _sample_inproj_kernel,
        out_shape=jax.ShapeDtypeStruct((n, N_IN), F32),
        grid=(N_IN // d,),
        in_specs=[
            pl.BlockSpec((n, d), lambda j: (0, 0)),
            pl.BlockSpec((n, 6 * d), lambda j: (0, 0)),
            pl.BlockSpec((1, d), lambda j: (0, 0)),
            pl.BlockSpec((d, d), lambda j: (0, j)),
        ],
        out_specs=pl.BlockSpec((n, d), lambda j: (0, j)),
        compiler_params=pltpu.CompilerParams(
            dimension_semantics=("arbitrary",), vmem_limit_bytes=VMEM_LIMIT),
        name=f"sample_inproj_{layer}",
    )(x, mod_s, wl["n1g"], wl["w_in"])


def _sample_mixer_kernel(layer, n_groups,
                         u_ref, x_ref, mod_ref, hg_ref, lru_ref, conv_ref, hgl_ref, ong_ref, cw_ref, cb_ref,
                         wax_ref, ba_ref, bx_ref, lam_ref, wba_ref, wbb_ref, wout_ref,
                         xo_ref, hgo_ref, lruo_ref, convo_ref,
                         ft_s, kt_s, qt_s, v_s, o_s, ob_s):
    d = D_MODEL
    n = x_ref.shape[0]
    i = pl.program_id(0)

    @pl.when(i == 0)
    def _():
        q = _silu(u_ref[:, 0:d])
        lb = _lower_bound(hgl_ref[...], layer)
        f, k = _forget_key(u_ref[:, d:2 * d], lb)
        for h in range(HG_HEADS):
            cols = slice(h * LANES, (h + 1) * LANES)
            ft_s[h] = f[:, cols].T
            kt_s[h] = k[:, cols].T
            qt_s[h] = q[:, cols].T
        v_s[...] = u_ref[:, 2 * d:3 * d]
        xb = u_ref[:, 4 * d:5 * d]
        cw = cw_ref[...]
        xc = cb_ref[...] + cw[3:4] * xb
        for j in range(CONV_W - 1):
            xc = xc + cw[j:j + 1] * conv_ref[j]
        convo_ref[0] = conv_ref[1]
        convo_ref[1] = conv_ref[2]
        convo_ref[2] = xb
        yb = _gelu_tanh(u_ref[:, 5 * d:6 * d])
        sp = _softplus(-lam_ref[...])
        for blk in range(LRU_BLOCKS):
            cols = slice(blk * LRU_BS, (blk + 1) * LRU_BS)
            xcn = xc[:, cols]
            gates = _dot(xcn.astype(BF16), wax_ref[blk])
            a, bt = _lru_coeffs(xcn, gates, ba_ref[:, cols], bx_ref[:, cols], sp[:, cols])
            hn = a * lru_ref[:, cols] + bt
            lruo_ref[:, cols] = hn
            ob_s[:, cols] = hn * yb[:, cols]

    shift = (n - i * SAMPLE_GROUP) % n
    rows = pl.ds(pl.multiple_of(i * SAMPLE_GROUP, SAMPLE_GROUP), SAMPLE_GROUP)
    for h in range(HG_HEADS):
        cols = slice(h * LANES, (h + 1) * LANES)
        fr = pltpu.roll(ft_s[h], shift, axis=1)
        kr = pltpu.roll(kt_s[h], shift, axis=1)
        qr = pltpu.roll(qt_s[h], shift, axis=1)
        vg = v_s[rows, cols]
        o_rows = []
        for j in range(SAMPLE_GROUP):
            s_new = fr[:, j:j + 1] * hg_ref[j, h] + kr[:, j:j + 1] * vg[j:j + 1, :]
            hgo_ref[j, h] = s_new
            o_rows.append(jnp.sum(qr[:, j:j + 1] * s_new, axis=0, keepdims=True))
        o_s[rows, cols] = jnp.concatenate(o_rows, axis=0)

    @pl.when(i == n_groups - 1)
    def _():
        ong = ong_ref[...]
        parts = []
        for h in range(HG_HEADS):
            cols = slice(h * LANES, (h + 1) * LANES)
            parts.append(_rmsnorm(o_s[:, cols], ong) * _silu(u_ref[:, 3 * d + h * LANES:3 * d + (h + 1) * LANES]))
        o_a = jnp.concatenate(parts, axis=1).astype(BF16)
        merged = (_sigmoid(u_ref[:, 6 * d:7 * d]) * _dot(o_a, wba_ref[...])
                  + _sigmoid(u_ref[:, 7 * d:8 * d]) * _dot(ob_s[...].astype(BF16), wbb_ref[...]))
        g1 = mod_ref[:, 2 * d:3 * d]
        xo_ref[...] = x_ref[...] + g1 * _dot(merged.astype(BF16), wout_ref[...])


def _sample_mixer_call(layer, u, x, mod_s, st_hg, st_lru, st_conv_t, hg_lower, wl):
    n, d = x.shape
    grp = SAMPLE_GROUP
    n_groups = n // grp

    def full(shape):
        nd = len(shape)
        return pl.BlockSpec(shape, lambda i: (0,) * nd)

    st_spec = pl.BlockSpec((grp, HG_HEADS, HG_DK, HG_DV), lambda i: (i, 0, 0, 0))
    out_shapes = (
        jax.ShapeDtypeStruct((n, d), F32),
        jax.ShapeDtypeStruct((n, HG_HEADS, HG_DK, HG_DV), F32),
        jax.ShapeDtypeStruct((n, d), F32),
        jax.ShapeDtypeStruct((CONV_W - 1, n, d), F32),
    )
    in_specs = [
        full((n, N_IN)), full((n, d)), full((n, 6 * d)), st_spec, full((n, d)), full((CONV_W - 1, n, d)),
        full((DEPTH, d)), full((1, HG_DV)), full((CONV_W, d)), full((1, d)),
        full((LRU_BLOCKS, LRU_BS, 2 * LRU_BS)), full((1, d)), full((1, d)), full((1, d)),
        full((d, d)), full((d, d)), full((d, d)),
    ]
    out_specs = (full((n, d)), st_spec, full((n, d)), full((CONV_W - 1, n, d)))
    scratch = [
        pltpu.VMEM((HG_HEADS, HG_DK, n), F32),
        pltpu.VMEM((HG_HEADS, HG_DK, n), F32),
        pltpu.VMEM((HG_HEADS, HG_DK, n), F32),
        pltpu.VMEM((n, d), F32),
        pltpu.VMEM((n, d), F32),
        pltpu.VMEM((n, d), F32),
    ]
    return pl.pallas_call(
        functools.partial(_sample_mixer_kernel, layer, n_groups),
        out_shape=out_shapes,
        grid=(n_groups,),
        in_specs=in_specs,
        out_specs=out_specs,
        scratch_shapes=scratch,
        compiler_params=pltpu.CompilerParams(
            dimension_semantics=("arbitrary",), vmem_limit_bytes=VMEM_LIMIT),
        name=f"sample_mixer_{layer}",
    )(u, x, mod_s, st_hg, st_lru, st_conv_t, hg_lower, wl["ong"], wl["cw"], wl["cb"], wl["wax"],
      wl["ba"], wl["bx"], wl["lam"], wl["wba"], wl["wbb"], wl["wout"])


def _layer_weights(l, norm1_g, norm2_g, w_in, hg_onorm_g, lru_conv_w, lru_conv_b, lru_wa, lru_ba, lru_wx,
                   lru_bx, lru_lambda, w_branch_a, w_branch_b, w_out, w_ff1, w_ff2):
    d = D_MODEL
    return {
        "n1g": norm1_g[l].reshape(1, d),
        "n2g": norm2_g[l].reshape(1, d),
        "w_in": w_in[l].astype(BF16),
        "ong": hg_onorm_g[l].reshape(1, HG_DV),
        "cw": lru_conv_w[l],
        "cb": lru_conv_b[l].reshape(1, d),
        "wax": jnp.concatenate([lru_wa[l], lru_wx[l]], axis=-1).astype(BF16),
        "ba": lru_ba[l].reshape(1, d),
        "bx": lru_bx[l].reshape(1, d),
        "lam": lru_lambda[l].reshape(1, d),
        "wba": w_branch_a[l].astype(BF16),
        "wbb": w_branch_b[l].astype(BF16),
        "wout": w_out[l].astype(BF16),
        "w1": w_ff1[l].astype(BF16),
        "w2": w_ff2[l].astype(BF16),
    }


def kernel(x_prompt, x_sample, state_hgrn, state_rglru, state_conv, c_prompt, c_sample, w_mod, b_mod, norm1_g, norm2_g, w_in, hg_lower, hg_onorm_g, lru_conv_w, lru_conv_b, lru_wa, lru_ba, lru_wx, lru_bx, lru_lambda, w_branch_a, w_branch_b, w_out, w_ff1, w_ff2, final_norm_g):
    bsz, seq, d = x_prompt.shape
    n_s = x_sample.shape[0]
    assert d == D_MODEL and x_sample.shape[1] == 1 and seq % MIX_TL == 0 and seq % FFN_TM == 0
    assert n_s % SAMPLE_GROUP == 0 and n_s == LANES

    mod = _mod_call(jnp.concatenate([c_prompt, c_sample], axis=0), w_mod, b_mod)
    lvl = _hg_tables()
    final_g = final_norm_g.reshape(1, d)

    xp = x_prompt
    xs = x_sample.reshape(n_s, d)
    hg_p, lru_p, conv_p, hg_s, lru_s, conv_s = [], [], [], [], [], []
    for l in range(DEPTH):
        wl = _layer_weights(l, norm1_g, norm2_g, w_in, hg_onorm_g, lru_conv_w, lru_conv_b, lru_wa, lru_ba,
                            lru_wx, lru_bx, lru_lambda, w_branch_a, w_branch_b, w_out, w_ff1, w_ff2)
        final = l == DEPTH - 1
        mod_p = mod[l, :bsz].reshape(bsz, 1, 6 * d)
        mod_s = mod[l, bsz:]

        xp, hg, lru, conv = _prompt_mixer_call(l, xp, mod_p, hg_lower, wl, lvl)
        xp = _ffn_call(final, xp, mod_p, wl, final_g, FFN_TM, f"prompt_ffn_{l}")
        hg_p.append(hg)
        lru_p.append(lru.reshape(bsz, d))
        conv_p.append(conv)

        u = _sample_inproj_call(l, xs, mod_s, wl)
        xs, hg, lru, conv = _sample_mixer_call(l, u, xs, mod_s, state_hgrn[l], state_rglru[l],
                                               jnp.swapaxes(state_conv[l], 0, 1), hg_lower, wl)
        xs = _ffn_call(final, xs.reshape(1, n_s, d), mod_s.reshape(1, n_s, 6 * d), wl, final_g, n_s,
                       f"sample_ffn_{l}").reshape(n_s, d)
        hg_s.append(hg)
        lru_s.append(lru)
        conv_s.append(jnp.swapaxes(conv, 0, 1))

    return (xp, xs.reshape(n_s, 1, d),
            jnp.stack(hg_p), jnp.stack(lru_p), jnp.stack(conv_p),
            jnp.stack(hg_s), jnp.stack(lru_s), jnp.stack(conv_s))
```

```python
import functools

import numpy as np
import jax
import jax.numpy as jnp
from jax import lax
from jax.experimental import pallas as pl
from jax.experimental.pallas import tpu as pltpu

F32 = jnp.float32
BF16 = jnp.bfloat16

D_MODEL = 1024
DEPTH = 2
HG_HEADS = 8
HG_DK = 128
HG_DV = 128
LB_FLOOR = 1e-30
LRU_BLOCKS = 8
LRU_BS = 128
LRU_C = 8.0
CONV_W = 4
D_FF = 4 * D_MODEL
EPS = 1e-6
N_IN = 8 * D_MODEL

LANES = 128
SUBLANES = 8
HG_CHUNK = 128
HG_LEVELS = 7
MIX_TL = 256
FFN_TM = 512
SAMPLE_GROUP = 8
VMEM_LIMIT = 56 * 1024 * 1024


def _sigmoid(x):
    return 0.5 * jnp.tanh(0.5 * x) + 0.5


def _silu(x):
    h = 0.5 * x
    return h * jnp.tanh(h) + h


def _gelu_tanh(x):
    return 0.5 * x * (1.0 + jnp.tanh(0.7978845608028654 * (x + 0.044715 * (x * x * x))))


def _log1p(x):
    u = 1.0 + x
    return jnp.where(u == 1.0, x, jnp.log(u) * (x / (u - 1.0)))


def _softplus(z):
    return jnp.maximum(z, 0.0) + _log1p(jnp.exp(-jnp.abs(z)))


def _rmsnorm(x, g):
    ms = jnp.mean(x * x, axis=-1, keepdims=True)
    return x * lax.rsqrt(ms + EPS) * g


def _dot(a, b):
    return jnp.dot(a, b, preferred_element_type=F32)


def _dot_nt(a, b):
    return lax.dot_general(a, b, (((1,), (1,)), ((), ())), preferred_element_type=F32)


def _dot_tn(a, b):
    return lax.dot_general(a, b, (((0,), (0,)), ((), ())), preferred_element_type=F32)


def _lower_bound(hgl, layer):
    m = jnp.max(hgl, axis=0, keepdims=True)
    e = jnp.exp(hgl - m)
    p = e / jnp.sum(e, axis=0, keepdims=True)
    cum = p[0:1]
    for j in range(1, layer + 1):
        cum = cum + p[j:j + 1]
    return cum - p[0:1]


def _forget_key(fa, lb):
    e = jnp.exp(-jnp.abs(fa))
    r = 1.0 / (1.0 + e)
    er = e * r
    pos = fa >= 0.0
    sig_p = jnp.where(pos, r, er)
    sig_n = jnp.where(pos, er, r)
    f = jnp.maximum(lb, LB_FLOOR) + (1.0 - lb) * sig_p
    k = (1.0 - lb) * sig_n
    return f, k


def _lru_coeffs(xc, gates, ba, bx, sp):
    r = _sigmoid(gates[:, :LRU_BS] + ba)
    ig = _sigmoid(gates[:, LRU_BS:] + bx)
    log_a = (-LRU_C) * r * sp
    a = jnp.exp(log_a)
    th = jnp.tanh(log_a)
    one_m_a2 = (-2.0 * th) / (1.0 - th)
    return a, jnp.sqrt(one_m_a2) * (ig * xc)


def _mod_kernel(c_ref, w_ref, b_ref, o_ref):
    cs = _silu(c_ref[...]).astype(BF16)
    o_ref[0] = _dot(cs, w_ref[0].astype(BF16)) + b_ref[0]


def _mod_call(c_all, w_mod, b_mod):
    n_rows = c_all.shape[0]
    tn = 1536
    return pl.pallas_call(
        _mod_kernel,
        out_shape=jax.ShapeDtypeStruct((DEPTH, n_rows, 6 * D_MODEL), F32),
        grid=(DEPTH, 6 * D_MODEL // tn),
        in_specs=[
            pl.BlockSpec((n_rows, D_MODEL), lambda l, j: (0, 0)),
            pl.BlockSpec((1, D_MODEL, tn), lambda l, j: (l, 0, j)),
            pl.BlockSpec((1, 1, tn), lambda l, j: (l, 0, j)),
        ],
        out_specs=pl.BlockSpec((1, n_rows, tn), lambda l, j: (l, 0, j)),
        compiler_params=pltpu.CompilerParams(
            dimension_semantics=("arbitrary", "arbitrary"), vmem_limit_bytes=VMEM_LIMIT),
        name="mod",
    )(c_all, w_mod, b_mod.reshape(DEPTH, 1, 6 * D_MODEL))


def _hg_tables():
    t = np.arange(HG_CHUNK)
    x = t[:, None] ^ t[None, :]
    hb = np.floor(np.log2(np.maximum(x, 1))).astype(np.int32)
    lvl = np.where(t[:, None] > t[None, :], hb + 1, np.where(t[:, None] == t[None, :], 0, -1)).astype(np.int32)
    return jnp.asarray(lvl)


def _x_within_group(q, k, g2, level):
    n = q.shape[0]
    shp = (n // SUBLANES, SUBLANES, LANES)
    q3, k3, g3 = q.reshape(shp), k.reshape(shp), g2.reshape(shp)
    sub = lax.broadcasted_iota(jnp.int32, (1, SUBLANES, LANES), 1)
    qside = ((sub >> level) & 1) == 1
    if level == 0:
        scaled = jnp.where(qside, q3 * jnp.exp2(g3 - pltpu.roll(g3, 1, axis=1)), k3)
        return scaled.reshape(n, LANES)

    def row(i):
        return jnp.broadcast_to(g3[:, i:i + 1, :], shp)

    piv = jnp.where(sub < 4, row(1), row(5)) if level == 1 else row(3)
    e = jnp.exp2(-jnp.abs(g3 - piv))
    return (jnp.where(qside, q3, k3) * e).reshape(n, LANES)


def _x_across_groups(q, k, g2, level):
    n = q.shape[0]
    b = 1 << level
    blocks = []
    for base in range(0, n, 2 * b):
        lo, mid, hi = base, base + b, base + 2 * b
        gp = g2[mid - 1:mid, :]
        blocks.append(k[lo:mid] * jnp.exp2(gp - g2[lo:mid]))
        blocks.append(q[mid:hi] * jnp.exp2(g2[mid:hi] - gp))
    return jnp.concatenate(blocks, axis=0)


HG_FULL_LEVELS = 4


def _hg_chunk(q, k, v, g2, st, lvl_ref):
    c = HG_CHUNK
    prods = [_dot_nt(q.astype(BF16), k.astype(BF16))]
    for level in range(HG_LEVELS):
        b = 1 << level
        if b < SUBLANES:
            x = _x_within_group(q, k, g2, level).astype(BF16)
        else:
            x = _x_across_groups(q, k, g2, level).astype(BF16)
        if level < HG_FULL_LEVELS:
            lhs = x
        else:
            lhs = jnp.concatenate([x[base + b:base + 2 * b] for base in range(0, c, 2 * b)], axis=0)
        prods.append(_dot_nt(lhs, x))
    groups = []
    for gi in range(c // SUBLANES):
        r0 = gi * SUBLANES
        lv = lvl_ref[r0:r0 + SUBLANES, :]
        a_g = jnp.where(lv == 0, prods[0][r0:r0 + SUBLANES], 0.0)
        for level in range(HG_LEVELS):
            b = 1 << level
            if level < HG_FULL_LEVELS:
                src = r0
            elif r0 & b:
                src = (r0 >> (level + 1)) * b + (r0 & (b - 1))
            else:
                continue
            a_g = jnp.where(lv == level + 1, prods[level + 1][src:src + SUBLANES], a_g)
        groups.append(a_g)
    a = jnp.concatenate(groups, axis=0).astype(BF16)
    g_last = g2[c - 1:c, :]
    qg = (q * jnp.exp2(g2)).astype(BF16)
    vb = v.astype(BF16)
    o = _dot_nt(qg, st.astype(BF16)) + _dot(a, vb)
    kd = (k * jnp.exp2(g_last - g2)).astype(BF16)
    st_new = st * jnp.exp2(g_last) + _dot_tn(vb, kd)
    return o, st_new


def _chunk_cumsum(y):
    t = y.shape[0]
    y3 = y.reshape(t // SUBLANES, SUBLANES, LANES)
    sub = lax.broadcasted_iota(jnp.int32, (1, SUBLANES, LANES), 1)
    d = 1
    while d < SUBLANES:
        y3 = y3 + jnp.where(sub >= d, pltpu.roll(y3, d, axis=1), 0.0)
        d *= 2
    per_chunk = HG_CHUNK // SUBLANES
    out = []
    for gi in range(t // SUBLANES):
        blk = y3[gi]
        if gi % per_chunk:
            blk = blk + out[-1][SUBLANES - 1:SUBLANES, :]
        out.append(blk)
    return jnp.concatenate(out, axis=0)


def _lru_scan(a, b, h0):
    t = a.shape[0]
    a3 = a.reshape(t // SUBLANES, SUBLANES, LANES)
    b3 = b.reshape(t // SUBLANES, SUBLANES, LANES)
    sub = lax.broadcasted_iota(jnp.int32, a3.shape, 1)
    d = 1
    while d < SUBLANES:
        m = sub >= d
        b3 = a3 * jnp.where(m, pltpu.roll(b3, d, axis=1), 0.0) + b3
        a3 = a3 * jnp.where(m, pltpu.roll(a3, d, axis=1), 1.0)
        d *= 2
    hs = []
    h = h0
    for gidx in range(t // SUBLANES):
        hg = a3[gidx] * h + b3[gidx]
        hs.append(hg)
        h = hg[SUBLANES - 1:SUBLANES, :]
    return jnp.concatenate(hs, axis=0), h


def _prompt_mixer_kernel(layer, n_t,
                         x_ref, mod_ref, hgl_ref, n1g_ref, win_ref, ong_ref, cw_ref, cb_ref, wax_ref,
                         ba_ref, bx_ref, lam_ref, wba_ref, wbb_ref, wout_ref, lvl_ref,
                         xo_ref, hg_ref, lru_ref, conv_ref,
                         q_s, k_s, v_s, g_s, og_s, oa_s, ob_s, xpad_s, st_s, hl_s):
    tl = MIX_TL
    d = D_MODEL
    t_idx = pl.program_id(1)

    @pl.when(t_idx == 0)
    def _():
        st_s[...] = jnp.zeros(st_s.shape, F32)
        hl_s[...] = jnp.zeros(hl_s.shape, F32)
        xpad_s[0:SUBLANES, :] = jnp.zeros((SUBLANES, d), F32)

    x = x_ref[0]
    mod = mod_ref[0]
    sh1, sc1, g1 = mod[:, 0:d], mod[:, d:2 * d], mod[:, 2 * d:3 * d]
    hb = (_rmsnorm(x, n1g_ref[...]) * (1.0 + sc1) + sh1).astype(BF16)

    def proj(j):
        return _dot(hb, win_ref[:, j * d:(j + 1) * d])

    def heads(val, dst):
        for h in range(HG_HEADS):
            dst[h] = val[:, h * LANES:(h + 1) * LANES]

    heads(_silu(proj(0)), q_s)
    lb = _lower_bound(hgl_ref[...], layer)
    f, k = _forget_key(proj(1), lb)
    heads(k, k_s)
    log2f = jnp.minimum(jnp.log2(f), 0.0)
    for h in range(HG_HEADS):
        g_s[h] = _chunk_cumsum(log2f[:, h * LANES:(h + 1) * LANES])
    heads(proj(2), v_s)
    heads(_silu(proj(3)), og_s)

    ong = ong_ref[...]

    for c in range(tl // HG_CHUNK):
        rows = slice(c * HG_CHUNK, (c + 1) * HG_CHUNK)
        for h in range(HG_HEADS):
            o, st_new = _hg_chunk(q_s[h, rows, :], k_s[h, rows, :], v_s[h, rows, :], g_s[h, rows, :],
                                  st_s[h], lvl_ref)
            st_s[h] = st_new
            oa_s[h, rows, :] = _rmsnorm(o, ong) * og_s[h, rows, :]

    xb = proj(4)
    xpad_s[SUBLANES:SUBLANES + tl, :] = xb
    cw = cw_ref[...]
    xc = cb_ref[...] + cw[3:4] * xb
    for j in range(CONV_W - 1):
        xc = xc + cw[j:j + 1] * xpad_s[pl.ds(SUBLANES - (CONV_W - 1) + j, tl), :]
    tail = xb[tl - (CONV_W - 1):tl, :]
    xpad_s[SUBLANES - (CONV_W - 1):SUBLANES, :] = tail
    conv_ref[0] = tail
    yb = _gelu_tanh(proj(5))
    sp = _softplus(-lam_ref[...])
    for n in range(LRU_BLOCKS):
        cols = slice(n * LRU_BS, (n + 1) * LRU_BS)
        xcn = xc[:, cols]
        gates = _dot(xcn.astype(BF16), wax_ref[n])
        a, bt = _lru_coeffs(xcn, gates, ba_ref[:, cols], bx_ref[:, cols], sp[:, cols])
        hseq, h_last = _lru_scan(a, bt, hl_s[:, cols])
        hl_s[:, cols] = h_last
        ob_s[:, cols] = hseq * yb[:, cols]
    lru_ref[0] = hl_s[...]

    o_a = jnp.concatenate([oa_s[h] for h in range(HG_HEADS)], axis=1).astype(BF16)
    merged = (_sigmoid(proj(6)) * _dot(o_a, wba_ref[...])
              + _sigmoid(proj(7)) * _dot(ob_s[...].astype(BF16), wbb_ref[...]))
    xo_ref[0] = x + g1 * _dot(merged.astype(BF16), wout_ref[...])

    @pl.when(t_idx == n_t - 1)
    def _():
        for h in range(HG_HEADS):
            hg_ref[0, h] = st_s[h].T


def _const_spec(shape):
    nd = len(shape)
    return pl.BlockSpec(shape, lambda *_: (0,) * nd, pipeline_mode=pl.Buffered(1))


def _prompt_mixer_call(layer, x, mod_p, hg_lower, wl, lvl):
    bsz, seq, d = x.shape
    tl = MIX_TL
    n_t = seq // tl
    hshape = (HG_HEADS, tl, LANES)
    out_shapes = (
        jax.ShapeDtypeStruct((bsz, seq, d), F32),
        jax.ShapeDtypeStruct((bsz, HG_HEADS, HG_DK, HG_DV), F32),
        jax.ShapeDtypeStruct((bsz, 1, d), F32),
        jax.ShapeDtypeStruct((bsz, CONV_W - 1, d), F32),
    )
    in_specs = [
        pl.BlockSpec((1, tl, d), lambda b, t: (b, t, 0)),
        pl.BlockSpec((1, 1, 6 * d), lambda b, t: (b, 0, 0)),
        _const_spec((DEPTH, d)),
        _const_spec((1, d)),
        _const_spec((d, N_IN)),
        _const_spec((1, HG_DV)),
        _const_spec((CONV_W, d)),
        _const_spec((1, d)),
        _const_spec((LRU_BLOCKS, LRU_BS, 2 * LRU_BS)),
        _const_spec((1, d)),
        _const_spec((1, d)),
        _const_spec((1, d)),
        _const_spec((d, d)),
        _const_spec((d, d)),
        _const_spec((d, d)),
        _const_spec((HG_CHUNK, HG_CHUNK)),
    ]
    out_specs = (
        pl.BlockSpec((1, tl, d), lambda b, t: (b, t, 0)),
        pl.BlockSpec((1, HG_HEADS, HG_DK, HG_DV), lambda b, t: (b, 0, 0, 0)),
        pl.BlockSpec((1, 1, d), lambda b, t: (b, 0, 0)),
        pl.BlockSpec((1, CONV_W - 1, d), lambda b, t: (b, 0, 0)),
    )
    scratch = [
        pltpu.VMEM(hshape, F32),
        pltpu.VMEM(hshape, F32),
        pltpu.VMEM(hshape, F32),
        pltpu.VMEM(hshape, F32),
        pltpu.VMEM(hshape, F32),
        pltpu.VMEM(hshape, F32),
        pltpu.VMEM((tl, d), F32),
        pltpu.VMEM((tl + SUBLANES, d), F32),
        pltpu.VMEM((HG_HEADS, HG_DV, HG_DK), F32),
        pltpu.VMEM((1, d), F32),
    ]
    return pl.pallas_call(
        functools.partial(_prompt_mixer_kernel, layer, n_t),
        out_shape=out_shapes,
        grid=(bsz, n_t),
        in_specs=in_specs,
        out_specs=out_specs,
        scratch_shapes=scratch,
        compiler_params=pltpu.CompilerParams(
            dimension_semantics=("arbitrary", "arbitrary"), vmem_limit_bytes=VMEM_LIMIT),
        name=f"prompt_mixer_{layer}",
    )(x, mod_p, hg_lower, wl["n1g"], wl["w_in"], wl["ong"], wl["cw"], wl["cb"], wl["wax"],
      wl["ba"], wl["bx"], wl["lam"], wl["wba"], wl["wbb"], wl["wout"], lvl)


def _ffn_kernel(final, x_ref, mod_ref, n2g_ref, w1_ref, w2_ref, fg_ref, o_ref):
    d = D_MODEL
    x = x_ref[0]
    mod = mod_ref[0]
    sh2, sc2, g2 = mod[:, 3 * d:4 * d], mod[:, 4 * d:5 * d], mod[:, 5 * d:6 * d]
    hb = (_rmsnorm(x, n2g_ref[...]) * (1.0 + sc2) + sh2).astype(BF16)
    acc = jnp.zeros(x.shape, F32)
    for c in range(D_FF // d):
        h1 = jnp.maximum(_dot(hb, w1_ref[:, c * d:(c + 1) * d]), 0.0)
        acc = acc + _dot((h1 * h1).astype(BF16), w2_ref[c * d:(c + 1) * d, :])
    xn = x + g2 * acc
    if final:
        xn = _rmsnorm(xn, fg_ref[...])
    o_ref[0] = xn


def _ffn_call(final, x, mod, wl, final_g, tm, name):
    bsz, seq, d = x.shape
    mrows = mod.shape[1]
    mod_spec = (pl.BlockSpec((1, 1, 6 * d), lambda b, t: (b, 0, 0)) if mrows == 1
                else pl.BlockSpec((1, tm, 6 * d), lambda b, t: (b, t, 0)))
    return pl.pallas_call(
        functools.partial(_ffn_kernel, final),
        out_shape=jax.ShapeDtypeStruct((bsz, seq, d), F32),
        grid=(bsz, seq // tm),
        in_specs=[
            pl.BlockSpec((1, tm, d), lambda b, t: (b, t, 0)),
            mod_spec,
            _const_spec((1, d)),
            _const_spec((d, D_FF)),
            _const_spec((D_FF, d)),
            _const_spec((1, d)),
        ],
        out_specs=pl.BlockSpec((1, tm, d), lambda b, t: (b, t, 0)),
        compiler_params=pltpu.CompilerParams(
            dimension_semantics=("arbitrary", "arbitrary"), vmem_limit_bytes=VMEM_LIMIT),
        name=name,
    )(x, mod, wl["n2g"], wl["w1"], wl["w2"], final_g)


def _sample_inproj_kernel(x_ref, mod_ref, n1g_ref, w_ref, o_ref):
    d = D_MODEL
    mod = mod_ref[...]
    hb = (_rmsnorm(x_ref[...], n1g_ref[...]) * (1.0 + mod[:, d:2 * d]) + mod[:, 0:d]).astype(BF16)
    o_ref[...] = _dot(hb, w_ref[...])


def _sample_inproj_call(layer, x, mod_s, wl):
    n, d = x.shape
    return pl.pallas_call(
        _sample_inproj_kernel,
        out_shape=jax.ShapeDtypeStruct((n, N_IN), F32),
        grid=(N_IN // d,),
        in_specs=[
            pl.BlockSpec((n, d), lambda j: (0, 0)),
            pl.BlockSpec((n, 6 * d), lambda j: (0, 0)),
            pl.BlockSpec((1, d), lambda j: (0, 0)),
            pl.BlockSpec((d, d), lambda j: (0, j)),
        ],
        out_specs=pl.BlockSpec((n, d), lambda j: (0, j)),
        compiler_params=pltpu.CompilerParams(
            dimension_semantics=("arbitrary",), vmem_limit_bytes=VMEM_LIMIT),
        name=f"sample_inproj_{layer}",
    )(x, mod_s, wl["n1g"], wl["w_in"])


N_SAMPLE_MIXER_INPUTS = 17


def _sample_mixer_kernel(layer, n_groups, *refs):
    refs = refs[:N_SAMPLE_MIXER_INPUTS] + refs[len(refs) - 10:]
    (u_ref, x_ref, mod_ref, hg_ref, lru_ref, conv_ref, hgl_ref, ong_ref, cw_ref, cb_ref,
     wax_ref, ba_ref, bx_ref, lam_ref, wba_ref, wbb_ref, wout_ref,
     xo_ref, hgo_ref, lruo_ref, convo_ref,
     ft_s, kt_s, q_s, v_s, o_s, ob_s) = refs
    d = D_MODEL
    n = x_ref.shape[0]
    i = pl.program_id(0)

    @pl.when(i == 0)
    def _():
        q_s[...] = _silu(u_ref[:, 0:d])
        lb = _lower_bound(hgl_ref[...], layer)
        f, k = _forget_key(u_ref[:, d:2 * d], lb)
        for h in range(HG_HEADS):
            cols = slice(h * LANES, (h + 1) * LANES)
            ft_s[h] = f[:, cols].T
            kt_s[h] = k[:, cols].T
        v_s[...] = u_ref[:, 2 * d:3 * d]
        xb = u_ref[:, 4 * d:5 * d]
        cw = cw_ref[...]
        xc = cb_ref[...] + cw[3:4] * xb
        for j in range(CONV_W - 1):
            xc = xc + cw[j:j + 1] * conv_ref[j]
        convo_ref[0] = conv_ref[1]
        convo_ref[1] = conv_ref[2]
        convo_ref[2] = xb
        yb = _gelu_tanh(u_ref[:, 5 * d:6 * d])
        sp = _softplus(-lam_ref[...])
        for blk in range(LRU_BLOCKS):
            cols = slice(blk * LRU_BS, (blk + 1) * LRU_BS)
            xcn = xc[:, cols]
            gates = _dot(xcn.astype(BF16), wax_ref[blk])
            a, bt = _lru_coeffs(xcn, gates, ba_ref[:, cols], bx_ref[:, cols], sp[:, cols])
            hn = a * lru_ref[:, cols] + bt
            lruo_ref[:, cols] = hn
            ob_s[:, cols] = hn * yb[:, cols]

    shift = (n - i * SAMPLE_GROUP) % n
    rows = pl.ds(pl.multiple_of(i * SAMPLE_GROUP, SAMPLE_GROUP), SAMPLE_GROUP)
    for h in range(HG_HEADS):
        cols = slice(h * LANES, (h + 1) * LANES)
        fr = pltpu.roll(ft_s[h], shift, axis=1)
        kr = pltpu.roll(kt_s[h], shift, axis=1)
        vg = v_s[rows, cols]
        qg = q_s[rows, cols].astype(BF16)
        o_rows = []
        for j in range(SAMPLE_GROUP):
            s_new = fr[:, j:j + 1] * hg_ref[j, h] + kr[:, j:j + 1] * vg[j:j + 1, :]
            hgo_ref[j, h] = s_new
            o_rows.append(_dot(qg, s_new.astype(BF16))[j:j + 1, :])
        o_s[rows, cols] = jnp.concatenate(o_rows, axis=0)

    @pl.when(i == n_groups - 1)
    def _():
        ong = ong_ref[...]
        parts = []
        for h in range(HG_HEADS):
            cols = slice(h * LANES, (h + 1) * LANES)
            parts.append(_rmsnorm(o_s[:, cols], ong) * _silu(u_ref[:, 3 * d + h * LANES:3 * d + (h + 1) * LANES]))
        o_a = jnp.concatenate(parts, axis=1).astype(BF16)
        merged = (_sigmoid(u_ref[:, 6 * d:7 * d]) * _dot(o_a, wba_ref[...])
                  + _sigmoid(u_ref[:, 7 * d:8 * d]) * _dot(ob_s[...].astype(BF16), wbb_ref[...]))
        g1 = mod_ref[:, 2 * d:3 * d]
        xo_ref[...] = x_ref[...] + g1 * _dot(merged.astype(BF16), wout_ref[...])


def _sample_mixer_call(layer, u, x, mod_s, st_hg_all, st_lru_all, st_conv_t, hg_lower, wl, hg_out_prev):
    n, d = x.shape
    grp = SAMPLE_GROUP
    n_groups = n // grp

    def full(shape):
        nd = len(shape)
        return pl.BlockSpec(shape, lambda i: (0,) * nd)

    st_spec = pl.BlockSpec((None, grp, HG_HEADS, HG_DK, HG_DV), lambda i: (layer, i, 0, 0, 0))
    out_shapes = (
        jax.ShapeDtypeStruct((n, d), F32),
        jax.ShapeDtypeStruct((DEPTH, n, HG_HEADS, HG_DK, HG_DV), F32),
        jax.ShapeDtypeStruct((n, d), F32),
        jax.ShapeDtypeStruct((CONV_W - 1, n, d), F32),
    )
    in_specs = [
        full((n, N_IN)), full((n, d)), full((n, 6 * d)), st_spec,
        pl.BlockSpec((None, n, d), lambda i: (layer, 0, 0)), full((CONV_W - 1, n, d)),
        full((DEPTH, d)), full((1, HG_DV)), full((CONV_W, d)), full((1, d)),
        full((LRU_BLOCKS, LRU_BS, 2 * LRU_BS)), full((1, d)), full((1, d)), full((1, d)),
        full((d, d)), full((d, d)), full((d, d)),
    ]
    assert len(in_specs) == N_SAMPLE_MIXER_INPUTS
    args = [u, x, mod_s, st_hg_all, st_lru_all, st_conv_t, hg_lower, wl["ong"], wl["cw"], wl["cb"], wl["wax"],
            wl["ba"], wl["bx"], wl["lam"], wl["wba"], wl["wbb"], wl["wout"]]
    aliases = {}
    if hg_out_prev is not None:
        in_specs.append(pl.BlockSpec(memory_space=pl.ANY))
        args.append(hg_out_prev)
        aliases = {N_SAMPLE_MIXER_INPUTS: 1}
    out_specs = (full((n, d)), st_spec, full((n, d)), full((CONV_W - 1, n, d)))
    scratch = [
        pltpu.VMEM((HG_HEADS, HG_DK, n), F32),
        pltpu.VMEM((HG_HEADS, HG_DK, n), F32),
        pltpu.VMEM((n, d), F32),
        pltpu.VMEM((n, d), F32),
        pltpu.VMEM((n, d), F32),
        pltpu.VMEM((n, d), F32),
    ]
    return pl.pallas_call(
        functools.partial(_sample_mixer_kernel, layer, n_groups),
        out_shape=out_shapes,
        grid=(n_groups,),
        in_specs=in_specs,
        out_specs=out_specs,
        scratch_shapes=scratch,
        input_output_aliases=aliases,
        compiler_params=pltpu.CompilerParams(
            dimension_semantics=("arbitrary",), vmem_limit_bytes=VMEM_LIMIT),
        name=f"sample_mixer_{layer}",
    )(*args)


def _layer_weights(l, norm1_g, norm2_g, w_in, hg_onorm_g, lru_conv_w, lru_conv_b, lru_wa, lru_ba, lru_wx,
                   lru_bx, lru_lambda, w_branch_a, w_branch_b, w_out, w_ff1, w_ff2):
    d = D_MODEL
    return {
        "n1g": norm1_g[l].reshape(1, d),
        "n2g": norm2_g[l].reshape(1, d),
        "w_in": w_in[l].astype(BF16),
        "ong": hg_onorm_g[l].reshape(1, HG_DV),
        "cw": lru_conv_w[l],
        "cb": lru_conv_b[l].reshape(1, d),
        "wax": jnp.concatenate([lru_wa[l], lru_wx[l]], axis=-1).astype(BF16),
        "ba": lru_ba[l].reshape(1, d),
        "bx": lru_bx[l].reshape(1, d),
        "lam": lru_lambda[l].reshape(1, d),
        "wba": w_branch_a[l].astype(BF16),
        "wbb": w_branch_b[l].astype(BF16),
        "wout": w_out[l].astype(BF16),
        "w1": w_ff1[l].astype(BF16),
        "w2": w_ff2[l].astype(BF16),
    }


def kernel(x_prompt, x_sample, state_hgrn, state_rglru, state_conv, c_prompt, c_sample, w_mod, b_mod, norm1_g, norm2_g, w_in, hg_lower, hg_onorm_g, lru_conv_w, lru_conv_b, lru_wa, lru_ba, lru_wx, lru_bx, lru_lambda, w_branch_a, w_branch_b, w_out, w_ff1, w_ff2, final_norm_g):
    bsz, seq, d = x_prompt.shape
    n_s = x_sample.shape[0]
    assert d == D_MODEL and x_sample.shape[1] == 1 and seq % MIX_TL == 0 and seq % FFN_TM == 0
    assert n_s % SAMPLE_GROUP == 0 and n_s == LANES

    mod = _mod_call(jnp.concatenate([c_prompt, c_sample], axis=0), w_mod, b_mod)
    lvl = _hg_tables()
    final_g = final_norm_g.reshape(1, d)

    xp = x_prompt
    xs = x_sample.reshape(n_s, d)
    hg_p, lru_p, conv_p, lru_s, conv_s = [], [], [], [], []
    hg_s = None
    for l in range(DEPTH):
        wl = _layer_weights(l, norm1_g, norm2_g, w_in, hg_onorm_g, lru_conv_w, lru_conv_b, lru_wa, lru_ba,
                            lru_wx, lru_bx, lru_lambda, w_branch_a, w_branch_b, w_out, w_ff1, w_ff2)
        final = l == DEPTH - 1
        mod_p = mod[l, :bsz].reshape(bsz, 1, 6 * d)
        mod_s = mod[l, bsz:]

        xp, hg, lru, conv = _prompt_mixer_call(l, xp, mod_p, hg_lower, wl, lvl)
        xp = _ffn_call(final, xp, mod_p, wl, final_g, FFN_TM, f"prompt_ffn_{l}")
        hg_p.append(hg)
        lru_p.append(lru.reshape(bsz, d))
        conv_p.append(conv)

        u = _sample_inproj_call(l, xs, mod_s, wl)
        xs, hg_s, lru, conv = _sample_mixer_call(l, u, xs, mod_s, state_hgrn, state_rglru,
                                                 jnp.swapaxes(state_conv[l], 0, 1), hg_lower, wl, hg_s)
        xs = _ffn_call(final, xs.reshape(1, n_s, d), mod_s.reshape(1, n_s, 6 * d), wl, final_g, n_s,
                       f"sample_ffn_{l}").reshape(n_s, d)
        lru_s.append(lru)
        conv_s.append(jnp.swapaxes(conv, 0, 1))

    return (xp, xs.reshape(n_s, 1, d),
            jnp.stack(hg_p), jnp.stack(lru_p), jnp.stack(conv_p),
            hg_s, jnp.stack(lru_s), jnp.stack(conv_s))
```

```python
import functools

import numpy as np
import jax
import jax.numpy as jnp
from jax import lax
from jax.experimental import pallas as pl
from jax.experimental.pallas import tpu as pltpu

F32 = jnp.float32
BF16 = jnp.bfloat16

D_MODEL = 1024
DEPTH = 2
HG_HEADS = 8
HG_DK = 128
HG_DV = 128
LB_FLOOR = 1e-30
LRU_BLOCKS = 8
LRU_BS = 128
LRU_C = 8.0
CONV_W = 4
D_FF = 4 * D_MODEL
EPS = 1e-6
N_IN = 8 * D_MODEL
F32_TINY = float(np.finfo(np.float32).tiny)

LANES = 128
SUBLANES = 8
HG_CHUNK = 128
HG_LEVELS = 7
MIX_TL = 256
FFN_TM = 512
SAMPLE_GROUP = 8
VMEM_LIMIT = 56 * 1024 * 1024


def _sigmoid(x):
    return 0.5 * jnp.tanh(0.5 * x) + 0.5


def _silu(x):
    h = 0.5 * x
    return h * jnp.tanh(h) + h


def _gelu_tanh(x):
    c = 0.7978845608028654
    h = 0.5 * x
    return h * jnp.tanh(x * (c + (c * 0.044715) * (x * x))) + h


def _half_tanh_gate(t, y):
    h = 0.5 * y
    return h * t + h


def _log1p(x):
    u = 1.0 + x
    return jnp.where(u == 1.0, x, jnp.log(u) * (x / (u - 1.0)))


def _softplus(z):
    return jnp.maximum(z, 0.0) + _log1p(jnp.exp(-jnp.abs(z)))


def _rmsnorm(x, g):
    ms = jnp.mean(x * x, axis=-1, keepdims=True)
    return x * lax.rsqrt(ms + EPS) * g


def _dot(a, b):
    return jnp.dot(a, b, preferred_element_type=F32)


def _dot_nt(a, b):
    return lax.dot_general(a, b, (((1,), (1,)), ((), ())), preferred_element_type=F32)


def _dot_tn(a, b):
    return lax.dot_general(a, b, (((0,), (0,)), ((), ())), preferred_element_type=F32)


def _lower_bound(hgl, layer):
    m = jnp.max(hgl, axis=0, keepdims=True)
    e = jnp.exp(hgl - m)
    p = e / jnp.sum(e, axis=0, keepdims=True)
    cum = p[0:1]
    for j in range(1, layer + 1):
        cum = cum + p[j:j + 1]
    return cum - p[0:1]


def _forget_key(fa, lb):
    e = jnp.exp(-jnp.abs(fa))
    r = 1.0 / (1.0 + e)
    er = e * r
    pos = fa >= 0.0
    sig_p = jnp.where(pos, r, er)
    sig_n = jnp.where(pos, er, r)
    f = jnp.maximum(lb, LB_FLOOR) + (1.0 - lb) * sig_p
    k = (1.0 - lb) * sig_n
    return f, k


def _lru_coeffs(xc, gates, ba, bx, sp):
    tr = jnp.tanh(0.5 * gates[:, :LRU_BS] + 0.5 * ba)
    ti = jnp.tanh(0.5 * gates[:, LRU_BS:] + 0.5 * bx)
    c1 = (-0.5 * LRU_C) * sp
    log_a = c1 * tr + c1
    a = jnp.exp(log_a)
    th = jnp.tanh(log_a)
    one_m_a2 = (-2.0 * th) / (1.0 - th)
    root = one_m_a2 * lax.rsqrt(jnp.maximum(one_m_a2, F32_TINY))
    return a, root * _half_tanh_gate(ti, xc)


def _mod_kernel(c_ref, w_ref, b_ref, o_ref):
    cs = _silu(c_ref[...]).astype(BF16)
    o_ref[0] = _dot(cs, w_ref[0].astype(BF16)) + b_ref[0]


def _mod_call(c_all, w_mod, b_mod):
    n_rows = c_all.shape[0]
    tn = 1536
    return pl.pallas_call(
        _mod_kernel,
        out_shape=jax.ShapeDtypeStruct((DEPTH, n_rows, 6 * D_MODEL), F32),
        grid=(DEPTH, 6 * D_MODEL // tn),
        in_specs=[
            pl.BlockSpec((n_rows, D_MODEL), lambda l, j: (0, 0)),
            pl.BlockSpec((1, D_MODEL, tn), lambda l, j: (l, 0, j)),
            pl.BlockSpec((1, 1, tn), lambda l, j: (l, 0, j)),
        ],
        out_specs=pl.BlockSpec((1, n_rows, tn), lambda l, j: (l, 0, j)),
        compiler_params=pltpu.CompilerParams(
            dimension_semantics=("arbitrary", "arbitrary"), vmem_limit_bytes=VMEM_LIMIT),
        name="mod",
    )(c_all, w_mod, b_mod.reshape(DEPTH, 1, 6 * D_MODEL))


def _hg_tables():
    t = np.arange(HG_CHUNK)
    x = t[:, None] ^ t[None, :]
    hb = np.floor(np.log2(np.maximum(x, 1))).astype(np.int32)
    lvl = np.where(t[:, None] > t[None, :], hb + 1, np.where(t[:, None] == t[None, :], 0, -1)).astype(np.int32)
    return jnp.asarray(lvl)


def _x_within_group(q, k, g2, level):
    n = q.shape[0]
    shp = (n // SUBLANES, SUBLANES, LANES)
    q3, k3, g3 = q.reshape(shp), k.reshape(shp), g2.reshape(shp)
    sub = lax.broadcasted_iota(jnp.int32, (1, SUBLANES, LANES), 1)
    qside = ((sub >> level) & 1) == 1
    if level == 0:
        scaled = jnp.where(qside, q3 * jnp.exp2(g3 - pltpu.roll(g3, 1, axis=1)), k3)
        return scaled.reshape(n, LANES)

    def row(i):
        return jnp.broadcast_to(g3[:, i:i + 1, :], shp)

    piv = jnp.where(sub < 4, row(1), row(5)) if level == 1 else row(3)
    e = jnp.exp2(-jnp.abs(g3 - piv))
    return (jnp.where(qside, q3, k3) * e).reshape(n, LANES)


def _x_across_groups(q, k, g2, level):
    n = q.shape[0]
    b = 1 << level
    blocks = []
    for base in range(0, n, 2 * b):
        lo, mid, hi = base, base + b, base + 2 * b
        gp = g2[mid - 1:mid, :]
        blocks.append(k[lo:mid] * jnp.exp2(gp - g2[lo:mid]))
        blocks.append(q[mid:hi] * jnp.exp2(g2[mid:hi] - gp))
    return jnp.concatenate(blocks, axis=0)


HG_FULL_LEVELS = 4


def _hg_level_products(q, k, g2):
    c = HG_CHUNK
    prods = [_dot_nt(q.astype(BF16), k.astype(BF16))]
    for level in range(HG_LEVELS):
        b = 1 << level
        if b < SUBLANES:
            x = _x_within_group(q, k, g2, level).astype(BF16)
        else:
            x = _x_across_groups(q, k, g2, level).astype(BF16)
        if level < HG_FULL_LEVELS:
            lhs = x
        else:
            lhs = jnp.concatenate([x[base + b:base + 2 * b] for base in range(0, c, 2 * b)], axis=0)
        prods.append(_dot_nt(lhs, x))
    return prods


def _hg_assemble(prods_per_head, lvl_ref):
    c = HG_CHUNK
    n_heads = len(prods_per_head)
    groups = [[] for _ in range(n_heads)]
    for gi in range(c // SUBLANES):
        r0 = gi * SUBLANES
        rows = slice(r0, r0 + SUBLANES)
        lv = lvl_ref[rows, :]
        on_diag = lv == 0
        a_g = [jnp.where(on_diag, p[0][rows], 0.0) for p in prods_per_head]
        for level in range(HG_LEVELS):
            b = 1 << level
            if level < HG_FULL_LEVELS:
                src = r0
            elif r0 & b:
                src = (r0 >> (level + 1)) * b + (r0 & (b - 1))
            else:
                continue
            at_level = lv == level + 1
            a_g = [jnp.where(at_level, p[level + 1][src:src + SUBLANES], a)
                   for p, a in zip(prods_per_head, a_g)]
        for h in range(n_heads):
            groups[h].append(a_g[h])
    return [jnp.concatenate(g, axis=0).astype(BF16) for g in groups]


def _hg_finish(q, k, v, g2, st, a):
    c = HG_CHUNK
    g_last = g2[c - 1:c, :]
    qg = (q * jnp.exp2(g2)).astype(BF16)
    vb = v.astype(BF16)
    o = _dot_nt(qg, st.astype(BF16)) + _dot(a, vb)
    kd = (k * jnp.exp2(g_last - g2)).astype(BF16)
    st_new = st * jnp.exp2(g_last) + _dot_tn(vb, kd)
    return o, st_new


def _causal_conv(xb, prev8, cw, cb):
    t, w = xb.shape
    x3 = xb.reshape(t // SUBLANES, SUBLANES, w)
    sub = lax.broadcasted_iota(jnp.int32, (1, SUBLANES, w), 1)
    out = cb + cw[CONV_W - 1:CONV_W] * xb
    for back in range(1, CONV_W):
        r = pltpu.roll(x3, back, axis=1)
        r_prev_group = jnp.concatenate([pltpu.roll(prev8, back, axis=0)[None], r[:-1]], axis=0)
        shifted = jnp.where(sub >= back, r, r_prev_group)
        out = out + cw[CONV_W - 1 - back:CONV_W - back] * shifted.reshape(t, w)
    return out


def _chunk_cumsum(y):
    t = y.shape[0]
    y3 = y.reshape(t // SUBLANES, SUBLANES, LANES)
    sub = lax.broadcasted_iota(jnp.int32, (1, SUBLANES, LANES), 1)
    d = 1
    while d < SUBLANES:
        y3 = y3 + jnp.where(sub >= d, pltpu.roll(y3, d, axis=1), 0.0)
        d *= 2
    per_chunk = HG_CHUNK // SUBLANES
    out = []
    for gi in range(t // SUBLANES):
        blk = y3[gi]
        if gi % per_chunk:
            blk = blk + out[-1][SUBLANES - 1:SUBLANES, :]
        out.append(blk)
    return jnp.concatenate(out, axis=0)


def _lru_scan(a, b, h0):
    t = a.shape[0]
    a3 = a.reshape(t // SUBLANES, SUBLANES, LANES)
    b3 = b.reshape(t // SUBLANES, SUBLANES, LANES)
    sub = lax.broadcasted_iota(jnp.int32, a3.shape, 1)
    d = 1
    while d < SUBLANES:
        m = sub >= d
        b3 = a3 * jnp.where(m, pltpu.roll(b3, d, axis=1), 0.0) + b3
        a3 = a3 * jnp.where(m, pltpu.roll(a3, d, axis=1), 1.0)
        d *= 2
    hs = []
    h = h0
    for gidx in range(t // SUBLANES):
        hg = a3[gidx] * h + b3[gidx]
        hs.append(hg)
        h = hg[SUBLANES - 1:SUBLANES, :]
    return jnp.concatenate(hs, axis=0), h


def _prompt_mixer_kernel(layer, n_t,
                         x_ref, mod_ref, hgl_ref, n1g_ref, win_ref, ong_ref, cw_ref, cb_ref, wax_ref,
                         ba_ref, bx_ref, lam_ref, wba_ref, wbb_ref, wout_ref, lvl_ref,
                         xo_ref, hg_ref, lru_ref, conv_ref,
                         q_s, k_s, v_s, g_s, og_s, oa_s, ob_s, prev8_s, st_s, hl_s, xc_s, yb_s, ga_s, gb_s, gr_s,
                         gi_s):
    tl = MIX_TL
    d = D_MODEL
    t_idx = pl.program_id(1)

    @pl.when(t_idx == 0)
    def _():
        st_s[...] = jnp.zeros(st_s.shape, F32)
        hl_s[...] = jnp.zeros(hl_s.shape, F32)
        prev8_s[...] = jnp.zeros(prev8_s.shape, F32)

    x = x_ref[0]
    mod = mod_ref[0]
    sh1, sc1, g1 = mod[:, 0:d], mod[:, d:2 * d], mod[:, 2 * d:3 * d]
    hb = (_rmsnorm(x, n1g_ref[...]) * (1.0 + sc1) + sh1).astype(BF16)

    def proj(j):
        return _dot(hb, win_ref[:, j * d:(j + 1) * d])

    def heads(val, dst):
        for h in range(HG_HEADS):
            dst[h] = val[:, h * LANES:(h + 1) * LANES]

    xb = proj(4)
    xc = _causal_conv(xb, prev8_s[...], cw_ref[...], cb_ref[...])
    xc_s[...] = xc
    prev8_s[...] = xb[tl - SUBLANES:tl, :]
    conv_ref[0] = xb[tl - (CONV_W - 1):tl, :]
    for n in range(LRU_BLOCKS):
        cols = slice(n * LRU_BS, (n + 1) * LRU_BS)
        gates = _dot(xc[:, cols].astype(BF16), wax_ref[n])
        gr_s[:, cols] = gates[:, :LRU_BS]
        gi_s[:, cols] = gates[:, LRU_BS:]
    yb_s[...] = _gelu_tanh(proj(5))
    sp = _softplus(-lam_ref[...])

    def lru_block(n):
        cols = slice(n * LRU_BS, (n + 1) * LRU_BS)
        gates = jnp.concatenate([gr_s[:, cols], gi_s[:, cols]], axis=1)
        a, bt = _lru_coeffs(xc_s[:, cols], gates, ba_ref[:, cols], bx_ref[:, cols], sp[:, cols])
        hseq, h_last = _lru_scan(a, bt, hl_s[:, cols])
        hl_s[:, cols] = h_last
        ob_s[:, cols] = hseq * yb_s[:, cols]

    heads(_silu(proj(0)), q_s)
    lru_block(0)
    lru_block(1)
    lb = _lower_bound(hgl_ref[...], layer)
    f, k = _forget_key(proj(1), lb)
    heads(k, k_s)
    log2f = jnp.minimum(jnp.log2(f), 0.0)
    for h in range(HG_HEADS):
        g_s[h] = _chunk_cumsum(log2f[:, h * LANES:(h + 1) * LANES])
    lru_block(2)
    lru_block(3)
    heads(proj(2), v_s)
    lru_block(4)
    lru_block(5)
    heads(_silu(proj(3)), og_s)
    lru_block(6)
    lru_block(7)
    lru_ref[0] = hl_s[...]
    ga_s[...] = jnp.tanh(0.5 * proj(6))
    gb_s[...] = jnp.tanh(0.5 * proj(7))

    ong = ong_ref[...]
    for c in range(tl // HG_CHUNK):
        rows = slice(c * HG_CHUNK, (c + 1) * HG_CHUNK)
        prods = [_hg_level_products(q_s[h, rows, :], k_s[h, rows, :], g_s[h, rows, :])
                 for h in range(HG_HEADS)]
        a_mats = _hg_assemble(prods, lvl_ref)
        for h in range(HG_HEADS):
            o, st_new = _hg_finish(q_s[h, rows, :], k_s[h, rows, :], v_s[h, rows, :], g_s[h, rows, :],
                                   st_s[h], a_mats[h])
            st_s[h] = st_new
            oa_s[h, rows, :] = _rmsnorm(o, ong) * og_s[h, rows, :]

    o_a = jnp.concatenate([oa_s[h] for h in range(HG_HEADS)], axis=1).astype(BF16)
    merged = (_half_tanh_gate(ga_s[...], _dot(o_a, wba_ref[...]))
              + _half_tanh_gate(gb_s[...], _dot(ob_s[...].astype(BF16), wbb_ref[...])))
    xo_ref[0] = x + g1 * _dot(merged.astype(BF16), wout_ref[...])

    @pl.when(t_idx == n_t - 1)
    def _():
        for h in range(HG_HEADS):
            hg_ref[0, h] = st_s[h].T


def _const_spec(shape):
    nd = len(shape)
    return pl.BlockSpec(shape, lambda *_: (0,) * nd, pipeline_mode=pl.Buffered(1))


def _prompt_mixer_call(layer, x, mod_p, hg_lower, wl, lvl):
    bsz, seq, d = x.shape
    tl = MIX_TL
    n_t = seq // tl
    hshape = (HG_HEADS, tl, LANES)
    out_shapes = (
        jax.ShapeDtypeStruct((bsz, seq, d), F32),
        jax.ShapeDtypeStruct((bsz, HG_HEADS, HG_DK, HG_DV), F32),
        jax.ShapeDtypeStruct((bsz, 1, d), F32),
        jax.ShapeDtypeStruct((bsz, CONV_W - 1, d), F32),
    )
    in_specs = [
        pl.BlockSpec((1, tl, d), lambda b, t: (b, t, 0)),
        pl.BlockSpec((1, 1, 6 * d), lambda b, t: (b, 0, 0)),
        _const_spec((DEPTH, d)),
        _const_spec((1, d)),
        _const_spec((d, N_IN)),
        _const_spec((1, HG_DV)),
        _const_spec((CONV_W, d)),
        _const_spec((1, d)),
        _const_spec((LRU_BLOCKS, LRU_BS, 2 * LRU_BS)),
        _const_spec((1, d)),
        _const_spec((1, d)),
        _const_spec((1, d)),
        _const_spec((d, d)),
        _const_spec((d, d)),
        _const_spec((d, d)),
        _const_spec((HG_CHUNK, HG_CHUNK)),
    ]
    out_specs = (
        pl.BlockSpec((1, tl, d), lambda b, t: (b, t, 0)),
        pl.BlockSpec((1, HG_HEADS, HG_DK, HG_DV), lambda b, t: (b, 0, 0, 0)),
        pl.BlockSpec((1, 1, d), lambda b, t: (b, 0, 0)),
        pl.BlockSpec((1, CONV_W - 1, d), lambda b, t: (b, 0, 0)),
    )
    scratch = [
        pltpu.VMEM(hshape, F32),
        pltpu.VMEM(hshape, F32),
        pltpu.VMEM(hshape, F32),
        pltpu.VMEM(hshape, F32),
        pltpu.VMEM(hshape, F32),
        pltpu.VMEM(hshape, F32),
        pltpu.VMEM((tl, d), F32),
        pltpu.VMEM((SUBLANES, d), F32),
        pltpu.VMEM((HG_HEADS, HG_DV, HG_DK), F32),
        pltpu.VMEM((1, d), F32),
        pltpu.VMEM((tl, d), F32),
        pltpu.VMEM((tl, d), F32),
        pltpu.VMEM((tl, d), F32),
        pltpu.VMEM((tl, d), F32),
        pltpu.VMEM((tl, d), F32),
        pltpu.VMEM((tl, d), F32),
    ]
    return pl.pallas_call(
        functools.partial(_prompt_mixer_kernel, layer, n_t),
        out_shape=out_shapes,
        grid=(bsz, n_t),
        in_specs=in_specs,
        out_specs=out_specs,
        scratch_shapes=scratch,
        compiler_params=pltpu.CompilerParams(
            dimension_semantics=("arbitrary", "arbitrary"), vmem_limit_bytes=VMEM_LIMIT),
        name=f"prompt_mixer_{layer}",
    )(x, mod_p, hg_lower, wl["n1g"], wl["w_in"], wl["ong"], wl["cw"], wl["cb"], wl["wax"],
      wl["ba"], wl["bx"], wl["lam"], wl["wba"], wl["wbb"], wl["wout"], lvl)


def _ffn_kernel(final, x_ref, mod_ref, n2g_ref, w1_ref, w2_ref, fg_ref, o_ref):
    d = D_MODEL
    x = x_ref[0]
    mod = mod_ref[0]
    sh2, sc2, g2 = mod[:, 3 * d:4 * d], mod[:, 4 * d:5 * d], mod[:, 5 * d:6 * d]
    hb = (_rmsnorm(x, n2g_ref[...]) * (1.0 + sc2) + sh2).astype(BF16)
    acc = jnp.zeros(x.shape, F32)
    for c in range(D_FF // d):
        h1 = jnp.maximum(_dot(hb, w1_ref[:, c * d:(c + 1) * d]), 0.0)
        acc = acc + _dot((h1 * h1).astype(BF16), w2_ref[c * d:(c + 1) * d, :])
    xn = x + g2 * acc
    if final:
        xn = _rmsnorm(xn, fg_ref[...])
    o_ref[0] = xn


def _ffn_call(final, x, mod, wl, final_g, tm, name):
    bsz, seq, d = x.shape
    mrows = mod.shape[1]
    mod_spec = (pl.BlockSpec((1, 1, 6 * d), lambda b, t: (b, 0, 0)) if mrows == 1
                else pl.BlockSpec((1, tm, 6 * d), lambda b, t: (b, t, 0)))
    return pl.pallas_call(
        functools.partial(_ffn_kernel, final),
        out_shape=jax.ShapeDtypeStruct((bsz, seq, d), F32),
        grid=(bsz, seq // tm),
        in_specs=[
            pl.BlockSpec((1, tm, d), lambda b, t: (b, t, 0)),
            mod_spec,
            _const_spec((1, d)),
            _const_spec((d, D_FF)),
            _const_spec((D_FF, d)),
            _const_spec((1, d)),
        ],
        out_specs=pl.BlockSpec((1, tm, d), lambda b, t: (b, t, 0)),
        compiler_params=pltpu.CompilerParams(
            dimension_semantics=("arbitrary", "arbitrary"), vmem_limit_bytes=VMEM_LIMIT),
        name=name,
    )(x, mod, wl["n2g"], wl["w1"], wl["w2"], final_g)


def _sample_inproj_kernel(x_ref, mod_ref, n1g_ref, w_ref, o_ref):
    d = D_MODEL
    mod = mod_ref[...]
    hb = (_rmsnorm(x_ref[...], n1g_ref[...]) * (1.0 + mod[:, d:2 * d]) + mod[:, 0:d]).astype(BF16)
    o_ref[...] = _dot(hb, w_ref[...])


def _sample_inproj_call(layer, x, mod_s, wl):
    n, d = x.shape
    return pl.pallas_call(
        _sample_inproj_kernel,
        out_shape=jax.ShapeDtypeStruct((n, N_IN), F32),
        grid=(N_IN // d,),
        in_specs=[
            pl.BlockSpec((n, d), lambda j: (0, 0)),
            pl.BlockSpec((n, 6 * d), lambda j: (0, 0)),
            pl.BlockSpec((1, d), lambda j: (0, 0)),
            pl.BlockSpec((d, d), lambda j: (0, j)),
        ],
        out_specs=pl.BlockSpec((n, d), lambda j: (0, j)),
        compiler_params=pltpu.CompilerParams(
            dimension_semantics=("arbitrary",), vmem_limit_bytes=VMEM_LIMIT),
        name=f"sample_inproj_{layer}",
    )(x, mod_s, wl["n1g"], wl["w_in"])


N_SAMPLE_MIXER_INPUTS = 17


def _sample_mixer_kernel(layer, n_groups, *refs):
    refs = refs[:N_SAMPLE_MIXER_INPUTS] + refs[len(refs) - 10:]
    (u_ref, x_ref, mod_ref, hg_ref, lru_ref, conv_ref, hgl_ref, ong_ref, cw_ref, cb_ref,
     wax_ref, ba_ref, bx_ref, lam_ref, wba_ref, wbb_ref, wout_ref,
     xo_ref, hgo_ref, lruo_ref, convo_ref,
     ft_s, kt_s, q_s, v_s, o_s, ob_s) = refs
    d = D_MODEL
    n = x_ref.shape[0]
    i = pl.program_id(0)

    @pl.when(i == 0)
    def _():
        q_s[...] = _silu(u_ref[:, 0:d])
        lb = _lower_bound(hgl_ref[...], layer)
        f, k = _forget_key(u_ref[:, d:2 * d], lb)
        for h in range(HG_HEADS):
            cols = slice(h * LANES, (h + 1) * LANES)
            ft_s[h] = f[:, cols].T
            kt_s[h] = k[:, cols].T
        v_s[...] = u_ref[:, 2 * d:3 * d]
        xb = u_ref[:, 4 * d:5 * d]
        cw = cw_ref[...]
        xc = cb_ref[...] + cw[3:4] * xb
        for j in range(CONV_W - 1):
            xc = xc + cw[j:j + 1] * conv_ref[j]
        convo_ref[0] = conv_ref[1]
        convo_ref[1] = conv_ref[2]
        convo_ref[2] = xb
        yb = _gelu_tanh(u_ref[:, 5 * d:6 * d])
        sp = _softplus(-lam_ref[...])
        for blk in range(LRU_BLOCKS):
            cols = slice(blk * LRU_BS, (blk + 1) * LRU_BS)
            xcn = xc[:, cols]
            gates = _dot(xcn.astype(BF16), wax_ref[blk])
            a, bt = _lru_coeffs(xcn, gates, ba_ref[:, cols], bx_ref[:, cols], sp[:, cols])
            hn = a * lru_ref[:, cols] + bt
            lruo_ref[:, cols] = hn
            ob_s[:, cols] = hn * yb[:, cols]

    shift = (n - i * SAMPLE_GROUP) % n
    rows = pl.ds(pl.multiple_of(i * SAMPLE_GROUP, SAMPLE_GROUP), SAMPLE_GROUP)
    for h in range(HG_HEADS):
        cols = slice(h * LANES, (h + 1) * LANES)
        fr = pltpu.roll(ft_s[h], shift, axis=1)
        kr = pltpu.roll(kt_s[h], shift, axis=1)
        vg = v_s[rows, cols]
        qg = q_s[rows, cols].astype(BF16)
        o_rows = []
        for j in range(SAMPLE_GROUP):
            s_new = fr[:, j:j + 1] * hg_ref[j, h] + kr[:, j:j + 1] * vg[j:j + 1, :]
            hgo_ref[j, h] = s_new
            o_rows.append(_dot(qg, s_new.astype(BF16))[j:j + 1, :])
        o_s[rows, cols] = jnp.concatenate(o_rows, axis=0)

    @pl.when(i == n_groups - 1)
    def _():
        ong = ong_ref[...]
        parts = []
        for h in range(HG_HEADS):
            cols = slice(h * LANES, (h + 1) * LANES)
            parts.append(_rmsnorm(o_s[:, cols], ong) * _silu(u_ref[:, 3 * d + h * LANES:3 * d + (h + 1) * LANES]))
        o_a = jnp.concatenate(parts, axis=1).astype(BF16)
        merged = (_sigmoid(u_ref[:, 6 * d:7 * d]) * _dot(o_a, wba_ref[...])
                  + _sigmoid(u_ref[:, 7 * d:8 * d]) * _dot(ob_s[...].astype(BF16), wbb_ref[...]))
        g1 = mod_ref[:, 2 * d:3 * d]
        xo_ref[...] = x_ref[...] + g1 * _dot(merged.astype(BF16), wout_ref[...])


def _sample_mixer_call(layer, u, x, mod_s, st_hg_all, st_lru_all, st_conv_t, hg_lower, wl, hg_out_prev):
    n, d = x.shape
    grp = SAMPLE_GROUP
    n_groups = n // grp

    def full(shape):
        nd = len(shape)
        return pl.BlockSpec(shape, lambda i: (0,) * nd)

    st_spec = pl.BlockSpec((None, grp, HG_HEADS, HG_DK, HG_DV), lambda i: (layer, i, 0, 0, 0))
    out_shapes = (
        jax.ShapeDtypeStruct((n, d), F32),
        jax.ShapeDtypeStruct((DEPTH, n, HG_HEADS, HG_DK, HG_DV), F32),
        jax.ShapeDtypeStruct((n, d), F32),
        jax.ShapeDtypeStruct((CONV_W - 1, n, d), F32),
    )
    in_specs = [
        full((n, N_IN)), full((n, d)), full((n, 6 * d)), st_spec,
        pl.BlockSpec((None, n, d), lambda i: (layer, 0, 0)), full((CONV_W - 1, n, d)),
        full((DEPTH, d)), full((1, HG_DV)), full((CONV_W, d)), full((1, d)),
        full((LRU_BLOCKS, LRU_BS, 2 * LRU_BS)), full((1, d)), full((1, d)), full((1, d)),
        full((d, d)), full((d, d)), full((d, d)),
    ]
    assert len(in_specs) == N_SAMPLE_MIXER_INPUTS
    args = [u, x, mod_s, st_hg_all, st_lru_all, st_conv_t, hg_lower, wl["ong"], wl["cw"], wl["cb"], wl["wax"],
            wl["ba"], wl["bx"], wl["lam"], wl["wba"], wl["wbb"], wl["wout"]]
    aliases = {}
    if hg_out_prev is not None:
        in_specs.append(pl.BlockSpec(memory_space=pl.ANY))
        args.append(hg_out_prev)
        aliases = {N_SAMPLE_MIXER_INPUTS: 1}
    out_specs = (full((n, d)), st_spec, full((n, d)), full((CONV_W - 1, n, d)))
    scratch = [
        pltpu.VMEM((HG_HEADS, HG_DK, n), F32),
        pltpu.VMEM((HG_HEADS, HG_DK, n), F32),
        pltpu.VMEM((n, d), F32),
        pltpu.VMEM((n, d), F32),
        pltpu.VMEM((n, d), F32),
        pltpu.VMEM((n, d), F32),
    ]
    return pl.pallas_call(
        functools.partial(_sample_mixer_kernel, layer, n_groups),
        out_shape=out_shapes,
        grid=(n_groups,),
        in_specs=in_specs,
        out_specs=out_specs,
        scratch_shapes=scratch,
        input_output_aliases=aliases,
        compiler_params=pltpu.CompilerParams(
            dimension_semantics=("arbitrary",), vmem_limit_bytes=VMEM_LIMIT),
        name=f"sample_mixer_{layer}",
    )(*args)


def _layer_weights(l, norm1_g, norm2_g, w_in, hg_onorm_g, lru_conv_w, lru_conv_b, lru_wa, lru_ba, lru_wx,
                   lru_bx, lru_lambda, w_branch_a, w_branch_b, w_out, w_ff1, w_ff2):
    d = D_MODEL
    return {
        "n1g": norm1_g[l].reshape(1, d),
        "n2g": norm2_g[l].reshape(1, d),
        "w_in": w_in[l].astype(BF16),
        "ong": hg_onorm_g[l].reshape(1, HG_DV),
        "cw": lru_conv_w[l],
        "cb": lru_conv_b[l].reshape(1, d),
        "wax": jnp.concatenate([lru_wa[l], lru_wx[l]], axis=-1).astype(BF16),
        "ba": lru_ba[l].reshape(1, d),
        "bx": lru_bx[l].reshape(1, d),
        "lam": lru_lambda[l].reshape(1, d),
        "wba": w_branch_a[l].astype(BF16),
        "wbb": w_branch_b[l].astype(BF16),
        "wout": w_out[l].astype(BF16),
        "w1": w_ff1[l].astype(BF16),
        "w2": w_ff2[l].astype(BF16),
    }


def kernel(x_prompt, x_sample, state_hgrn, state_rglru, state_conv, c_prompt, c_sample, w_mod, b_mod, norm1_g, norm2_g, w_in, hg_lower, hg_onorm_g, lru_conv_w, lru_conv_b, lru_wa, lru_ba, lru_wx, lru_bx, lru_lambda, w_branch_a, w_branch_b, w_out, w_ff1, w_ff2, final_norm_g):
    bsz, seq, d = x_prompt.shape
    n_s = x_sample.shape[0]
    assert d == D_MODEL and x_sample.shape[1] == 1 and seq % MIX_TL == 0 and seq % FFN_TM == 0
    assert n_s % SAMPLE_GROUP == 0 and n_s == LANES

    mod = _mod_call(jnp.concatenate([c_prompt, c_sample], axis=0), w_mod, b_mod)
    lvl = _hg_tables()
    final_g = final_norm_g.reshape(1, d)

    xp = x_prompt
    xs = x_sample.reshape(n_s, d)
    hg_p, lru_p, conv_p, lru_s, conv_s = [], [], [], [], []
    hg_s = None
    for l in range(DEPTH):
        wl = _layer_weights(l, norm1_g, norm2_g, w_in, hg_onorm_g, lru_conv_w, lru_conv_b, lru_wa, lru_ba,
                            lru_wx, lru_bx, lru_lambda, w_branch_a, w_branch_b, w_out, w_ff1, w_ff2)
        final = l == DEPTH - 1
        mod_p = mod[l, :bsz].reshape(bsz, 1, 6 * d)
        mod_s = mod[l, bsz:]

        xp, hg, lru, conv = _prompt_mixer_call(l, xp, mod_p, hg_lower, wl, lvl)
        xp = _ffn_call(final, xp, mod_p, wl, final_g, FFN_TM, f"prompt_ffn_{l}")
        hg_p.append(hg)
        lru_p.append(lru.reshape(bsz, d))
        conv_p.append(conv)

        u = _sample_inproj_call(l, xs, mod_s, wl)
        xs, hg_s, lru, conv = _sample_mixer_call(l, u, xs, mod_s, state_hgrn, state_rglru,
                                                 jnp.swapaxes(state_conv[l], 0, 1), hg_lower, wl, hg_s)
        xs = _ffn_call(final, xs.reshape(1, n_s, d), mod_s.reshape(1, n_s, 6 * d), wl, final_g, n_s,
                       f"sample_ffn_{l}").reshape(n_s, d)
        lru_s.append(lru)
        conv_s.append(jnp.swapaxes(conv, 0, 1))

    return (xp, xs.reshape(n_s, 1, d),
            jnp.stack(hg_p), jnp.stack(lru_p), jnp.stack(conv_p),
            hg_s, jnp.stack(lru_s), jnp.stack(conv_s))
```

```python
import functools

import numpy as np
import jax
import jax.numpy as jnp
from jax import lax
from jax.experimental import pallas as pl
from jax.experimental.pallas import tpu as pltpu

F32 = jnp.float32
BF16 = jnp.bfloat16

D_MODEL = 1024
DEPTH = 2
HG_HEADS = 8
HG_DK = 128
HG_DV = 128
LB_FLOOR = 1e-30
LRU_BLOCKS = 8
LRU_BS = 128
LRU_C = 8.0
CONV_W = 4
D_FF = 4 * D_MODEL
EPS = 1e-6
N_IN = 8 * D_MODEL
F32_TINY = float(np.finfo(np.float32).tiny)

LANES = 128
SUBLANES = 8
HG_CHUNK = 128
HG_LEVELS = 7
MIX_TL = 256
FFN_TM = 512
SAMPLE_GROUP = 8
VMEM_LIMIT = 56 * 1024 * 1024


def _sigmoid(x):
    return 0.5 * jnp.tanh(0.5 * x) + 0.5


def _silu(x):
    h = 0.5 * x
    return h * jnp.tanh(h) + h


def _gelu_tanh(x):
    c = 0.7978845608028654
    h = 0.5 * x
    return h * jnp.tanh(x * (c + (c * 0.044715) * (x * x))) + h


def _half_tanh_gate(t, y):
    h = 0.5 * y
    return h * t + h


def _log1p(x):
    u = 1.0 + x
    return jnp.where(u == 1.0, x, jnp.log(u) * (x / (u - 1.0)))


def _softplus(z):
    return jnp.maximum(z, 0.0) + _log1p(jnp.exp(-jnp.abs(z)))


def _rmsnorm(x, g):
    ms = jnp.mean(x * x, axis=-1, keepdims=True)
    return x * lax.rsqrt(ms + EPS) * g


def _dot(a, b):
    return jnp.dot(a, b, preferred_element_type=F32)


def _dot_nt(a, b):
    return lax.dot_general(a, b, (((1,), (1,)), ((), ())), preferred_element_type=F32)


def _dot_tn(a, b):
    return lax.dot_general(a, b, (((0,), (0,)), ((), ())), preferred_element_type=F32)


def _lower_bound(hgl, layer):
    m = jnp.max(hgl, axis=0, keepdims=True)
    e = jnp.exp(hgl - m)
    p = e / jnp.sum(e, axis=0, keepdims=True)
    cum = p[0:1]
    for j in range(1, layer + 1):
        cum = cum + p[j:j + 1]
    return cum - p[0:1]


def _forget_key(fa, lb):
    e = jnp.exp(-jnp.abs(fa))
    r = 1.0 / (1.0 + e)
    er = e * r
    pos = fa >= 0.0
    sig_p = jnp.where(pos, r, er)
    sig_n = jnp.where(pos, er, r)
    f = jnp.maximum(lb, LB_FLOOR) + (1.0 - lb) * sig_p
    k = (1.0 - lb) * sig_n
    return f, k


def _lru_coeffs(xc, gates, ba, bx, sp):
    tr = jnp.tanh(0.5 * gates[:, :LRU_BS] + 0.5 * ba)
    ti = jnp.tanh(0.5 * gates[:, LRU_BS:] + 0.5 * bx)
    c1 = (-0.5 * LRU_C) * sp
    log_a = c1 * tr + c1
    a = jnp.exp(log_a)
    th = jnp.tanh(log_a)
    one_m_a2 = (-2.0 * th) / (1.0 - th)
    root = one_m_a2 * lax.rsqrt(jnp.maximum(one_m_a2, F32_TINY))
    return a, root * _half_tanh_gate(ti, xc)


def _mod_kernel(c_ref, w_ref, b_ref, o_ref):
    cs = _silu(c_ref[...]).astype(BF16)
    o_ref[0] = _dot(cs, w_ref[0].astype(BF16)) + b_ref[0]


def _mod_call(c_all, w_mod, b_mod):
    n_rows = c_all.shape[0]
    tn = 1536
    return pl.pallas_call(
        _mod_kernel,
        out_shape=jax.ShapeDtypeStruct((DEPTH, n_rows, 6 * D_MODEL), F32),
        grid=(DEPTH, 6 * D_MODEL // tn),
        in_specs=[
            pl.BlockSpec((n_rows, D_MODEL), lambda l, j: (0, 0)),
            pl.BlockSpec((1, D_MODEL, tn), lambda l, j: (l, 0, j)),
            pl.BlockSpec((1, 1, tn), lambda l, j: (l, 0, j)),
        ],
        out_specs=pl.BlockSpec((1, n_rows, tn), lambda l, j: (l, 0, j)),
        compiler_params=pltpu.CompilerParams(
            dimension_semantics=("arbitrary", "arbitrary"), vmem_limit_bytes=VMEM_LIMIT),
        name="mod",
    )(c_all, w_mod, b_mod.reshape(DEPTH, 1, 6 * D_MODEL))


def _hg_tables():
    t = np.arange(HG_CHUNK)
    x = t[:, None] ^ t[None, :]
    hb = np.floor(np.log2(np.maximum(x, 1))).astype(np.int32)
    lvl = np.where(t[:, None] > t[None, :], hb + 1, np.where(t[:, None] == t[None, :], 0, -1)).astype(np.int32)
    return jnp.asarray(lvl)


def _x_within_group(q, k, g2, level):
    n = q.shape[0]
    shp = (n // SUBLANES, SUBLANES, LANES)
    q3, k3, g3 = q.reshape(shp), k.reshape(shp), g2.reshape(shp)
    sub = lax.broadcasted_iota(jnp.int32, (1, SUBLANES, LANES), 1)
    qside = ((sub >> level) & 1) == 1
    if level == 0:
        scaled = jnp.where(qside, q3 * jnp.exp2(g3 - pltpu.roll(g3, 1, axis=1)), k3)
        return scaled.reshape(n, LANES)

    def row(i):
        return jnp.broadcast_to(g3[:, i:i + 1, :], shp)

    piv = jnp.where(sub < 4, row(1), row(5)) if level == 1 else row(3)
    e = jnp.exp2(-jnp.abs(g3 - piv))
    return (jnp.where(qside, q3, k3) * e).reshape(n, LANES)


def _x_across_groups(q, k, g2, level):
    n = q.shape[0]
    b = 1 << level
    blocks = []
    for base in range(0, n, 2 * b):
        lo, mid, hi = base, base + b, base + 2 * b
        gp = g2[mid - 1:mid, :]
        blocks.append(k[lo:mid] * jnp.exp2(gp - g2[lo:mid]))
        blocks.append(q[mid:hi] * jnp.exp2(g2[mid:hi] - gp))
    return jnp.concatenate(blocks, axis=0)


HG_FULL_LEVELS = 4


def _hg_level_products(q, k, g2):
    c = HG_CHUNK
    prods = [_dot_nt(q.astype(BF16), k.astype(BF16))]
    for level in range(HG_LEVELS):
        b = 1 << level
        if b < SUBLANES:
            x = _x_within_group(q, k, g2, level).astype(BF16)
        else:
            x = _x_across_groups(q, k, g2, level).astype(BF16)
        if level < HG_FULL_LEVELS:
            lhs = x
        else:
            lhs = jnp.concatenate([x[base + b:base + 2 * b] for base in range(0, c, 2 * b)], axis=0)
        prods.append(_dot_nt(lhs, x))
    return prods


def _hg_assemble(prods_per_head, lvl_ref):
    c = HG_CHUNK
    n_heads = len(prods_per_head)
    groups = [[] for _ in range(n_heads)]
    for gi in range(c // SUBLANES):
        r0 = gi * SUBLANES
        rows = slice(r0, r0 + SUBLANES)
        lv = lvl_ref[rows, :]
        on_diag = lv == 0
        a_g = [jnp.where(on_diag, p[0][rows], 0.0) for p in prods_per_head]
        for level in range(HG_LEVELS):
            b = 1 << level
            if level < HG_FULL_LEVELS:
                src = r0
            elif r0 & b:
                src = (r0 >> (level + 1)) * b + (r0 & (b - 1))
            else:
                continue
            at_level = lv == level + 1
            a_g = [jnp.where(at_level, p[level + 1][src:src + SUBLANES], a)
                   for p, a in zip(prods_per_head, a_g)]
        for h in range(n_heads):
            groups[h].append(a_g[h])
    return [jnp.concatenate(g, axis=0).astype(BF16) for g in groups]


def _hg_finish(q, k, v, g2, st, a):
    c = HG_CHUNK
    g_last = g2[c - 1:c, :]
    qg = (q * jnp.exp2(g2)).astype(BF16)
    vb = v.astype(BF16)
    o = _dot_nt(qg, st.astype(BF16)) + _dot(a, vb)
    kd = (k * jnp.exp2(g_last - g2)).astype(BF16)
    st_new = st * jnp.exp2(g_last) + _dot_tn(vb, kd)
    return o, st_new


def _causal_conv(xb, prev8, cw, cb):
    t, w = xb.shape
    x3 = xb.reshape(t // SUBLANES, SUBLANES, w)
    sub = lax.broadcasted_iota(jnp.int32, (1, SUBLANES, w), 1)
    out = cb + cw[CONV_W - 1:CONV_W] * xb
    for back in range(1, CONV_W):
        r = pltpu.roll(x3, back, axis=1)
        r_prev_group = jnp.concatenate([pltpu.roll(prev8, back, axis=0)[None], r[:-1]], axis=0)
        shifted = jnp.where(sub >= back, r, r_prev_group)
        out = out + cw[CONV_W - 1 - back:CONV_W - back] * shifted.reshape(t, w)
    return out


def _chunk_cumsum(y):
    t = y.shape[0]
    y3 = y.reshape(t // SUBLANES, SUBLANES, LANES)
    sub = lax.broadcasted_iota(jnp.int32, (1, SUBLANES, LANES), 1)
    d = 1
    while d < SUBLANES:
        y3 = y3 + jnp.where(sub >= d, pltpu.roll(y3, d, axis=1), 0.0)
        d *= 2
    per_chunk = HG_CHUNK // SUBLANES
    out = []
    for gi in range(t // SUBLANES):
        blk = y3[gi]
        if gi % per_chunk:
            blk = blk + out[-1][SUBLANES - 1:SUBLANES, :]
        out.append(blk)
    return jnp.concatenate(out, axis=0)


def _lru_scan(a, b, h0):
    t = a.shape[0]
    a3 = a.reshape(t // SUBLANES, SUBLANES, LANES)
    b3 = b.reshape(t // SUBLANES, SUBLANES, LANES)
    sub = lax.broadcasted_iota(jnp.int32, a3.shape, 1)
    d = 1
    while d < SUBLANES:
        m = sub >= d
        b3 = a3 * jnp.where(m, pltpu.roll(b3, d, axis=1), 0.0) + b3
        a3 = a3 * jnp.where(m, pltpu.roll(a3, d, axis=1), 1.0)
        d *= 2
    hs = []
    h = h0
    for gidx in range(t // SUBLANES):
        hg = a3[gidx] * h + b3[gidx]
        hs.append(hg)
        h = hg[SUBLANES - 1:SUBLANES, :]
    return jnp.concatenate(hs, axis=0), h


def _prompt_mixer_kernel(layer, n_t,
                         x_ref, mod_ref, hgl_ref, n1g_ref, win_ref, ong_ref, cw_ref, cb_ref, wax_ref,
                         ba_ref, bx_ref, lam_ref, wba_ref, wbb_ref, wout_ref, lvl_ref,
                         xo_ref, hg_ref, lru_ref, conv_ref,
                         q_s, k_s, v_s, g_s, og_s, oa_s, ob_s, prev8_s, st_s, hl_s, xc_s, yb_s, ga_s, gb_s, gr_s,
                         gi_s):
    tl = MIX_TL
    d = D_MODEL
    t_idx = pl.program_id(1)

    @pl.when(t_idx == 0)
    def _():
        st_s[...] = jnp.zeros(st_s.shape, F32)
        hl_s[...] = jnp.zeros(hl_s.shape, F32)
        prev8_s[...] = jnp.zeros(prev8_s.shape, F32)

    x = x_ref[0]
    mod = mod_ref[0]
    sh1, sc1, g1 = mod[:, 0:d], mod[:, d:2 * d], mod[:, 2 * d:3 * d]
    hb = (_rmsnorm(x, n1g_ref[...]) * (1.0 + sc1) + sh1).astype(BF16)

    def proj(j):
        return _dot(hb, win_ref[:, j * d:(j + 1) * d])

    def heads(val, dst):
        for h in range(HG_HEADS):
            dst[h] = val[:, h * LANES:(h + 1) * LANES]

    xb = proj(4)
    xc = _causal_conv(xb, prev8_s[...], cw_ref[...], cb_ref[...])
    xc_s[...] = xc
    prev8_s[...] = xb[tl - SUBLANES:tl, :]
    conv_ref[0] = xb[tl - (CONV_W - 1):tl, :]
    for n in range(LRU_BLOCKS):
        cols = slice(n * LRU_BS, (n + 1) * LRU_BS)
        gates = _dot(xc[:, cols].astype(BF16), wax_ref[n])
        gr_s[:, cols] = gates[:, :LRU_BS]
        gi_s[:, cols] = gates[:, LRU_BS:]
    yb_s[...] = _gelu_tanh(proj(5))
    sp = _softplus(-lam_ref[...])

    def lru_block(n):
        cols = slice(n * LRU_BS, (n + 1) * LRU_BS)
        gates = jnp.concatenate([gr_s[:, cols], gi_s[:, cols]], axis=1)
        a, bt = _lru_coeffs(xc_s[:, cols], gates, ba_ref[:, cols], bx_ref[:, cols], sp[:, cols])
        hseq, h_last = _lru_scan(a, bt, hl_s[:, cols])
        hl_s[:, cols] = h_last
        ob_s[:, cols] = hseq * yb_s[:, cols]

    lb = _lower_bound(hgl_ref[...], layer)
    f, k = _forget_key(proj(1), lb)
    heads(k, k_s)
    log2f = jnp.minimum(jnp.log2(f), 0.0)
    for h in range(HG_HEADS):
        g_s[h] = _chunk_cumsum(log2f[:, h * LANES:(h + 1) * LANES])
    lru_block(0)
    lru_block(1)
    heads(_silu(proj(0)), q_s)
    lru_block(2)
    lru_block(3)
    heads(proj(2), v_s)
    lru_block(4)
    lru_block(5)
    heads(_silu(proj(3)), og_s)
    lru_block(6)
    lru_block(7)
    lru_ref[0] = hl_s[...]
    ga_s[...] = jnp.tanh(0.5 * proj(6))
    gb_s[...] = jnp.tanh(0.5 * proj(7))

    ong = ong_ref[...]
    for c in range(tl // HG_CHUNK):
        rows = slice(c * HG_CHUNK, (c + 1) * HG_CHUNK)
        prods = [_hg_level_products(q_s[h, rows, :], k_s[h, rows, :], g_s[h, rows, :])
                 for h in range(HG_HEADS)]
        a_mats = _hg_assemble(prods, lvl_ref)
        for h in range(HG_HEADS):
            o, st_new = _hg_finish(q_s[h, rows, :], k_s[h, rows, :], v_s[h, rows, :], g_s[h, rows, :],
                                   st_s[h], a_mats[h])
            st_s[h] = st_new
            oa_s[h, rows, :] = _rmsnorm(o, ong) * og_s[h, rows, :]

    o_a = jnp.concatenate([oa_s[h] for h in range(HG_HEADS)], axis=1).astype(BF16)
    merged = (_half_tanh_gate(ga_s[...], _dot(o_a, wba_ref[...]))
              + _half_tanh_gate(gb_s[...], _dot(ob_s[...].astype(BF16), wbb_ref[...])))
    xo_ref[0] = x + g1 * _dot(merged.astype(BF16), wout_ref[...])

    @pl.when(t_idx == n_t - 1)
    def _():
        for h in range(HG_HEADS):
            hg_ref[0, h] = st_s[h].T


def _const_spec(shape):
    nd = len(shape)
    return pl.BlockSpec(shape, lambda *_: (0,) * nd, pipeline_mode=pl.Buffered(1))


def _prompt_mixer_call(layer, x, mod_p, hg_lower, wl, lvl):
    bsz, seq, d = x.shape
    tl = MIX_TL
    n_t = seq // tl
    hshape = (HG_HEADS, tl, LANES)
    out_shapes = (
        jax.ShapeDtypeStruct((bsz, seq, d), F32),
        jax.ShapeDtypeStruct((bsz, HG_HEADS, HG_DK, HG_DV), F32),
        jax.ShapeDtypeStruct((bsz, 1, d), F32),
        jax.ShapeDtypeStruct((bsz, CONV_W - 1, d), F32),
    )
    in_specs = [
        pl.BlockSpec((1, tl, d), lambda b, t: (b, t, 0)),
        pl.BlockSpec((1, 1, 6 * d), lambda b, t: (b, 0, 0)),
        _const_spec((DEPTH, d)),
        _const_spec((1, d)),
        _const_spec((d, N_IN)),
        _const_spec((1, HG_DV)),
        _const_spec((CONV_W, d)),
        _const_spec((1, d)),
        _const_spec((LRU_BLOCKS, LRU_BS, 2 * LRU_BS)),
        _const_spec((1, d)),
        _const_spec((1, d)),
        _const_spec((1, d)),
        _const_spec((d, d)),
        _const_spec((d, d)),
        _const_spec((d, d)),
        _const_spec((HG_CHUNK, HG_CHUNK)),
    ]
    out_specs = (
        pl.BlockSpec((1, tl, d), lambda b, t: (b, t, 0)),
        pl.BlockSpec((1, HG_HEADS, HG_DK, HG_DV), lambda b, t: (b, 0, 0, 0)),
        pl.BlockSpec((1, 1, d), lambda b, t: (b, 0, 0)),
        pl.BlockSpec((1, CONV_W - 1, d), lambda b, t: (b, 0, 0)),
    )
    scratch = [
        pltpu.VMEM(hshape, F32),
        pltpu.VMEM(hshape, F32),
        pltpu.VMEM(hshape, F32),
        pltpu.VMEM(hshape, F32),
        pltpu.VMEM(hshape, F32),
        pltpu.VMEM(hshape, F32),
        pltpu.VMEM((tl, d), F32),
        pltpu.VMEM((SUBLANES, d), F32),
        pltpu.VMEM((HG_HEADS, HG_DV, HG_DK), F32),
        pltpu.VMEM((1, d), F32),
        pltpu.VMEM((tl, d), F32),
        pltpu.VMEM((tl, d), F32),
        pltpu.VMEM((tl, d), F32),
        pltpu.VMEM((tl, d), F32),
        pltpu.VMEM((tl, d), F32),
        pltpu.VMEM((tl, d), F32),
    ]
    return pl.pallas_call(
        functools.partial(_prompt_mixer_kernel, layer, n_t),
        out_shape=out_shapes,
        grid=(bsz, n_t),
        in_specs=in_specs,
        out_specs=out_specs,
        scratch_shapes=scratch,
        compiler_params=pltpu.CompilerParams(
            dimension_semantics=("arbitrary", "arbitrary"), vmem_limit_bytes=VMEM_LIMIT),
        name=f"prompt_mixer_{layer}",
    )(x, mod_p, hg_lower, wl["n1g"], wl["w_in"], wl["ong"], wl["cw"], wl["cb"], wl["wax"],
      wl["ba"], wl["bx"], wl["lam"], wl["wba"], wl["wbb"], wl["wout"], lvl)


def _ffn_kernel(final, x_ref, mod_ref, n2g_ref, w1_ref, w2_ref, fg_ref, o_ref):
    d = D_MODEL
    x = x_ref[0]
    mod = mod_ref[0]
    sh2, sc2, g2 = mod[:, 3 * d:4 * d], mod[:, 4 * d:5 * d], mod[:, 5 * d:6 * d]
    hb = (_rmsnorm(x, n2g_ref[...]) * (1.0 + sc2) + sh2).astype(BF16)
    acc = jnp.zeros(x.shape, F32)
    for c in range(D_FF // d):
        h1 = jnp.maximum(_dot(hb, w1_ref[:, c * d:(c + 1) * d]), 0.0)
        acc = acc + _dot((h1 * h1).astype(BF16), w2_ref[c * d:(c + 1) * d, :])
    xn = x + g2 * acc
    if final:
        xn = _rmsnorm(xn, fg_ref[...])
    o_ref[0] = xn


def _ffn_call(final, x, mod, wl, final_g, tm, name):
    bsz, seq, d = x.shape
    mrows = mod.shape[1]
    mod_spec = (pl.BlockSpec((1, 1, 6 * d), lambda b, t: (b, 0, 0)) if mrows == 1
                else pl.BlockSpec((1, tm, 6 * d), lambda b, t: (b, t, 0)))
    return pl.pallas_call(
        functools.partial(_ffn_kernel, final),
        out_shape=jax.ShapeDtypeStruct((bsz, seq, d), F32),
        grid=(bsz, seq // tm),
        in_specs=[
            pl.BlockSpec((1, tm, d), lambda b, t: (b, t, 0)),
            mod_spec,
            _const_spec((1, d)),
            _const_spec((d, D_FF)),
            _const_spec((D_FF, d)),
            _const_spec((1, d)),
        ],
        out_specs=pl.BlockSpec((1, tm, d), lambda b, t: (b, t, 0)),
        compiler_params=pltpu.CompilerParams(
            dimension_semantics=("arbitrary", "arbitrary"), vmem_limit_bytes=VMEM_LIMIT),
        name=name,
    )(x, mod, wl["n2g"], wl["w1"], wl["w2"], final_g)


def _sample_inproj_kernel(x_ref, mod_ref, n1g_ref, w_ref, o_ref):
    d = D_MODEL
    mod = mod_ref[...]
    hb = (_rmsnorm(x_ref[...], n1g_ref[...]) * (1.0 + mod[:, d:2 * d]) + mod[:, 0:d]).astype(BF16)
    o_ref[...] = _dot(hb, w_ref[...])


def _sample_inproj_call(layer, x, mod_s, wl):
    n, d = x.shape
    return pl.pallas_call(
        _sample_inproj_kernel,
        out_shape=jax.ShapeDtypeStruct((n, N_IN), F32),
        grid=(N_IN // d,),
        in_specs=[
            pl.BlockSpec((n, d), lambda j: (0, 0)),
            pl.BlockSpec((n, 6 * d), lambda j: (0, 0)),
            pl.BlockSpec((1, d), lambda j: (0, 0)),
            pl.BlockSpec((d, d), lambda j: (0, j)),
        ],
        out_specs=pl.BlockSpec((n, d), lambda j: (0, j)),
        compiler_params=pltpu.CompilerParams(
            dimension_semantics=("arbitrary",), vmem_limit_bytes=VMEM_LIMIT),
        name=f"sample_inproj_{layer}",
    )(x, mod_s, wl["n1g"], wl["w_in"])


N_SAMPLE_MIXER_INPUTS = 20

OUTER_ROWS = 16
OUTER_LEFT_SRC = (0, 0, 1, 0, 2, 1, 3, 4, 5)
OUTER_RIGHT_SRC = (0, 1, 0, 2, 0, 1)


def _outer_tables():
    g = SAMPLE_GROUP
    sel_l = np.zeros((g * OUTER_ROWS, 6 * g), np.float32)
    sel_r = np.zeros((g * OUTER_ROWS, 4 * g), np.float32)
    for j in range(g):
        for r, a in enumerate(OUTER_LEFT_SRC):
            sel_l[OUTER_ROWS * j + r, g * a + j] = 1.0
        for r, a in enumerate(OUTER_RIGHT_SRC):
            sel_r[OUTER_ROWS * j + r, g * a + j] = 1.0
    ones_rows = np.zeros((g, OUTER_ROWS, LANES), np.float32)
    ones_rows[:, len(OUTER_RIGHT_SRC):len(OUTER_LEFT_SRC)] = 1.0
    return (jnp.asarray(sel_l, BF16), jnp.asarray(sel_r, BF16),
            jnp.asarray(ones_rows.reshape(g * OUTER_ROWS, LANES), BF16))


def _split3(x):
    x1 = x.astype(BF16).astype(F32)
    r = x - x1
    x2 = r.astype(BF16).astype(F32)
    x3 = (r - x2).astype(BF16).astype(F32)
    return [x1, x2, x3]


def _sample_mixer_kernel(layer, n_groups, *refs):
    refs = refs[:N_SAMPLE_MIXER_INPUTS] + refs[len(refs) - 10:]
    (u_ref, x_ref, mod_ref, hg_ref, lru_ref, conv_ref, hgl_ref, ong_ref, cw_ref, cb_ref,
     wax_ref, ba_ref, bx_ref, lam_ref, wba_ref, wbb_ref, wout_ref, sell_ref, selr_ref, ones_ref,
     xo_ref, hgo_ref, lruo_ref, convo_ref,
     f_s, k_s, q_s, v_s, o_s, ob_s) = refs
    d = D_MODEL
    n = x_ref.shape[0]
    i = pl.program_id(0)
    if layer == 0:
        for other in range(1, DEPTH):
            hgo_ref[other] = jnp.zeros(hgo_ref.shape[1:], F32)
        hgo_ref = hgo_ref.at[0]

    @pl.when(i == 0)
    def _():
        q_s[...] = _silu(u_ref[:, 0:d])
        lb = _lower_bound(hgl_ref[...], layer)
        f, k = _forget_key(u_ref[:, d:2 * d], lb)
        f_s[...] = f
        k_s[...] = k
        v_s[...] = u_ref[:, 2 * d:3 * d]
        xb = u_ref[:, 4 * d:5 * d]
        cw = cw_ref[...]
        xc = cb_ref[...] + cw[3:4] * xb
        for j in range(CONV_W - 1):
            xc = xc + cw[j:j + 1] * conv_ref[j]
        convo_ref[0] = conv_ref[1]
        convo_ref[1] = conv_ref[2]
        convo_ref[2] = xb
        yb = _gelu_tanh(u_ref[:, 5 * d:6 * d])
        sp = _softplus(-lam_ref[...])
        for blk in range(LRU_BLOCKS):
            cols = slice(blk * LRU_BS, (blk + 1) * LRU_BS)
            xcn = xc[:, cols]
            gates = _dot(xcn.astype(BF16), wax_ref[blk])
            a, bt = _lru_coeffs(xcn, gates, ba_ref[:, cols], bx_ref[:, cols], sp[:, cols])
            hn = a * lru_ref[:, cols] + bt
            lruo_ref[:, cols] = hn
            ob_s[:, cols] = hn * yb[:, cols]

    rows = pl.ds(pl.multiple_of(i * SAMPLE_GROUP, SAMPLE_GROUP), SAMPLE_GROUP)
    ones_rows = ones_ref[...]

    def outer_products(h):
        cols = slice(h * LANES, (h + 1) * LANES)
        vparts = _split3(v_s[rows, cols])
        left = jnp.concatenate(_split3(k_s[rows, cols]) + _split3(f_s[rows, cols]), axis=0).astype(BF16)
        right = jnp.concatenate(vparts + [jnp.zeros_like(vparts[0])], axis=0).astype(BF16)
        left_t = _dot(sell_ref[...], left).T.astype(BF16)
        right_rows = jnp.concatenate([_dot(selr_ref[...], right).astype(BF16), ones_rows], axis=1)
        out = []
        for j in range(SAMPLE_GROUP):
            lo, hi = j * OUTER_ROWS, (j + 1) * OUTER_ROWS
            pieces = [right_rows[lo:hi]]
            if lo:
                pieces.insert(0, jnp.zeros((lo, 2 * HG_DV), BF16))
            if hi < SAMPLE_GROUP * OUTER_ROWS:
                pieces.append(jnp.zeros((SAMPLE_GROUP * OUTER_ROWS - hi, 2 * HG_DV), BF16))
            out.append(_dot(left_t, jnp.concatenate(pieces, axis=0)))
        return out

    kv_f = outer_products(0)
    for h in range(HG_HEADS):
        cols = slice(h * LANES, (h + 1) * LANES)
        kv_f_next = outer_products(h + 1) if h + 1 < HG_HEADS else None
        s_new = [hg_ref[j, h] * kv_f[j][:, HG_DV:] + kv_f[j][:, :HG_DV] for j in range(SAMPLE_GROUP)]
        for j in range(SAMPLE_GROUP):
            hgo_ref[j, h] = s_new[j]
        qg = q_s[rows, cols].astype(BF16)
        o_rows = [_dot(qg, s_new[j].astype(BF16))[j:j + 1, :] for j in range(SAMPLE_GROUP)]
        o_s[rows, cols] = jnp.concatenate(o_rows, axis=0)
        kv_f = kv_f_next

    @pl.when(i == n_groups - 1)
    def _():
        ong = ong_ref[...]
        parts = []
        for h in range(HG_HEADS):
            cols = slice(h * LANES, (h + 1) * LANES)
            parts.append(_rmsnorm(o_s[:, cols], ong) * _silu(u_ref[:, 3 * d + h * LANES:3 * d + (h + 1) * LANES]))
        o_a = jnp.concatenate(parts, axis=1).astype(BF16)
        merged = (_sigmoid(u_ref[:, 6 * d:7 * d]) * _dot(o_a, wba_ref[...])
                  + _sigmoid(u_ref[:, 7 * d:8 * d]) * _dot(ob_s[...].astype(BF16), wbb_ref[...]))
        g1 = mod_ref[:, 2 * d:3 * d]
        xo_ref[...] = x_ref[...] + g1 * _dot(merged.astype(BF16), wout_ref[...])


def _sample_mixer_call(layer, u, x, mod_s, st_hg_all, st_lru_all, st_conv_t, hg_lower, wl, hg_out_prev):
    n, d = x.shape
    grp = SAMPLE_GROUP
    n_groups = n // grp

    def full(shape):
        nd = len(shape)
        return pl.BlockSpec(shape, lambda i: (0,) * nd)

    st_spec = pl.BlockSpec((None, grp, HG_HEADS, HG_DK, HG_DV), lambda i: (layer, i, 0, 0, 0))
    out_shapes = (
        jax.ShapeDtypeStruct((n, d), F32),
        jax.ShapeDtypeStruct((DEPTH, n, HG_HEADS, HG_DK, HG_DV), F32),
        jax.ShapeDtypeStruct((n, d), F32),
        jax.ShapeDtypeStruct((CONV_W - 1, n, d), F32),
    )
    in_specs = [
        full((n, N_IN)), full((n, d)), full((n, 6 * d)), st_spec,
        pl.BlockSpec((None, n, d), lambda i: (layer, 0, 0)), full((CONV_W - 1, n, d)),
        full((DEPTH, d)), full((1, HG_DV)), full((CONV_W, d)), full((1, d)),
        full((LRU_BLOCKS, LRU_BS, 2 * LRU_BS)), full((1, d)), full((1, d)), full((1, d)),
        full((d, d)), full((d, d)), full((d, d)),
    ]
    outer_tables = _outer_tables()
    in_specs += [full(t.shape) for t in outer_tables]
    assert len(in_specs) == N_SAMPLE_MIXER_INPUTS
    args = [u, x, mod_s, st_hg_all, st_lru_all, st_conv_t, hg_lower, wl["ong"], wl["cw"], wl["cb"], wl["wax"],
            wl["ba"], wl["bx"], wl["lam"], wl["wba"], wl["wbb"], wl["wout"], *outer_tables]
    aliases = {}
    if hg_out_prev is not None:
        in_specs.append(pl.BlockSpec(memory_space=pl.ANY))
        args.append(hg_out_prev)
        aliases = {N_SAMPLE_MIXER_INPUTS: 1}
    st_out_spec = st_spec if layer else pl.BlockSpec((DEPTH, grp, HG_HEADS, HG_DK, HG_DV),
                                                     lambda i: (0, i, 0, 0, 0))
    out_specs = (full((n, d)), st_out_spec, full((n, d)), full((CONV_W - 1, n, d)))
    scratch = [
        pltpu.VMEM((n, d), F32),
        pltpu.VMEM((n, d), F32),
        pltpu.VMEM((n, d), F32),
        pltpu.VMEM((n, d), F32),
        pltpu.VMEM((n, d), F32),
        pltpu.VMEM((n, d), F32),
    ]
    return pl.pallas_call(
        functools.partial(_sample_mixer_kernel, layer, n_groups),
        out_shape=out_shapes,
        grid=(n_groups,),
        in_specs=in_specs,
        out_specs=out_specs,
        scratch_shapes=scratch,
        input_output_aliases=aliases,
        compiler_params=pltpu.CompilerParams(
            dimension_semantics=("arbitrary",), vmem_limit_bytes=VMEM_LIMIT),
        name=f"sample_mixer_{layer}",
    )(*args)


def _layer_weights(l, norm1_g, norm2_g, w_in, hg_onorm_g, lru_conv_w, lru_conv_b, lru_wa, lru_ba, lru_wx,
                   lru_bx, lru_lambda, w_branch_a, w_branch_b, w_out, w_ff1, w_ff2):
    d = D_MODEL
    return {
        "n1g": norm1_g[l].reshape(1, d),
        "n2g": norm2_g[l].reshape(1, d),
        "w_in": w_in[l].astype(BF16),
        "ong": hg_onorm_g[l].reshape(1, HG_DV),
        "cw": lru_conv_w[l],
        "cb": lru_conv_b[l].reshape(1, d),
        "wax": jnp.concatenate([lru_wa[l], lru_wx[l]], axis=-1).astype(BF16),
        "ba": lru_ba[l].reshape(1, d),
        "bx": lru_bx[l].reshape(1, d),
        "lam": lru_lambda[l].reshape(1, d),
        "wba": w_branch_a[l].astype(BF16),
        "wbb": w_branch_b[l].astype(BF16),
        "wout": w_out[l].astype(BF16),
        "w1": w_ff1[l].astype(BF16),
        "w2": w_ff2[l].astype(BF16),
    }


def kernel(x_prompt, x_sample, state_hgrn, state_rglru, state_conv, c_prompt, c_sample, w_mod, b_mod, norm1_g, norm2_g, w_in, hg_lower, hg_onorm_g, lru_conv_w, lru_conv_b, lru_wa, lru_ba, lru_wx, lru_bx, lru_lambda, w_branch_a, w_branch_b, w_out, w_ff1, w_ff2, final_norm_g):
    bsz, seq, d = x_prompt.shape
    n_s = x_sample.shape[0]
    assert d == D_MODEL and x_sample.shape[1] == 1 and seq % MIX_TL == 0 and seq % FFN_TM == 0
    assert n_s % SAMPLE_GROUP == 0 and n_s == LANES

    mod = _mod_call(jnp.concatenate([c_prompt, c_sample], axis=0), w_mod, b_mod)
    lvl = _hg_tables()
    final_g = final_norm_g.reshape(1, d)

    xp = x_prompt
    xs = x_sample.reshape(n_s, d)
    hg_p, lru_p, conv_p, lru_s, conv_s = [], [], [], [], []
    hg_s = None
    for l in range(DEPTH):
        wl = _layer_weights(l, norm1_g, norm2_g, w_in, hg_onorm_g, lru_conv_w, lru_conv_b, lru_wa, lru_ba,
                            lru_wx, lru_bx, lru_lambda, w_branch_a, w_branch_b, w_out, w_ff1, w_ff2)
        final = l == DEPTH - 1
        mod_p = mod[l, :bsz].reshape(bsz, 1, 6 * d)
        mod_s = mod[l, bsz:]

        xp, hg, lru, conv = _prompt_mixer_call(l, xp, mod_p, hg_lower, wl, lvl)
        xp = _ffn_call(final, xp, mod_p, wl, final_g, FFN_TM, f"prompt_ffn_{l}")
        hg_p.append(hg)
        lru_p.append(lru.reshape(bsz, d))
        conv_p.append(conv)

        u = _sample_inproj_call(l, xs, mod_s, wl)
        xs, hg_s, lru, conv = _sample_mixer_call(l, u, xs, mod_s, state_hgrn, state_rglru,
                                                 jnp.swapaxes(state_conv[l], 0, 1), hg_lower, wl, hg_s)
        xs = _ffn_call(final, xs.reshape(1, n_s, d), mod_s.reshape(1, n_s, 6 * d), wl, final_g, n_s,
                       f"sample_ffn_{l}").reshape(n_s, d)
        lru_s.append(lru)
        conv_s.append(jnp.swapaxes(conv, 0, 1))

    return (xp, xs.reshape(n_s, 1, d),
            jnp.stack(hg_p), jnp.stack(lru_p), jnp.stack(conv_p),
            hg_s, jnp.stack(lru_s), jnp.stack(conv_s))
```

```python
import functools

import numpy as np
import jax
import jax.numpy as jnp
from jax import lax
from jax.experimental import pallas as pl
from jax.experimental.pallas import tpu as pltpu

F32 = jnp.float32
BF16 = jnp.bfloat16

D_MODEL = 1024
DEPTH = 2
HG_HEADS = 8
HG_DK = 128
HG_DV = 128
LB_FLOOR = 1e-30
LRU_BLOCKS = 8
LRU_BS = 128
LRU_C = 8.0
CONV_W = 4
D_FF = 4 * D_MODEL
EPS = 1e-6
N_IN = 8 * D_MODEL
F32_TINY = float(np.finfo(np.float32).tiny)

LANES = 128
SUBLANES = 8
HG_CHUNK = 128
HG_LEVELS = 7
MIX_TL = 256
FFN_TM = 512
SAMPLE_GROUP = 8
VMEM_LIMIT = 56 * 1024 * 1024


def _sigmoid(x):
    return 0.5 * jnp.tanh(0.5 * x) + 0.5


def _silu(x):
    h = 0.5 * x
    return h * jnp.tanh(h) + h


def _gelu_tanh(x):
    c = 0.7978845608028654
    h = 0.5 * x
    return h * jnp.tanh(x * (c + (c * 0.044715) * (x * x))) + h


def _half_tanh_gate(t, y):
    h = 0.5 * y
    return h * t + h


def _log1p(x):
    u = 1.0 + x
    return jnp.where(u == 1.0, x, jnp.log(u) * (x / (u - 1.0)))


def _softplus(z):
    return jnp.maximum(z, 0.0) + _log1p(jnp.exp(-jnp.abs(z)))


def _rmsnorm(x, g):
    ms = jnp.mean(x * x, axis=-1, keepdims=True)
    return x * lax.rsqrt(ms + EPS) * g


def _dot(a, b):
    return jnp.dot(a, b, preferred_element_type=F32)


def _dot_nt(a, b):
    return lax.dot_general(a, b, (((1,), (1,)), ((), ())), preferred_element_type=F32)


def _dot_tn(a, b):
    return lax.dot_general(a, b, (((0,), (0,)), ((), ())), preferred_element_type=F32)


def _lower_bound(hgl, layer):
    m = jnp.max(hgl, axis=0, keepdims=True)
    e = jnp.exp(hgl - m)
    p = e / jnp.sum(e, axis=0, keepdims=True)
    cum = p[0:1]
    for j in range(1, layer + 1):
        cum = cum + p[j:j + 1]
    return cum - p[0:1]


def _forget_key(fa, lb):
    e = jnp.exp(-jnp.abs(fa))
    r = 1.0 / (1.0 + e)
    er = e * r
    pos = fa >= 0.0
    sig_p = jnp.where(pos, r, er)
    sig_n = jnp.where(pos, er, r)
    f = jnp.maximum(lb, LB_FLOOR) + (1.0 - lb) * sig_p
    k = (1.0 - lb) * sig_n
    return f, k


def _lru_coeffs(xc, gates, ba, bx, sp):
    tr = jnp.tanh(0.5 * gates[:, :LRU_BS] + 0.5 * ba)
    ti = jnp.tanh(0.5 * gates[:, LRU_BS:] + 0.5 * bx)
    c1 = (-0.5 * LRU_C) * sp
    log_a = c1 * tr + c1
    a = jnp.exp(log_a)
    th = jnp.tanh(log_a)
    one_m_a2 = (-2.0 * th) / (1.0 - th)
    root = one_m_a2 * lax.rsqrt(jnp.maximum(one_m_a2, F32_TINY))
    return a, root * _half_tanh_gate(ti, xc)


def _mod_kernel(c_ref, w_ref, b_ref, o_ref):
    cs = _silu(c_ref[...]).astype(BF16)
    o_ref[0] = _dot(cs, w_ref[0].astype(BF16)) + b_ref[0]


def _mod_call(c_all, w_mod, b_mod):
    n_rows = c_all.shape[0]
    tn = 1536
    return pl.pallas_call(
        _mod_kernel,
        out_shape=jax.ShapeDtypeStruct((DEPTH, n_rows, 6 * D_MODEL), F32),
        grid=(DEPTH, 6 * D_MODEL // tn),
        in_specs=[
            pl.BlockSpec((n_rows, D_MODEL), lambda l, j: (0, 0)),
            pl.BlockSpec((1, D_MODEL, tn), lambda l, j: (l, 0, j)),
            pl.BlockSpec((1, 1, tn), lambda l, j: (l, 0, j)),
        ],
        out_specs=pl.BlockSpec((1, n_rows, tn), lambda l, j: (l, 0, j)),
        compiler_params=pltpu.CompilerParams(
            dimension_semantics=("arbitrary", "arbitrary"), vmem_limit_bytes=VMEM_LIMIT),
        name="mod",
    )(c_all, w_mod, b_mod.reshape(DEPTH, 1, 6 * D_MODEL))


def _hg_tables():
    t = np.arange(HG_CHUNK)
    x = t[:, None] ^ t[None, :]
    hb = np.floor(np.log2(np.maximum(x, 1))).astype(np.int32)
    lvl = np.where(t[:, None] > t[None, :], hb + 1, np.where(t[:, None] == t[None, :], 0, -1)).astype(np.int32)
    return jnp.asarray(lvl)


def _x_within_group(q, k, g2, level):
    n = q.shape[0]
    shp = (n // SUBLANES, SUBLANES, LANES)
    q3, k3, g3 = q.reshape(shp), k.reshape(shp), g2.reshape(shp)
    sub = lax.broadcasted_iota(jnp.int32, (1, SUBLANES, LANES), 1)
    qside = ((sub >> level) & 1) == 1
    if level == 0:
        scaled = jnp.where(qside, q3 * jnp.exp2(g3 - pltpu.roll(g3, 1, axis=1)), k3)
        return scaled.reshape(n, LANES)

    def row(i):
        return jnp.broadcast_to(g3[:, i:i + 1, :], shp)

    piv = jnp.where(sub < 4, row(1), row(5)) if level == 1 else row(3)
    e = jnp.exp2(-jnp.abs(g3 - piv))
    return (jnp.where(qside, q3, k3) * e).reshape(n, LANES)


def _x_across_groups(q, k, g2, level):
    n = q.shape[0]
    b = 1 << level
    blocks = []
    for base in range(0, n, 2 * b):
        lo, mid, hi = base, base + b, base + 2 * b
        gp = g2[mid - 1:mid, :]
        blocks.append(k[lo:mid] * jnp.exp2(gp - g2[lo:mid]))
        blocks.append(q[mid:hi] * jnp.exp2(g2[mid:hi] - gp))
    return jnp.concatenate(blocks, axis=0)


HG_FULL_LEVELS = 4


def _hg_level_products(q, k, g2):
    c = HG_CHUNK
    prods = [_dot_nt(q.astype(BF16), k.astype(BF16))]
    for level in range(HG_LEVELS):
        b = 1 << level
        if b < SUBLANES:
            x = _x_within_group(q, k, g2, level).astype(BF16)
        else:
            x = _x_across_groups(q, k, g2, level).astype(BF16)
        if level < HG_FULL_LEVELS:
            lhs = x
        else:
            lhs = jnp.concatenate([x[base + b:base + 2 * b] for base in range(0, c, 2 * b)], axis=0)
        prods.append(_dot_nt(lhs, x))
    return prods


def _hg_assemble(prods_per_head, lvl_ref):
    c = HG_CHUNK
    n_heads = len(prods_per_head)
    groups = [[] for _ in range(n_heads)]
    for gi in range(c // SUBLANES):
        r0 = gi * SUBLANES
        rows = slice(r0, r0 + SUBLANES)
        lv = lvl_ref[rows, :]
        on_diag = lv == 0
        a_g = [jnp.where(on_diag, p[0][rows], 0.0) for p in prods_per_head]
        for level in range(HG_LEVELS):
            b = 1 << level
            if level < HG_FULL_LEVELS:
                src = r0
            elif r0 & b:
                src = (r0 >> (level + 1)) * b + (r0 & (b - 1))
            else:
                continue
            at_level = lv == level + 1
            a_g = [jnp.where(at_level, p[level + 1][src:src + SUBLANES], a)
                   for p, a in zip(prods_per_head, a_g)]
        for h in range(n_heads):
            groups[h].append(a_g[h])
    return [jnp.concatenate(g, axis=0).astype(BF16) for g in groups]


def _hg_finish(q, k, v, g2, st, a):
    c = HG_CHUNK
    g_last = g2[c - 1:c, :]
    qg = (q * jnp.exp2(g2)).astype(BF16)
    vb = v.astype(BF16)
    o = _dot_nt(qg, st.astype(BF16)) + _dot(a, vb)
    kd = (k * jnp.exp2(g_last - g2)).astype(BF16)
    st_new = st * jnp.exp2(g_last) + _dot_tn(vb, kd)
    return o, st_new


def _causal_conv(xb, prev8, cw_ref, cb_ref):
    t, w = xb.shape
    x3 = xb.reshape(t // SUBLANES, SUBLANES, w)
    sub = lax.broadcasted_iota(jnp.int32, (1, SUBLANES, w), 1)

    def tap(j):
        return cw_ref[j:j + 1, :][None]

    out = cb_ref[...][None] + tap(CONV_W - 1) * x3
    for back in range(1, CONV_W):
        r = pltpu.roll(x3, back, axis=1)
        r_prev_group = jnp.concatenate([pltpu.roll(prev8, back, axis=0)[None], r[:-1]], axis=0)
        shifted = jnp.where(sub >= back, r, r_prev_group)
        out = out + tap(CONV_W - 1 - back) * shifted
    return out.reshape(t, w)


def _chunk_cumsum(y):
    t = y.shape[0]
    y3 = y.reshape(t // SUBLANES, SUBLANES, LANES)
    sub = lax.broadcasted_iota(jnp.int32, (1, SUBLANES, LANES), 1)
    d = 1
    while d < SUBLANES:
        y3 = y3 + jnp.where(sub >= d, pltpu.roll(y3, d, axis=1), 0.0)
        d *= 2
    per_chunk = HG_CHUNK // SUBLANES
    out = []
    for gi in range(t // SUBLANES):
        blk = y3[gi]
        if gi % per_chunk:
            blk = blk + out[-1][SUBLANES - 1:SUBLANES, :]
        out.append(blk)
    return jnp.concatenate(out, axis=0)


def _lru_scan(a, b, h0):
    t = a.shape[0]
    a3 = a.reshape(t // SUBLANES, SUBLANES, LANES)
    b3 = b.reshape(t // SUBLANES, SUBLANES, LANES)
    sub = lax.broadcasted_iota(jnp.int32, a3.shape, 1)
    d = 1
    while d < SUBLANES:
        m = sub >= d
        b3 = a3 * jnp.where(m, pltpu.roll(b3, d, axis=1), 0.0) + b3
        a3 = a3 * jnp.where(m, pltpu.roll(a3, d, axis=1), 1.0)
        d *= 2
    hs = []
    h = h0
    for gidx in range(t // SUBLANES):
        hg = a3[gidx] * h + b3[gidx]
        hs.append(hg)
        h = hg[SUBLANES - 1:SUBLANES, :]
    return jnp.concatenate(hs, axis=0), h


def _prompt_mixer_kernel(layer, n_t,
                         x_ref, mod_ref, hgl_ref, n1g_ref, win_ref, ong_ref, cw_ref, cb_ref, wax_ref,
                         ba_ref, bx_ref, lam_ref, wba_ref, wbb_ref, wout_ref, lvl_ref,
                         xo_ref, hg_ref, lru_ref, conv_ref,
                         q_s, k_s, v_s, g_s, og_s, oa_s, ob_s, prev8_s, st_s, hl_s, xc_s, yb_s, ga_s, gb_s, gr_s,
                         gi_s):
    tl = MIX_TL
    d = D_MODEL
    t_idx = pl.program_id(1)

    @pl.when(t_idx == 0)
    def _():
        st_s[...] = jnp.zeros(st_s.shape, F32)
        hl_s[...] = jnp.zeros(hl_s.shape, F32)
        prev8_s[...] = jnp.zeros(prev8_s.shape, F32)

    x = x_ref[0]
    mod = mod_ref[0]
    sh1, sc1, g1 = mod[:, 0:d], mod[:, d:2 * d], mod[:, 2 * d:3 * d]
    hb = (_rmsnorm(x, n1g_ref[...]) * (1.0 + sc1) + sh1).astype(BF16)

    def proj(j):
        return _dot(hb, win_ref[:, j * d:(j + 1) * d])

    def heads(val, dst):
        for h in range(HG_HEADS):
            dst[h] = val[:, h * LANES:(h + 1) * LANES]

    sp = _softplus(-lam_ref[...])
    ong = ong_ref[...]

    def lru_block(n):
        cols = slice(n * LRU_BS, (n + 1) * LRU_BS)
        gates = jnp.concatenate([gr_s[:, cols], gi_s[:, cols]], axis=1)
        a, bt = _lru_coeffs(xc_s[:, cols], gates, ba_ref[:, cols], bx_ref[:, cols], sp[:, cols])
        hseq, h_last = _lru_scan(a, bt, hl_s[:, cols])
        hl_s[:, cols] = h_last
        ob_s[:, cols] = hseq * yb_s[:, cols]

    def level_products(c):
        rows = slice(c * HG_CHUNK, (c + 1) * HG_CHUNK)
        return [_hg_level_products(q_s[h, rows, :], k_s[h, rows, :], g_s[h, rows, :]) for h in range(HG_HEADS)]

    def finish(c, a_mats):
        rows = slice(c * HG_CHUNK, (c + 1) * HG_CHUNK)
        for h in range(HG_HEADS):
            o, st_new = _hg_finish(q_s[h, rows, :], k_s[h, rows, :], v_s[h, rows, :], g_s[h, rows, :],
                                   st_s[h], a_mats[h])
            st_s[h] = st_new
            oa_s[h, rows, :] = _rmsnorm(o, ong) * og_s[h, rows, :]

    xb = proj(4)
    xc = _causal_conv(xb, prev8_s[...], cw_ref, cb_ref)
    xc_s[...] = xc
    prev8_s[...] = xb[tl - SUBLANES:tl, :]
    conv_ref[0] = xb[tl - (CONV_W - 1):tl, :]
    for n in range(LRU_BLOCKS):
        cols = slice(n * LRU_BS, (n + 1) * LRU_BS)
        gates = _dot(xc[:, cols].astype(BF16), wax_ref[n])
        gr_s[:, cols] = gates[:, :LRU_BS]
        gi_s[:, cols] = gates[:, LRU_BS:]
    yb_s[...] = _gelu_tanh(proj(5))
    lb = _lower_bound(hgl_ref[...], layer)
    f, k = _forget_key(proj(1), lb)
    heads(k, k_s)
    log2f = jnp.minimum(jnp.log2(f), 0.0)
    for h in range(HG_HEADS):
        g_s[h] = _chunk_cumsum(log2f[:, h * LANES:(h + 1) * LANES])
    lru_block(0)
    lru_block(1)
    heads(_silu(proj(0)), q_s)
    lru_block(2)
    lru_block(3)
    heads(proj(2), v_s)
    lru_block(4)
    lru_block(5)
    heads(_silu(proj(3)), og_s)
    lru_block(6)
    lru_block(7)
    lru_ref[0] = hl_s[...]
    ga_s[...] = jnp.tanh(0.5 * proj(6))
    gb_s[...] = jnp.tanh(0.5 * proj(7))
    for c in range(tl // HG_CHUNK):
        finish(c, _hg_assemble(level_products(c), lvl_ref))

    o_a = jnp.concatenate([oa_s[h] for h in range(HG_HEADS)], axis=1).astype(BF16)
    merged = (_half_tanh_gate(ga_s[...], _dot(o_a, wba_ref[...]))
              + _half_tanh_gate(gb_s[...], _dot(ob_s[...].astype(BF16), wbb_ref[...])))
    xo_ref[0] = x + g1 * _dot(merged.astype(BF16), wout_ref[...])

    @pl.when(t_idx == n_t - 1)
    def _():
        for h in range(HG_HEADS):
            hg_ref[0, h] = st_s[h].T


def _const_spec(shape, layer=None):
    nd = len(shape)
    if layer is None:
        return pl.BlockSpec(shape, lambda *_: (0,) * nd, pipeline_mode=pl.Buffered(1))
    return pl.BlockSpec((None,) + tuple(shape), lambda *_: (layer,) + (0,) * nd, pipeline_mode=pl.Buffered(1))


def _prompt_mixer_call(layer, x, mod_p, hg_lower, wl, lvl):
    bsz, seq, d = x.shape
    tl = MIX_TL
    n_t = seq // tl
    hshape = (HG_HEADS, tl, LANES)
    out_shapes = (
        jax.ShapeDtypeStruct((bsz, seq, d), F32),
        jax.ShapeDtypeStruct((bsz, HG_HEADS, HG_DK, HG_DV), F32),
        jax.ShapeDtypeStruct((bsz, 1, d), F32),
        jax.ShapeDtypeStruct((bsz, CONV_W - 1, d), F32),
    )
    in_specs = [
        pl.BlockSpec((1, tl, d), lambda b, t: (b, t, 0)),
        pl.BlockSpec((1, 1, 6 * d), lambda b, t: (b, 0, 0)),
        _const_spec((DEPTH, d)),
        _const_spec((1, d), layer),
        _const_spec((d, N_IN), layer),
        _const_spec((1, HG_DV), layer),
        _const_spec((CONV_W, d), layer),
        _const_spec((1, d), layer),
        _const_spec((LRU_BLOCKS, LRU_BS, 2 * LRU_BS), layer),
        _const_spec((1, d), layer),
        _const_spec((1, d), layer),
        _const_spec((1, d), layer),
        _const_spec((d, d), layer),
        _const_spec((d, d), layer),
        _const_spec((d, d), layer),
        _const_spec((HG_CHUNK, HG_CHUNK)),
    ]
    out_specs = (
        pl.BlockSpec((1, tl, d), lambda b, t: (b, t, 0)),
        pl.BlockSpec((1, HG_HEADS, HG_DK, HG_DV), lambda b, t: (b, 0, 0, 0)),
        pl.BlockSpec((1, 1, d), lambda b, t: (b, 0, 0)),
        pl.BlockSpec((1, CONV_W - 1, d), lambda b, t: (b, 0, 0)),
    )
    scratch = [
        pltpu.VMEM(hshape, F32),
        pltpu.VMEM(hshape, F32),
        pltpu.VMEM(hshape, F32),
        pltpu.VMEM(hshape, F32),
        pltpu.VMEM(hshape, F32),
        pltpu.VMEM(hshape, F32),
        pltpu.VMEM((tl, d), F32),
        pltpu.VMEM((SUBLANES, d), F32),
        pltpu.VMEM((HG_HEADS, HG_DV, HG_DK), F32),
        pltpu.VMEM((1, d), F32),
        pltpu.VMEM((tl, d), F32),
        pltpu.VMEM((tl, d), F32),
        pltpu.VMEM((tl, d), F32),
        pltpu.VMEM((tl, d), F32),
        pltpu.VMEM((tl, d), F32),
        pltpu.VMEM((tl, d), F32),
    ]
    return pl.pallas_call(
        functools.partial(_prompt_mixer_kernel, layer, n_t),
        out_shape=out_shapes,
        grid=(bsz, n_t),
        in_specs=in_specs,
        out_specs=out_specs,
        scratch_shapes=scratch,
        compiler_params=pltpu.CompilerParams(
            dimension_semantics=("arbitrary", "arbitrary"), vmem_limit_bytes=VMEM_LIMIT),
        name=f"prompt_mixer_{layer}",
    )(x, mod_p, hg_lower, wl["n1g"], wl["w_in"], wl["ong"], wl["cw"], wl["cb"], wl["wax"],
      wl["ba"], wl["bx"], wl["lam"], wl["wba"], wl["wbb"], wl["wout"], lvl)


def _ffn_kernel(final, x_ref, mod_ref, n2g_ref, w1_ref, w2_ref, fg_ref, o_ref):
    d = D_MODEL
    x = x_ref[0]
    mod = mod_ref[0]
    sh2, sc2, g2 = mod[:, 3 * d:4 * d], mod[:, 4 * d:5 * d], mod[:, 5 * d:6 * d]
    hb = (_rmsnorm(x, n2g_ref[...]) * (1.0 + sc2) + sh2).astype(BF16)
    acc = jnp.zeros(x.shape, F32)
    for c in range(D_FF // d):
        h1 = jnp.maximum(_dot(hb, w1_ref[:, c * d:(c + 1) * d]), 0.0)
        acc = acc + _dot((h1 * h1).astype(BF16), w2_ref[c * d:(c + 1) * d, :])
    xn = x + g2 * acc
    if final:
        xn = _rmsnorm(xn, fg_ref[...])
    o_ref[0] = xn


def _ffn_call(layer, x, mod, wl, final_g, tm, name):
    final = layer == DEPTH - 1
    bsz, seq, d = x.shape
    mrows = mod.shape[1]
    mod_spec = (pl.BlockSpec((1, 1, 6 * d), lambda b, t: (b, 0, 0)) if mrows == 1
                else pl.BlockSpec((1, tm, 6 * d), lambda b, t: (b, t, 0)))
    return pl.pallas_call(
        functools.partial(_ffn_kernel, final),
        out_shape=jax.ShapeDtypeStruct((bsz, seq, d), F32),
        grid=(bsz, seq // tm),
        in_specs=[
            pl.BlockSpec((1, tm, d), lambda b, t: (b, t, 0)),
            mod_spec,
            _const_spec((1, d), layer),
            _const_spec((d, D_FF), layer),
            _const_spec((D_FF, d), layer),
            _const_spec((1, d)),
        ],
        out_specs=pl.BlockSpec((1, tm, d), lambda b, t: (b, t, 0)),
        compiler_params=pltpu.CompilerParams(
            dimension_semantics=("arbitrary", "arbitrary"), vmem_limit_bytes=VMEM_LIMIT),
        name=name,
    )(x, mod, wl["n2g"], wl["w1"], wl["w2"], final_g)


def _sample_inproj_kernel(x_ref, mod_ref, n1g_ref, w_ref, o_ref):
    d = D_MODEL
    mod = mod_ref[...]
    hb = (_rmsnorm(x_ref[...], n1g_ref[...]) * (1.0 + mod[:, d:2 * d]) + mod[:, 0:d]).astype(BF16)
    o_ref[...] = _dot(hb, w_ref[...])


def _sample_inproj_call(layer, x, mod_s, wl):
    n, d = x.shape
    return pl.pallas_call(
        _sample_inproj_kernel,
        out_shape=jax.ShapeDtypeStruct((n, N_IN), F32),
        grid=(N_IN // d,),
        in_specs=[
            pl.BlockSpec((n, d), lambda j: (0, 0)),
            pl.BlockSpec((n, 6 * d), lambda j: (0, 0)),
            pl.BlockSpec((None, 1, d), lambda j: (layer, 0, 0)),
            pl.BlockSpec((None, d, d), lambda j: (layer, 0, j)),
        ],
        out_specs=pl.BlockSpec((n, d), lambda j: (0, j)),
        compiler_params=pltpu.CompilerParams(
            dimension_semantics=("arbitrary",), vmem_limit_bytes=VMEM_LIMIT),
        name=f"sample_inproj_{layer}",
    )(x, mod_s, wl["n1g"], wl["w_in"])


N_SAMPLE_MIXER_INPUTS = 20

OUTER_ROWS = 16
OUTER_LEFT_SRC = (0, 0, 1, 0, 2, 1, 3, 4, 5)
OUTER_RIGHT_SRC = (0, 1, 0, 2, 0, 1)


def _outer_tables():
    g = SAMPLE_GROUP
    sel_l = np.zeros((g * OUTER_ROWS, 6 * g), np.float32)
    sel_r = np.zeros((g * OUTER_ROWS, 4 * g), np.float32)
    for j in range(g):
        for r, a in enumerate(OUTER_LEFT_SRC):
            sel_l[OUTER_ROWS * j + r, g * a + j] = 1.0
        for r, a in enumerate(OUTER_RIGHT_SRC):
            sel_r[OUTER_ROWS * j + r, g * a + j] = 1.0
    ones_rows = np.zeros((g, OUTER_ROWS, LANES), np.float32)
    ones_rows[:, len(OUTER_RIGHT_SRC):len(OUTER_LEFT_SRC)] = 1.0
    return (jnp.asarray(sel_l, BF16), jnp.asarray(sel_r, BF16),
            jnp.asarray(ones_rows.reshape(g * OUTER_ROWS, LANES), BF16))


def _split3(x):
    x1 = x.astype(BF16).astype(F32)
    r = x - x1
    x2 = r.astype(BF16).astype(F32)
    x3 = (r - x2).astype(BF16).astype(F32)
    return [x1, x2, x3]


def _sample_mixer_kernel(layer, n_groups, *refs):
    refs = refs[:N_SAMPLE_MIXER_INPUTS] + refs[len(refs) - 10:]
    (u_ref, x_ref, mod_ref, hg_ref, lru_ref, conv_ref, hgl_ref, ong_ref, cw_ref, cb_ref,
     wax_ref, ba_ref, bx_ref, lam_ref, wba_ref, wbb_ref, wout_ref, sell_ref, selr_ref, ones_ref,
     xo_ref, hgo_ref, lruo_ref, convo_ref,
     f_s, k_s, q_s, v_s, o_s, ob_s) = refs
    d = D_MODEL
    n = x_ref.shape[0]
    i = pl.program_id(0)
    if layer == 0:
        for other in range(1, DEPTH):
            hgo_ref[other] = jnp.zeros(hgo_ref.shape[1:], F32)
        hgo_ref = hgo_ref.at[0]

    @pl.when(i == 0)
    def _():
        q_s[...] = _silu(u_ref[:, 0:d])
        lb = _lower_bound(hgl_ref[...], layer)
        f, k = _forget_key(u_ref[:, d:2 * d], lb)
        f_s[...] = f
        k_s[...] = k
        v_s[...] = u_ref[:, 2 * d:3 * d]
        xb = u_ref[:, 4 * d:5 * d]
        cw = cw_ref[...]
        xc = cb_ref[...] + cw[3:4] * xb
        for j in range(CONV_W - 1):
            xc = xc + cw[j:j + 1] * conv_ref[j]
        convo_ref[0] = conv_ref[1]
        convo_ref[1] = conv_ref[2]
        convo_ref[2] = xb
        yb = _gelu_tanh(u_ref[:, 5 * d:6 * d])
        sp = _softplus(-lam_ref[...])
        for blk in range(LRU_BLOCKS):
            cols = slice(blk * LRU_BS, (blk + 1) * LRU_BS)
            xcn = xc[:, cols]
            gates = _dot(xcn.astype(BF16), wax_ref[blk])
            a, bt = _lru_coeffs(xcn, gates, ba_ref[:, cols], bx_ref[:, cols], sp[:, cols])
            hn = a * lru_ref[:, cols] + bt
            lruo_ref[:, cols] = hn
            ob_s[:, cols] = hn * yb[:, cols]

    rows = pl.ds(pl.multiple_of(i * SAMPLE_GROUP, SAMPLE_GROUP), SAMPLE_GROUP)
    ones_rows = ones_ref[...]

    def outer_products(h):
        cols = slice(h * LANES, (h + 1) * LANES)
        vparts = _split3(v_s[rows, cols])
        left = jnp.concatenate(_split3(k_s[rows, cols]) + _split3(f_s[rows, cols]), axis=0).astype(BF16)
        right = jnp.concatenate(vparts + [jnp.zeros_like(vparts[0])], axis=0).astype(BF16)
        left_t = _dot(sell_ref[...], left).T.astype(BF16)
        right_rows = jnp.concatenate([_dot(selr_ref[...], right).astype(BF16), ones_rows], axis=1)
        out = []
        for j in range(SAMPLE_GROUP):
            lo, hi = j * OUTER_ROWS, (j + 1) * OUTER_ROWS
            pieces = [right_rows[lo:hi]]
            if lo:
                pieces.insert(0, jnp.zeros((lo, 2 * HG_DV), BF16))
            if hi < SAMPLE_GROUP * OUTER_ROWS:
                pieces.append(jnp.zeros((SAMPLE_GROUP * OUTER_ROWS - hi, 2 * HG_DV), BF16))
            out.append(_dot(left_t, jnp.concatenate(pieces, axis=0)))
        return out

    kv_f = outer_products(0)
    for h in range(HG_HEADS):
        cols = slice(h * LANES, (h + 1) * LANES)
        kv_f_next = outer_products(h + 1) if h + 1 < HG_HEADS else None
        s_new = [hg_ref[j, h] * kv_f[j][:, HG_DV:] + kv_f[j][:, :HG_DV] for j in range(SAMPLE_GROUP)]
        for j in range(SAMPLE_GROUP):
            hgo_ref[j, h] = s_new[j]
        qg = q_s[rows, cols].astype(BF16)
        o_rows = [_dot(qg, s_new[j].astype(BF16))[j:j + 1, :] for j in range(SAMPLE_GROUP)]
        o_s[rows, cols] = jnp.concatenate(o_rows, axis=0)
        kv_f = kv_f_next

    @pl.when(i == n_groups - 1)
    def _():
        ong = ong_ref[...]
        parts = []
        for h in range(HG_HEADS):
            cols = slice(h * LANES, (h + 1) * LANES)
            parts.append(_rmsnorm(o_s[:, cols], ong) * _silu(u_ref[:, 3 * d + h * LANES:3 * d + (h + 1) * LANES]))
        o_a = jnp.concatenate(parts, axis=1).astype(BF16)
        merged = (_sigmoid(u_ref[:, 6 * d:7 * d]) * _dot(o_a, wba_ref[...])
                  + _sigmoid(u_ref[:, 7 * d:8 * d]) * _dot(ob_s[...].astype(BF16), wbb_ref[...]))
        g1 = mod_ref[:, 2 * d:3 * d]
        xo_ref[...] = x_ref[...] + g1 * _dot(merged.astype(BF16), wout_ref[...])


def _sample_mixer_call(layer, u, x, mod_s, st_hg_all, st_lru_all, st_conv_t, hg_lower, wl, hg_out_prev):
    n, d = x.shape
    grp = SAMPLE_GROUP
    n_groups = n // grp

    def full(shape):
        nd = len(shape)
        return pl.BlockSpec(shape, lambda i: (0,) * nd)

    def of_layer(shape):
        nd = len(shape)
        return pl.BlockSpec((None,) + tuple(shape), lambda i: (layer,) + (0,) * nd)

    st_spec = pl.BlockSpec((None, grp, HG_HEADS, HG_DK, HG_DV), lambda i: (layer, i, 0, 0, 0))
    out_shapes = (
        jax.ShapeDtypeStruct((n, d), F32),
        jax.ShapeDtypeStruct((DEPTH, n, HG_HEADS, HG_DK, HG_DV), F32),
        jax.ShapeDtypeStruct((n, d), F32),
        jax.ShapeDtypeStruct((CONV_W - 1, n, d), F32),
    )
    in_specs = [
        full((n, N_IN)), full((n, d)), full((n, 6 * d)), st_spec,
        of_layer((n, d)), full((CONV_W - 1, n, d)),
        full((DEPTH, d)), of_layer((1, HG_DV)), of_layer((CONV_W, d)), of_layer((1, d)),
        of_layer((LRU_BLOCKS, LRU_BS, 2 * LRU_BS)), of_layer((1, d)), of_layer((1, d)), of_layer((1, d)),
        of_layer((d, d)), of_layer((d, d)), of_layer((d, d)),
    ]
    outer_tables = _outer_tables()
    in_specs += [full(t.shape) for t in outer_tables]
    assert len(in_specs) == N_SAMPLE_MIXER_INPUTS
    args = [u, x, mod_s, st_hg_all, st_lru_all, st_conv_t, hg_lower, wl["ong"], wl["cw"], wl["cb"], wl["wax"],
            wl["ba"], wl["bx"], wl["lam"], wl["wba"], wl["wbb"], wl["wout"], *outer_tables]
    aliases = {}
    if hg_out_prev is not None:
        in_specs.append(pl.BlockSpec(memory_space=pl.ANY))
        args.append(hg_out_prev)
        aliases = {N_SAMPLE_MIXER_INPUTS: 1}
    st_out_spec = st_spec if layer else pl.BlockSpec((DEPTH, grp, HG_HEADS, HG_DK, HG_DV),
                                                     lambda i: (0, i, 0, 0, 0))
    out_specs = (full((n, d)), st_out_spec, full((n, d)), full((CONV_W - 1, n, d)))
    scratch = [
        pltpu.VMEM((n, d), F32),
        pltpu.VMEM((n, d), F32),
        pltpu.VMEM((n, d), F32),
        pltpu.VMEM((n, d), F32),
        pltpu.VMEM((n, d), F32),
        pltpu.VMEM((n, d), F32),
    ]
    return pl.pallas_call(
        functools.partial(_sample_mixer_kernel, layer, n_groups),
        out_shape=out_shapes,
        grid=(n_groups,),
        in_specs=in_specs,
        out_specs=out_specs,
        scratch_shapes=scratch,
        input_output_aliases=aliases,
        compiler_params=pltpu.CompilerParams(
            dimension_semantics=("arbitrary",), vmem_limit_bytes=VMEM_LIMIT),
        name=f"sample_mixer_{layer}",
    )(*args)


def _stacked_weights(norm1_g, norm2_g, w_in, hg_onorm_g, lru_conv_w, lru_conv_b, lru_wa, lru_ba, lru_wx,
                     lru_bx, lru_lambda, w_branch_a, w_branch_b, w_out, w_ff1, w_ff2):
    d = D_MODEL
    return {
        "n1g": norm1_g.reshape(DEPTH, 1, d),
        "n2g": norm2_g.reshape(DEPTH, 1, d),
        "w_in": w_in.astype(BF16),
        "ong": hg_onorm_g.reshape(DEPTH, 1, HG_DV),
        "cw": lru_conv_w,
        "cb": lru_conv_b.reshape(DEPTH, 1, d),
        "wax": jnp.concatenate([lru_wa, lru_wx], axis=-1).astype(BF16),
        "ba": lru_ba.reshape(DEPTH, 1, d),
        "bx": lru_bx.reshape(DEPTH, 1, d),
        "lam": lru_lambda.reshape(DEPTH, 1, d),
        "wba": w_branch_a.astype(BF16),
        "wbb": w_branch_b.astype(BF16),
        "wout": w_out.astype(BF16),
        "w1": w_ff1.astype(BF16),
        "w2": w_ff2.astype(BF16),
    }


def kernel(x_prompt, x_sample, state_hgrn, state_rglru, state_conv, c_prompt, c_sample, w_mod, b_mod, norm1_g, norm2_g, w_in, hg_lower, hg_onorm_g, lru_conv_w, lru_conv_b, lru_wa, lru_ba, lru_wx, lru_bx, lru_lambda, w_branch_a, w_branch_b, w_out, w_ff1, w_ff2, final_norm_g):
    bsz, seq, d = x_prompt.shape
    n_s = x_sample.shape[0]
    assert d == D_MODEL and x_sample.shape[1] == 1 and seq % MIX_TL == 0 and seq % FFN_TM == 0
    assert n_s % SAMPLE_GROUP == 0 and n_s == LANES

    mod = _mod_call(jnp.concatenate([c_prompt, c_sample], axis=0), w_mod, b_mod)
    lvl = _hg_tables()
    final_g = final_norm_g.reshape(1, d)

    xp = x_prompt
    xs = x_sample.reshape(n_s, d)
    hg_p, lru_p, conv_p, lru_s, conv_s = [], [], [], [], []
    hg_s = None
    wl = _stacked_weights(norm1_g, norm2_g, w_in, hg_onorm_g, lru_conv_w, lru_conv_b, lru_wa, lru_ba,
                          lru_wx, lru_bx, lru_lambda, w_branch_a, w_branch_b, w_out, w_ff1, w_ff2)
    for l in range(DEPTH):
        mod_p = mod[l, :bsz].reshape(bsz, 1, 6 * d)
        mod_s = mod[l, bsz:]

        xp, hg, lru, conv = _prompt_mixer_call(l, xp, mod_p, hg_lower, wl, lvl)
        xp = _ffn_call(l, xp, mod_p, wl, final_g, FFN_TM, f"prompt_ffn_{l}")
        hg_p.append(hg)
        lru_p.append(lru.reshape(bsz, d))
        conv_p.append(conv)

        u = _sample_inproj_call(l, xs, mod_s, wl)
        xs, hg_s, lru, conv = _sample_mixer_call(l, u, xs, mod_s, state_hgrn, state_rglru,
                                                 jnp.swapaxes(state_conv[l], 0, 1), hg_lower, wl, hg_s)
        xs = _ffn_call(l, xs.reshape(1, n_s, d), mod_s.reshape(1, n_s, 6 * d), wl, final_g, n_s,
                       f"sample_ffn_{l}").reshape(n_s, d)
        lru_s.append(lru)
        conv_s.append(jnp.swapaxes(conv, 0, 1))

    return (xp, xs.reshape(n_s, 1, d),
            jnp.stack(hg_p), jnp.stack(lru_p), jnp.stack(conv_p),
            hg_s, jnp.stack(lru_s), jnp.stack(conv_s))
```

```python
import functools

import numpy as np
import jax
import jax.numpy as jnp
from jax import lax
from jax.experimental import pallas as pl
from jax.experimental.pallas import tpu as pltpu

F32 = jnp.float32
BF16 = jnp.bfloat16

D_MODEL = 1024
DEPTH = 2
HG_HEADS = 8
HG_DK = 128
HG_DV = 128
LB_FLOOR = 1e-30
LRU_BLOCKS = 8
LRU_BS = 128
LRU_C = 8.0
CONV_W = 4
D_FF = 4 * D_MODEL
EPS = 1e-6
N_IN = 8 * D_MODEL
F32_TINY = float(np.finfo(np.float32).tiny)

LANES = 128
SUBLANES = 8
HG_CHUNK = 128
HG_LEVELS = 7
MIX_TL = 256
FFN_TM = 512
SAMPLE_GROUP = 8
VMEM_LIMIT = 56 * 1024 * 1024


def _sigmoid(x):
    return 0.5 * jnp.tanh(0.5 * x) + 0.5


def _silu(x):
    h = 0.5 * x
    return h * jnp.tanh(h) + h


def _gelu_tanh(x):
    c = 0.7978845608028654
    h = 0.5 * x
    return h * jnp.tanh(x * (c + (c * 0.044715) * (x * x))) + h


def _half_tanh_gate(t, y):
    h = 0.5 * y
    return h * t + h


def _log1p(x):
    u = 1.0 + x
    return jnp.where(u == 1.0, x, jnp.log(u) * (x / (u - 1.0)))


def _softplus(z):
    return jnp.maximum(z, 0.0) + _log1p(jnp.exp(-jnp.abs(z)))


def _rmsnorm(x, g):
    ms = jnp.mean(x * x, axis=-1, keepdims=True)
    return x * lax.rsqrt(ms + EPS) * g


def _dot(a, b):
    return jnp.dot(a, b, preferred_element_type=F32)


def _dot_nt(a, b):
    return lax.dot_general(a, b, (((1,), (1,)), ((), ())), preferred_element_type=F32)


def _dot_tn(a, b):
    return lax.dot_general(a, b, (((0,), (0,)), ((), ())), preferred_element_type=F32)


def _lower_bound(hgl, layer):
    m = jnp.max(hgl, axis=0, keepdims=True)
    e = jnp.exp(hgl - m)
    p = e / jnp.sum(e, axis=0, keepdims=True)
    cum = p[0:1]
    for j in range(1, layer + 1):
        cum = cum + p[j:j + 1]
    return cum - p[0:1]


def _forget_key(fa, lb):
    e = jnp.exp(-jnp.abs(fa))
    r = 1.0 / (1.0 + e)
    er = e * r
    pos = fa >= 0.0
    sig_p = jnp.where(pos, r, er)
    sig_n = jnp.where(pos, er, r)
    f = jnp.maximum(lb, LB_FLOOR) + (1.0 - lb) * sig_p
    k = (1.0 - lb) * sig_n
    return f, k


def _lru_coeffs(xc, gates, ba, bx, sp):
    tr = jnp.tanh(0.5 * gates[:, :LRU_BS] + 0.5 * ba)
    ti = jnp.tanh(0.5 * gates[:, LRU_BS:] + 0.5 * bx)
    c1 = (-0.5 * LRU_C) * sp
    log_a = c1 * tr + c1
    a = jnp.exp(log_a)
    th = jnp.tanh(log_a)
    one_m_a2 = (-2.0 * th) / (1.0 - th)
    root = one_m_a2 * lax.rsqrt(jnp.maximum(one_m_a2, F32_TINY))
    return a, root * _half_tanh_gate(ti, xc)


def _mod_kernel(c_ref, w_ref, b_ref, o_ref):
    cs = _silu(c_ref[...]).astype(BF16)
    o_ref[0] = _dot(cs, w_ref[0].astype(BF16)) + b_ref[0]


def _mod_call(c_all, w_mod, b_mod):
    n_rows = c_all.shape[0]
    tn = 1536
    return pl.pallas_call(
        _mod_kernel,
        out_shape=jax.ShapeDtypeStruct((DEPTH, n_rows, 6 * D_MODEL), F32),
        grid=(DEPTH, 6 * D_MODEL // tn),
        in_specs=[
            pl.BlockSpec((n_rows, D_MODEL), lambda l, j: (0, 0)),
            pl.BlockSpec((1, D_MODEL, tn), lambda l, j: (l, 0, j)),
            pl.BlockSpec((1, 1, tn), lambda l, j: (l, 0, j)),
        ],
        out_specs=pl.BlockSpec((1, n_rows, tn), lambda l, j: (l, 0, j)),
        compiler_params=pltpu.CompilerParams(
            dimension_semantics=("arbitrary", "arbitrary"), vmem_limit_bytes=VMEM_LIMIT),
        name="mod",
    )(c_all, w_mod, b_mod.reshape(DEPTH, 1, 6 * D_MODEL))


def _hg_tables():
    t = np.arange(HG_CHUNK)
    x = t[:, None] ^ t[None, :]
    hb = np.floor(np.log2(np.maximum(x, 1))).astype(np.int32)
    lvl = np.where(t[:, None] > t[None, :], hb + 1, np.where(t[:, None] == t[None, :], 0, -1)).astype(np.int32)
    return jnp.asarray(lvl)


def _x_within_group(q, k, g2, level):
    n = q.shape[0]
    shp = (n // SUBLANES, SUBLANES, LANES)
    q3, k3, g3 = q.reshape(shp), k.reshape(shp), g2.reshape(shp)
    sub = lax.broadcasted_iota(jnp.int32, (1, SUBLANES, LANES), 1)
    qside = ((sub >> level) & 1) == 1
    if level == 0:
        scaled = jnp.where(qside, q3 * jnp.exp2(g3 - pltpu.roll(g3, 1, axis=1)), k3)
        return scaled.reshape(n, LANES)

    def row(i):
        return jnp.broadcast_to(g3[:, i:i + 1, :], shp)

    piv = jnp.where(sub < 4, row(1), row(5)) if level == 1 else row(3)
    e = jnp.exp2(-jnp.abs(g3 - piv))
    return (jnp.where(qside, q3, k3) * e).reshape(n, LANES)


def _x_across_groups(q, k, g2, level):
    n = q.shape[0]
    b = 1 << level
    blocks = []
    for base in range(0, n, 2 * b):
        lo, mid, hi = base, base + b, base + 2 * b
        gp = g2[mid - 1:mid, :]
        blocks.append(k[lo:mid] * jnp.exp2(gp - g2[lo:mid]))
        blocks.append(q[mid:hi] * jnp.exp2(g2[mid:hi] - gp))
    return jnp.concatenate(blocks, axis=0)


HG_FULL_LEVELS = 4


def _hg_level_products(q, k, g2):
    c = HG_CHUNK
    prods = [_dot_nt(q.astype(BF16), k.astype(BF16))]
    for level in range(HG_LEVELS):
        b = 1 << level
        if b < SUBLANES:
            x = _x_within_group(q, k, g2, level).astype(BF16)
        else:
            x = _x_across_groups(q, k, g2, level).astype(BF16)
        if level < HG_FULL_LEVELS:
            lhs = x
        else:
            lhs = jnp.concatenate([x[base + b:base + 2 * b] for base in range(0, c, 2 * b)], axis=0)
        prods.append(_dot_nt(lhs, x))
    return prods


def _hg_assemble(prods_per_head, lvl_ref):
    c = HG_CHUNK
    n_heads = len(prods_per_head)
    groups = [[] for _ in range(n_heads)]
    for gi in range(c // SUBLANES):
        r0 = gi * SUBLANES
        rows = slice(r0, r0 + SUBLANES)
        lv = lvl_ref[rows, :]
        on_diag = lv == 0
        a_g = [jnp.where(on_diag, p[0][rows], 0.0) for p in prods_per_head]
        for level in range(HG_LEVELS):
            b = 1 << level
            if level < HG_FULL_LEVELS:
                src = r0
            elif r0 & b:
                src = (r0 >> (level + 1)) * b + (r0 & (b - 1))
            else:
                continue
            at_level = lv == level + 1
            a_g = [jnp.where(at_level, p[level + 1][src:src + SUBLANES], a)
                   for p, a in zip(prods_per_head, a_g)]
        for h in range(n_heads):
            groups[h].append(a_g[h])
    return [jnp.concatenate(g, axis=0).astype(BF16) for g in groups]


def _hg_finish(q, k, v, g2, st, a):
    c = HG_CHUNK
    g_last = g2[c - 1:c, :]
    qg = (q * jnp.exp2(g2)).astype(BF16)
    vb = v.astype(BF16)
    o = _dot_nt(qg, st.astype(BF16)) + _dot(a, vb)
    kd = (k * jnp.exp2(g_last - g2)).astype(BF16)
    st_new = st * jnp.exp2(g_last) + _dot_tn(vb, kd)
    return o, st_new


def _causal_conv(xb, prev8, cw_ref, cb_ref):
    t, w = xb.shape
    x3 = xb.reshape(t // SUBLANES, SUBLANES, w)
    sub = lax.broadcasted_iota(jnp.int32, (1, SUBLANES, w), 1)

    def tap(j):
        return cw_ref[j:j + 1, :][None]

    out = cb_ref[...][None] + tap(CONV_W - 1) * x3
    for back in range(1, CONV_W):
        r = pltpu.roll(x3, back, axis=1)
        r_prev_group = jnp.concatenate([pltpu.roll(prev8, back, axis=0)[None], r[:-1]], axis=0)
        shifted = jnp.where(sub >= back, r, r_prev_group)
        out = out + tap(CONV_W - 1 - back) * shifted
    return out.reshape(t, w)


def _chunk_cumsum(y):
    t = y.shape[0]
    y3 = y.reshape(t // SUBLANES, SUBLANES, LANES)
    sub = lax.broadcasted_iota(jnp.int32, (1, SUBLANES, LANES), 1)
    d = 1
    while d < SUBLANES:
        y3 = y3 + jnp.where(sub >= d, pltpu.roll(y3, d, axis=1), 0.0)
        d *= 2
    per_chunk = HG_CHUNK // SUBLANES
    out = []
    for gi in range(t // SUBLANES):
        blk = y3[gi]
        if gi % per_chunk:
            blk = blk + out[-1][SUBLANES - 1:SUBLANES, :]
        out.append(blk)
    return jnp.concatenate(out, axis=0)


def _lru_scan(a, b, h0):
    t = a.shape[0]
    a3 = a.reshape(t // SUBLANES, SUBLANES, LANES)
    b3 = b.reshape(t // SUBLANES, SUBLANES, LANES)
    sub = lax.broadcasted_iota(jnp.int32, a3.shape, 1)
    d = 1
    while d < SUBLANES:
        m = sub >= d
        b3 = a3 * jnp.where(m, pltpu.roll(b3, d, axis=1), 0.0) + b3
        a3 = a3 * jnp.where(m, pltpu.roll(a3, d, axis=1), 1.0)
        d *= 2
    hs = []
    h = h0
    for gidx in range(t // SUBLANES):
        hg = a3[gidx] * h + b3[gidx]
        hs.append(hg)
        h = hg[SUBLANES - 1:SUBLANES, :]
    return jnp.concatenate(hs, axis=0), h


def _prompt_mixer_kernel(layer, n_t,
                         x_ref, mod_ref, hgl_ref, n1g_ref, win_ref, ong_ref, cw_ref, cb_ref, wax_ref,
                         ba_ref, bx_ref, lam_ref, wba_ref, wbb_ref, wout_ref, lvl_ref,
                         xo_ref, hg_ref, lru_ref, conv_ref,
                         q_s, k_s, v_s, g_s, og_s, oa_s, ob_s, prev8_s, st_s, hl_s, xc_s, yb_s, ga_s, gb_s, gr_s,
                         gi_s):
    tl = MIX_TL
    d = D_MODEL
    t_idx = pl.program_id(1)

    @pl.when(t_idx == 0)
    def _():
        st_s[...] = jnp.zeros(st_s.shape, F32)
        hl_s[...] = jnp.zeros(hl_s.shape, F32)
        prev8_s[...] = jnp.zeros(prev8_s.shape, F32)

    x = x_ref[0]
    mod = mod_ref[0]
    sh1, sc1, g1 = mod[:, 0:d], mod[:, d:2 * d], mod[:, 2 * d:3 * d]
    hb = (_rmsnorm(x, n1g_ref[...]) * (1.0 + sc1) + sh1).astype(BF16)

    def proj(j):
        return _dot(hb, win_ref[:, j * d:(j + 1) * d])

    def heads(val, dst):
        for h in range(HG_HEADS):
            dst[h] = val[:, h * LANES:(h + 1) * LANES]

    sp = _softplus(-lam_ref[...])
    ong = ong_ref[...]

    def lru_block(n):
        cols = slice(n * LRU_BS, (n + 1) * LRU_BS)
        gates = jnp.concatenate([gr_s[:, cols], gi_s[:, cols]], axis=1)
        a, bt = _lru_coeffs(xc_s[:, cols], gates, ba_ref[:, cols], bx_ref[:, cols], sp[:, cols])
        hseq, h_last = _lru_scan(a, bt, hl_s[:, cols])
        hl_s[:, cols] = h_last
        ob_s[:, cols] = hseq * yb_s[:, cols]

    def level_products(c):
        rows = slice(c * HG_CHUNK, (c + 1) * HG_CHUNK)
        return [_hg_level_products(q_s[h, rows, :], k_s[h, rows, :], g_s[h, rows, :]) for h in range(HG_HEADS)]

    def finish(c, a_mats):
        rows = slice(c * HG_CHUNK, (c + 1) * HG_CHUNK)
        for h in range(HG_HEADS):
            o, st_new = _hg_finish(q_s[h, rows, :], k_s[h, rows, :], v_s[h, rows, :], g_s[h, rows, :],
                                   st_s[h], a_mats[h])
            st_s[h] = st_new
            oa_s[h, rows, :] = _rmsnorm(o, og_s[h, rows, :])

    xb = proj(4)
    xc = _causal_conv(xb, prev8_s[...], cw_ref, cb_ref)
    xc_s[...] = xc
    prev8_s[...] = xb[tl - SUBLANES:tl, :]
    conv_ref[0] = xb[tl - (CONV_W - 1):tl, :]
    for n in range(LRU_BLOCKS):
        cols = slice(n * LRU_BS, (n + 1) * LRU_BS)
        gates = _dot(xc[:, cols].astype(BF16), wax_ref[n])
        gr_s[:, cols] = gates[:, :LRU_BS]
        gi_s[:, cols] = gates[:, LRU_BS:]
    yb_s[...] = _gelu_tanh(proj(5))
    lb = _lower_bound(hgl_ref[...], layer)
    f, k = _forget_key(proj(1), lb)
    heads(k, k_s)
    log2f = jnp.minimum(jnp.log2(f), 0.0)
    for h in range(HG_HEADS):
        g_s[h] = _chunk_cumsum(log2f[:, h * LANES:(h + 1) * LANES])
    lru_block(0)
    lru_block(1)
    heads(_silu(proj(0)), q_s)
    lru_block(2)
    lru_block(3)
    heads(proj(2), v_s)
    lru_block(4)
    lru_block(5)
    og = _silu(proj(3))
    for h in range(HG_HEADS):
        og_s[h] = og[:, h * LANES:(h + 1) * LANES] * ong
    lru_block(6)
    lru_block(7)
    lru_ref[0] = hl_s[...]
    ga_s[...] = jnp.tanh(0.5 * proj(6))
    gb_s[...] = jnp.tanh(0.5 * proj(7))
    for c in range(tl // HG_CHUNK):
        finish(c, _hg_assemble(level_products(c), lvl_ref))

    o_a = jnp.concatenate([oa_s[h] for h in range(HG_HEADS)], axis=1).astype(BF16)
    merged = (_half_tanh_gate(ga_s[...], _dot(o_a, wba_ref[...]))
              + _half_tanh_gate(gb_s[...], _dot(ob_s[...].astype(BF16), wbb_ref[...])))
    xo_ref[0] = x + g1 * _dot(merged.astype(BF16), wout_ref[...])

    @pl.when(t_idx == n_t - 1)
    def _():
        for h in range(HG_HEADS):
            hg_ref[0, h] = st_s[h].T


def _const_spec(shape, layer=None):
    nd = len(shape)
    if layer is None:
        return pl.BlockSpec(shape, lambda *_: (0,) * nd, pipeline_mode=pl.Buffered(1))
    return pl.BlockSpec((None,) + tuple(shape), lambda *_: (layer,) + (0,) * nd, pipeline_mode=pl.Buffered(1))


def _prompt_mixer_call(layer, x, mod_p, hg_lower, wl, lvl):
    bsz, seq, d = x.shape
    tl = MIX_TL
    n_t = seq // tl
    hshape = (HG_HEADS, tl, LANES)
    out_shapes = (
        jax.ShapeDtypeStruct((bsz, seq, d), F32),
        jax.ShapeDtypeStruct((bsz, HG_HEADS, HG_DK, HG_DV), F32),
        jax.ShapeDtypeStruct((bsz, 1, d), F32),
        jax.ShapeDtypeStruct((bsz, CONV_W - 1, d), F32),
    )
    in_specs = [
        pl.BlockSpec((1, tl, d), lambda b, t: (b, t, 0)),
        pl.BlockSpec((1, 1, 6 * d), lambda b, t: (b, 0, 0)),
        _const_spec((DEPTH, d)),
        _const_spec((1, d), layer),
        _const_spec((d, N_IN), layer),
        _const_spec((1, HG_DV), layer),
        _const_spec((CONV_W, d), layer),
        _const_spec((1, d), layer),
        _const_spec((LRU_BLOCKS, LRU_BS, 2 * LRU_BS), layer),
        _const_spec((1, d), layer),
        _const_spec((1, d), layer),
        _const_spec((1, d), layer),
        _const_spec((d, d), layer),
        _const_spec((d, d), layer),
        _const_spec((d, d), layer),
        _const_spec((HG_CHUNK, HG_CHUNK)),
    ]
    out_specs = (
        pl.BlockSpec((1, tl, d), lambda b, t: (b, t, 0)),
        pl.BlockSpec((1, HG_HEADS, HG_DK, HG_DV), lambda b, t: (b, 0, 0, 0)),
        pl.BlockSpec((1, 1, d), lambda b, t: (b, 0, 0)),
        pl.BlockSpec((1, CONV_W - 1, d), lambda b, t: (b, 0, 0)),
    )
    scratch = [
        pltpu.VMEM(hshape, F32),
        pltpu.VMEM(hshape, F32),
        pltpu.VMEM(hshape, F32),
        pltpu.VMEM(hshape, F32),
        pltpu.VMEM(hshape, F32),
        pltpu.VMEM(hshape, F32),
        pltpu.VMEM((tl, d), F32),
        pltpu.VMEM((SUBLANES, d), F32),
        pltpu.VMEM((HG_HEADS, HG_DV, HG_DK), F32),
        pltpu.VMEM((1, d), F32),
        pltpu.VMEM((tl, d), F32),
        pltpu.VMEM((tl, d), F32),
        pltpu.VMEM((tl, d), F32),
        pltpu.VMEM((tl, d), F32),
        pltpu.VMEM((tl, d), F32),
        pltpu.VMEM((tl, d), F32),
    ]
    return pl.pallas_call(
        functools.partial(_prompt_mixer_kernel, layer, n_t),
        out_shape=out_shapes,
        grid=(bsz, n_t),
        in_specs=in_specs,
        out_specs=out_specs,
        scratch_shapes=scratch,
        compiler_params=pltpu.CompilerParams(
            dimension_semantics=("arbitrary", "arbitrary"), vmem_limit_bytes=VMEM_LIMIT),
        name=f"prompt_mixer_{layer}",
    )(x, mod_p, hg_lower, wl["n1g"], wl["w_in"], wl["ong"], wl["cw"], wl["cb"], wl["wax"],
      wl["ba"], wl["bx"], wl["lam"], wl["wba"], wl["wbb"], wl["wout"], lvl)


def _ffn_kernel(final, x_ref, mod_ref, n2g_ref, w1_ref, w2_ref, fg_ref, o_ref):
    d = D_MODEL
    x = x_ref[0]
    mod = mod_ref[0]
    sh2, sc2, g2 = mod[:, 3 * d:4 * d], mod[:, 4 * d:5 * d], mod[:, 5 * d:6 * d]
    hb = (_rmsnorm(x, n2g_ref[...]) * (1.0 + sc2) + sh2).astype(BF16)
    acc = jnp.zeros(x.shape, F32)
    for c in range(D_FF // d):
        h1 = jnp.maximum(_dot(hb, w1_ref[:, c * d:(c + 1) * d].astype(BF16)), 0.0)
        acc = acc + _dot((h1 * h1).astype(BF16), w2_ref[c * d:(c + 1) * d, :].astype(BF16))
    xn = x + g2 * acc
    if final:
        xn = _rmsnorm(xn, fg_ref[...])
    o_ref[0] = xn


def _ffn_call(layer, x, mod, wl, final_g, tm, name):
    final = layer == DEPTH - 1
    bsz, seq, d = x.shape
    mrows = mod.shape[1]
    mod_spec = (pl.BlockSpec((1, 1, 6 * d), lambda b, t: (b, 0, 0)) if mrows == 1
                else pl.BlockSpec((1, tm, 6 * d), lambda b, t: (b, t, 0)))
    return pl.pallas_call(
        functools.partial(_ffn_kernel, final),
        out_shape=jax.ShapeDtypeStruct((bsz, seq, d), F32),
        grid=(bsz, seq // tm),
        in_specs=[
            pl.BlockSpec((1, tm, d), lambda b, t: (b, t, 0)),
            mod_spec,
            _const_spec((1, d), layer),
            _const_spec((d, D_FF), layer),
            _const_spec((D_FF, d), layer),
            _const_spec((1, d)),
        ],
        out_specs=pl.BlockSpec((1, tm, d), lambda b, t: (b, t, 0)),
        compiler_params=pltpu.CompilerParams(
            dimension_semantics=("arbitrary", "arbitrary"), vmem_limit_bytes=VMEM_LIMIT),
        name=name,
    )(x, mod, wl["n2g"], wl["w1"], wl["w2"], final_g)


def _sample_inproj_kernel(x_ref, mod_ref, n1g_ref, w_ref, o_ref):
    d = D_MODEL
    mod = mod_ref[...]
    hb = (_rmsnorm(x_ref[...], n1g_ref[...]) * (1.0 + mod[:, d:2 * d]) + mod[:, 0:d]).astype(BF16)
    o_ref[...] = _dot(hb, w_ref[...])


def _sample_inproj_call(layer, x, mod_s, wl):
    n, d = x.shape
    return pl.pallas_call(
        _sample_inproj_kernel,
        out_shape=jax.ShapeDtypeStruct((n, N_IN), F32),
        grid=(N_IN // d,),
        in_specs=[
            pl.BlockSpec((n, d), lambda j: (0, 0)),
            pl.BlockSpec((n, 6 * d), lambda j: (0, 0)),
            pl.BlockSpec((None, 1, d), lambda j: (layer, 0, 0)),
            pl.BlockSpec((None, d, d), lambda j: (layer, 0, j)),
        ],
        out_specs=pl.BlockSpec((n, d), lambda j: (0, j)),
        compiler_params=pltpu.CompilerParams(
            dimension_semantics=("arbitrary",), vmem_limit_bytes=VMEM_LIMIT),
        name=f"sample_inproj_{layer}",
    )(x, mod_s, wl["n1g"], wl["w_in"])


N_SAMPLE_MIXER_INPUTS = 20

OUTER_ROWS = 16
OUTER_LEFT_SRC = (0, 0, 1, 0, 2, 1, 3, 4, 5)
OUTER_RIGHT_SRC = (0, 1, 0, 2, 0, 1)


def _outer_tables():
    g = SAMPLE_GROUP
    sel_l = np.zeros((g * OUTER_ROWS, 6 * g), np.float32)
    sel_r = np.zeros((g * OUTER_ROWS, 4 * g), np.float32)
    for j in range(g):
        for r, a in enumerate(OUTER_LEFT_SRC):
            sel_l[OUTER_ROWS * j + r, g * a + j] = 1.0
        for r, a in enumerate(OUTER_RIGHT_SRC):
            sel_r[OUTER_ROWS * j + r, g * a + j] = 1.0
    ones_rows = np.zeros((g, OUTER_ROWS, LANES), np.float32)
    ones_rows[:, len(OUTER_RIGHT_SRC):len(OUTER_LEFT_SRC)] = 1.0
    return (jnp.asarray(sel_l, BF16), jnp.asarray(sel_r, BF16),
            jnp.asarray(ones_rows.reshape(g * OUTER_ROWS, LANES), BF16))


def _split3(x):
    x1 = x.astype(BF16).astype(F32)
    r = x - x1
    x2 = r.astype(BF16).astype(F32)
    x3 = (r - x2).astype(BF16).astype(F32)
    return [x1, x2, x3]


def _sample_mixer_kernel(layer, n_groups, *refs):
    refs = refs[:N_SAMPLE_MIXER_INPUTS] + refs[len(refs) - 10:]
    (u_ref, x_ref, mod_ref, hg_ref, lru_ref, conv_ref, hgl_ref, ong_ref, cw_ref, cb_ref,
     wax_ref, ba_ref, bx_ref, lam_ref, wba_ref, wbb_ref, wout_ref, sell_ref, selr_ref, ones_ref,
     xo_ref, hgo_ref, lruo_ref, convo_ref,
     f_s, k_s, q_s, v_s, o_s, ob_s) = refs
    d = D_MODEL
    n = x_ref.shape[0]
    i = pl.program_id(0)
    if layer == 0:
        for other in range(1, DEPTH):
            hgo_ref[other] = jnp.zeros(hgo_ref.shape[1:], F32)
        hgo_ref = hgo_ref.at[0]

    @pl.when(i == 0)
    def _():
        q_s[...] = _silu(u_ref[:, 0:d])
        lb = _lower_bound(hgl_ref[...], layer)
        f, k = _forget_key(u_ref[:, d:2 * d], lb)
        f_s[...] = f
        k_s[...] = k
        v_s[...] = u_ref[:, 2 * d:3 * d]
        xb = u_ref[:, 4 * d:5 * d]
        cw = cw_ref[...]
        xc = cb_ref[...] + cw[3:4] * xb
        for j in range(CONV_W - 1):
            xc = xc + cw[j:j + 1] * conv_ref[j]
        convo_ref[0] = conv_ref[1]
        convo_ref[1] = conv_ref[2]
        convo_ref[2] = xb
        yb = _gelu_tanh(u_ref[:, 5 * d:6 * d])
        sp = _softplus(-lam_ref[...])
        for blk in range(LRU_BLOCKS):
            cols = slice(blk * LRU_BS, (blk + 1) * LRU_BS)
            xcn = xc[:, cols]
            gates = _dot(xcn.astype(BF16), wax_ref[blk])
            a, bt = _lru_coeffs(xcn, gates, ba_ref[:, cols], bx_ref[:, cols], sp[:, cols])
            hn = a * lru_ref[:, cols] + bt
            lruo_ref[:, cols] = hn
            ob_s[:, cols] = hn * yb[:, cols]

    rows = pl.ds(pl.multiple_of(i * SAMPLE_GROUP, SAMPLE_GROUP), SAMPLE_GROUP)
    ones_rows = ones_ref[...]

    def outer_products(h):
        cols = slice(h * LANES, (h + 1) * LANES)
        vparts = _split3(v_s[rows, cols])
        left = jnp.concatenate(_split3(k_s[rows, cols]) + _split3(f_s[rows, cols]), axis=0).astype(BF16)
        right = jnp.concatenate(vparts + [jnp.zeros_like(vparts[0])], axis=0).astype(BF16)
        left_t = _dot(sell_ref[...], left).T.astype(BF16)
        right_rows = jnp.concatenate([_dot(selr_ref[...], right).astype(BF16), ones_rows], axis=1)
        out = []
        for j in range(SAMPLE_GROUP):
            lo, hi = j * OUTER_ROWS, (j + 1) * OUTER_ROWS
            pieces = [right_rows[lo:hi]]
            if lo:
                pieces.insert(0, jnp.zeros((lo, 2 * HG_DV), BF16))
            if hi < SAMPLE_GROUP * OUTER_ROWS:
                pieces.append(jnp.zeros((SAMPLE_GROUP * OUTER_ROWS - hi, 2 * HG_DV), BF16))
            out.append(_dot(left_t, jnp.concatenate(pieces, axis=0)))
        return out

    kv_f = outer_products(0)
    for h in range(HG_HEADS):
        cols = slice(h * LANES, (h + 1) * LANES)
        kv_f_next = outer_products(h + 1) if h + 1 < HG_HEADS else None
        s_new = [hg_ref[j, h] * kv_f[j][:, HG_DV:] + kv_f[j][:, :HG_DV] for j in range(SAMPLE_GROUP)]
        for j in range(SAMPLE_GROUP):
            hgo_ref[j, h] = s_new[j]
        qg = q_s[rows, cols].astype(BF16)
        o_rows = [_dot(qg, s_new[j].astype(BF16))[j:j + 1, :] for j in range(SAMPLE_GROUP)]
        o_s[rows, cols] = jnp.concatenate(o_rows, axis=0)
        kv_f = kv_f_next

    @pl.when(i == n_groups - 1)
    def _():
        ong = ong_ref[...]
        parts = []
        for h in range(HG_HEADS):
            cols = slice(h * LANES, (h + 1) * LANES)
            parts.append(_rmsnorm(o_s[:, cols], ong) * _silu(u_ref[:, 3 * d + h * LANES:3 * d + (h + 1) * LANES]))
        o_a = jnp.concatenate(parts, axis=1).astype(BF16)
        merged = (_sigmoid(u_ref[:, 6 * d:7 * d]) * _dot(o_a, wba_ref[...])
                  + _sigmoid(u_ref[:, 7 * d:8 * d]) * _dot(ob_s[...].astype(BF16), wbb_ref[...]))
        g1 = mod_ref[:, 2 * d:3 * d]
        xo_ref[...] = x_ref[...] + g1 * _dot(merged.astype(BF16), wout_ref[...])


def _sample_mixer_call(layer, u, x, mod_s, st_hg_all, st_lru_all, st_conv_t, hg_lower, wl, hg_out_prev):
    n, d = x.shape
    grp = SAMPLE_GROUP
    n_groups = n // grp

    def full(shape):
        nd = len(shape)
        return pl.BlockSpec(shape, lambda i: (0,) * nd)

    def of_layer(shape):
        nd = len(shape)
        return pl.BlockSpec((None,) + tuple(shape), lambda i: (layer,) + (0,) * nd)

    st_spec = pl.BlockSpec((None, grp, HG_HEADS, HG_DK, HG_DV), lambda i: (layer, i, 0, 0, 0))
    out_shapes = (
        jax.ShapeDtypeStruct((n, d), F32),
        jax.ShapeDtypeStruct((DEPTH, n, HG_HEADS, HG_DK, HG_DV), F32),
        jax.ShapeDtypeStruct((n, d), F32),
        jax.ShapeDtypeStruct((CONV_W - 1, n, d), F32),
    )
    in_specs = [
        full((n, N_IN)), full((n, d)), full((n, 6 * d)), st_spec,
        of_layer((n, d)), full((CONV_W - 1, n, d)),
        full((DEPTH, d)), of_layer((1, HG_DV)), of_layer((CONV_W, d)), of_layer((1, d)),
        of_layer((LRU_BLOCKS, LRU_BS, 2 * LRU_BS)), of_layer((1, d)), of_layer((1, d)), of_layer((1, d)),
        of_layer((d, d)), of_layer((d, d)), of_layer((d, d)),
    ]
    outer_tables = _outer_tables()
    in_specs += [full(t.shape) for t in outer_tables]
    assert len(in_specs) == N_SAMPLE_MIXER_INPUTS
    args = [u, x, mod_s, st_hg_all, st_lru_all, st_conv_t, hg_lower, wl["ong"], wl["cw"], wl["cb"], wl["wax"],
            wl["ba"], wl["bx"], wl["lam"], wl["wba"], wl["wbb"], wl["wout"], *outer_tables]
    aliases = {}
    if hg_out_prev is not None:
        in_specs.append(pl.BlockSpec(memory_space=pl.ANY))
        args.append(hg_out_prev)
        aliases = {N_SAMPLE_MIXER_INPUTS: 1}
    st_out_spec = st_spec if layer else pl.BlockSpec((DEPTH, grp, HG_HEADS, HG_DK, HG_DV),
                                                     lambda i: (0, i, 0, 0, 0))
    out_specs = (full((n, d)), st_out_spec, full((n, d)), full((CONV_W - 1, n, d)))
    scratch = [
        pltpu.VMEM((n, d), F32),
        pltpu.VMEM((n, d), F32),
        pltpu.VMEM((n, d), F32),
        pltpu.VMEM((n, d), F32),
        pltpu.VMEM((n, d), F32),
        pltpu.VMEM((n, d), F32),
    ]
    return pl.pallas_call(
        functools.partial(_sample_mixer_kernel, layer, n_groups),
        out_shape=out_shapes,
        grid=(n_groups,),
        in_specs=in_specs,
        out_specs=out_specs,
        scratch_shapes=scratch,
        input_output_aliases=aliases,
        compiler_params=pltpu.CompilerParams(
            dimension_semantics=("arbitrary",), vmem_limit_bytes=VMEM_LIMIT),
        name=f"sample_mixer_{layer}",
    )(*args)


def _stacked_weights(norm1_g, norm2_g, w_in, hg_onorm_g, lru_conv_w, lru_conv_b, lru_wa, lru_ba, lru_wx,
                     lru_bx, lru_lambda, w_branch_a, w_branch_b, w_out, w_ff1, w_ff2):
    d = D_MODEL
    return {
        "n1g": norm1_g.reshape(DEPTH, 1, d),
        "n2g": norm2_g.reshape(DEPTH, 1, d),
        "w_in": w_in.astype(BF16),
        "ong": hg_onorm_g.reshape(DEPTH, 1, HG_DV),
        "cw": lru_conv_w,
        "cb": lru_conv_b.reshape(DEPTH, 1, d),
        "wax": jnp.concatenate([lru_wa, lru_wx], axis=-1).astype(BF16),
        "ba": lru_ba.reshape(DEPTH, 1, d),
        "bx": lru_bx.reshape(DEPTH, 1, d),
        "lam": lru_lambda.reshape(DEPTH, 1, d),
        "wba": w_branch_a.astype(BF16),
        "wbb": w_branch_b.astype(BF16),
        "wout": w_out.astype(BF16),
        "w1": w_ff1,
        "w2": w_ff2,
    }


def kernel(x_prompt, x_sample, state_hgrn, state_rglru, state_conv, c_prompt, c_sample, w_mod, b_mod, norm1_g, norm2_g, w_in, hg_lower, hg_onorm_g, lru_conv_w, lru_conv_b, lru_wa, lru_ba, lru_wx, lru_bx, lru_lambda, w_branch_a, w_branch_b, w_out, w_ff1, w_ff2, final_norm_g):
    bsz, seq, d = x_prompt.shape
    n_s = x_sample.shape[0]
    assert d == D_MODEL and x_sample.shape[1] == 1 and seq % MIX_TL == 0 and seq % FFN_TM == 0
    assert n_s % SAMPLE_GROUP == 0 and n_s == LANES

    mod = _mod_call(jnp.concatenate([c_prompt, c_sample], axis=0), w_mod, b_mod)
    lvl = _hg_tables()
    final_g = final_norm_g.reshape(1, d)

    xp = x_prompt
    xs = x_sample.reshape(n_s, d)
    hg_p, lru_p, conv_p, lru_s, conv_s = [], [], [], [], []
    hg_s = None
    wl = _stacked_weights(norm1_g, norm2_g, w_in, hg_onorm_g, lru_conv_w, lru_conv_b, lru_wa, lru_ba,
                          lru_wx, lru_bx, lru_lambda, w_branch_a, w_branch_b, w_out, w_ff1, w_ff2)
    for l in range(DEPTH):
        mod_p = mod[l, :bsz].reshape(bsz, 1, 6 * d)
        mod_s = mod[l, bsz:]

        xp, hg, lru, conv = _prompt_mixer_call(l, xp, mod_p, hg_lower, wl, lvl)
        xp = _ffn_call(l, xp, mod_p, wl, final_g, FFN_TM, f"prompt_ffn_{l}")
        hg_p.append(hg)
        lru_p.append(lru.reshape(bsz, d))
        conv_p.append(conv)

        u = _sample_inproj_call(l, xs, mod_s, wl)
        xs, hg_s, lru, conv = _sample_mixer_call(l, u, xs, mod_s, state_hgrn, state_rglru,
                                                 jnp.swapaxes(state_conv[l], 0, 1), hg_lower, wl, hg_s)
        xs = _ffn_call(l, xs.reshape(1, n_s, d), mod_s.reshape(1, n_s, 6 * d), wl, final_g, n_s,
                       f"sample_ffn_{l}").reshape(n_s, d)
        lru_s.append(lru)
        conv_s.append(jnp.swapaxes(conv, 0, 1))

    return (xp, xs.reshape(n_s, 1, d),
            jnp.stack(hg_p), jnp.stack(lru_p), jnp.stack(conv_p),
            hg_s, jnp.stack(lru_s), jnp.stack(conv_s))
```

```python
import functools

import numpy as np
import jax
import jax.numpy as jnp
from jax import lax
from jax.experimental import pallas as pl
from jax.experimental.pallas import tpu as pltpu

F32 = jnp.float32
BF16 = jnp.bfloat16

D_MODEL = 1024
DEPTH = 2
HG_HEADS = 8
HG_DK = 128
HG_DV = 128
LB_FLOOR = 1e-30
LRU_BLOCKS = 8
LRU_BS = 128
LRU_C = 8.0
CONV_W = 4
D_FF = 4 * D_MODEL
EPS = 1e-6
N_IN = 8 * D_MODEL
F32_TINY = float(np.finfo(np.float32).tiny)

LANES = 128
SUBLANES = 8
HG_CHUNK = 128
HG_LEVELS = 7
MIX_TL = 256
FFN_TM = 512
SAMPLE_GROUP = 8
VMEM_LIMIT = 56 * 1024 * 1024


def _sigmoid(x):
    return 0.5 * jnp.tanh(0.5 * x) + 0.5


def _silu(x):
    h = 0.5 * x
    return h * jnp.tanh(h) + h


def _gelu_tanh(x):
    c = 0.7978845608028654
    h = 0.5 * x
    return h * jnp.tanh(x * (c + (c * 0.044715) * (x * x))) + h


def _half_tanh_gate(t, y):
    h = 0.5 * y
    return h * t + h


def _log1p(x):
    u = 1.0 + x
    return jnp.where(u == 1.0, x, jnp.log(u) * (x / (u - 1.0)))


def _softplus(z):
    return jnp.maximum(z, 0.0) + _log1p(jnp.exp(-jnp.abs(z)))


def _rmsnorm(x, g):
    ms = jnp.mean(x * x, axis=-1, keepdims=True)
    return x * lax.rsqrt(ms + EPS) * g


def _dot(a, b):
    return jnp.dot(a, b, preferred_element_type=F32)


def _dot_nt(a, b):
    return lax.dot_general(a, b, (((1,), (1,)), ((), ())), preferred_element_type=F32)


def _dot_tn(a, b):
    return lax.dot_general(a, b, (((0,), (0,)), ((), ())), preferred_element_type=F32)


def _lower_bound(hgl, layer):
    m = jnp.max(hgl, axis=0, keepdims=True)
    e = jnp.exp(hgl - m)
    p = e / jnp.sum(e, axis=0, keepdims=True)
    cum = p[0:1]
    for j in range(1, layer + 1):
        cum = cum + p[j:j + 1]
    return cum - p[0:1]


def _forget_key(fa, lb):
    e = jnp.exp(-jnp.abs(fa))
    r = 1.0 / (1.0 + e)
    er = e * r
    pos = fa >= 0.0
    sig_p = jnp.where(pos, r, er)
    sig_n = jnp.where(pos, er, r)
    f = jnp.maximum(lb, LB_FLOOR) + (1.0 - lb) * sig_p
    k = (1.0 - lb) * sig_n
    return f, k


def _lru_coeffs(xc, gates, ba, bx, sp):
    tr = jnp.tanh(0.5 * gates[:, :LRU_BS] + 0.5 * ba)
    ti = jnp.tanh(0.5 * gates[:, LRU_BS:] + 0.5 * bx)
    c1 = (-0.5 * LRU_C) * sp
    log_a = c1 * tr + c1
    a = jnp.exp(log_a)
    th = jnp.tanh(log_a)
    one_m_a2 = (-2.0 * th) / (1.0 - th)
    root = one_m_a2 * lax.rsqrt(jnp.maximum(one_m_a2, F32_TINY))
    return a, root * _half_tanh_gate(ti, xc)


def _mod_kernel(cp_ref, cs_ref, w_ref, b_ref, op_ref, os_ref):
    w = w_ref[0].astype(BF16)
    op_ref[0] = _dot(_silu(cp_ref[...]).astype(BF16), w) + b_ref[0]
    os_ref[0] = _dot(_silu(cs_ref[...]).astype(BF16), w) + b_ref[0]


def _mod_call(c_prompt, c_sample, w_mod, b_mod):
    n_p, n_s = c_prompt.shape[0], c_sample.shape[0]
    tn = 1536
    return pl.pallas_call(
        _mod_kernel,
        out_shape=(jax.ShapeDtypeStruct((DEPTH, n_p, 6 * D_MODEL), F32),
                   jax.ShapeDtypeStruct((DEPTH, n_s, 6 * D_MODEL), F32)),
        grid=(DEPTH, 6 * D_MODEL // tn),
        in_specs=[
            pl.BlockSpec((n_p, D_MODEL), lambda l, j: (0, 0)),
            pl.BlockSpec((n_s, D_MODEL), lambda l, j: (0, 0)),
            pl.BlockSpec((1, D_MODEL, tn), lambda l, j: (l, 0, j)),
            pl.BlockSpec((1, 1, tn), lambda l, j: (l, 0, j)),
        ],
        out_specs=(pl.BlockSpec((1, n_p, tn), lambda l, j: (l, 0, j)),
                   pl.BlockSpec((1, n_s, tn), lambda l, j: (l, 0, j))),
        compiler_params=pltpu.CompilerParams(
            dimension_semantics=("arbitrary", "arbitrary"), vmem_limit_bytes=VMEM_LIMIT),
        name="mod",
    )(c_prompt, c_sample, w_mod, b_mod.reshape(DEPTH, 1, 6 * D_MODEL))


def _hg_tables():
    t = np.arange(HG_CHUNK)
    x = t[:, None] ^ t[None, :]
    hb = np.floor(np.log2(np.maximum(x, 1))).astype(np.int32)
    lvl = np.where(t[:, None] > t[None, :], hb + 1, np.where(t[:, None] == t[None, :], 0, -1)).astype(np.int32)
    return jnp.asarray(lvl)


def _x_within_group(q, k, g2, level):
    n = q.shape[0]
    shp = (n // SUBLANES, SUBLANES, LANES)
    q3, k3, g3 = q.reshape(shp), k.reshape(shp), g2.reshape(shp)
    sub = lax.broadcasted_iota(jnp.int32, (1, SUBLANES, LANES), 1)
    qside = ((sub >> level) & 1) == 1
    if level == 0:
        scaled = jnp.where(qside, q3 * jnp.exp2(g3 - pltpu.roll(g3, 1, axis=1)), k3)
        return scaled.reshape(n, LANES)

    def row(i):
        return jnp.broadcast_to(g3[:, i:i + 1, :], shp)

    piv = jnp.where(sub < 4, row(1), row(5)) if level == 1 else row(3)
    e = jnp.exp2(-jnp.abs(g3 - piv))
    return (jnp.where(qside, q3, k3) * e).reshape(n, LANES)


def _x_across_groups(q, k, g2, level):
    n = q.shape[0]
    b = 1 << level
    blocks = []
    for base in range(0, n, 2 * b):
        lo, mid, hi = base, base + b, base + 2 * b
        gp = g2[mid - 1:mid, :]
        blocks.append(k[lo:mid] * jnp.exp2(gp - g2[lo:mid]))
        blocks.append(q[mid:hi] * jnp.exp2(g2[mid:hi] - gp))
    return jnp.concatenate(blocks, axis=0)


HG_FULL_LEVELS = 4


def _hg_level_products(q, k, g2):
    c = HG_CHUNK
    prods = [_dot_nt(q.astype(BF16), k.astype(BF16))]
    for level in range(HG_LEVELS):
        b = 1 << level
        if b < SUBLANES:
            x = _x_within_group(q, k, g2, level).astype(BF16)
        else:
            x = _x_across_groups(q, k, g2, level).astype(BF16)
        if level < HG_FULL_LEVELS:
            lhs = x
        else:
            lhs = jnp.concatenate([x[base + b:base + 2 * b] for base in range(0, c, 2 * b)], axis=0)
        prods.append(_dot_nt(lhs, x))
    return prods


def _hg_assemble(prods_per_head, lvl_ref):
    c = HG_CHUNK
    n_heads = len(prods_per_head)
    groups = [[] for _ in range(n_heads)]
    for gi in range(c // SUBLANES):
        r0 = gi * SUBLANES
        rows = slice(r0, r0 + SUBLANES)
        lv = lvl_ref[rows, :]
        on_diag = lv == 0
        a_g = [jnp.where(on_diag, p[0][rows], 0.0) for p in prods_per_head]
        for level in range(HG_LEVELS):
            b = 1 << level
            if level < HG_FULL_LEVELS:
                src = r0
            elif r0 & b:
                src = (r0 >> (level + 1)) * b + (r0 & (b - 1))
            else:
                continue
            at_level = lv == level + 1
            a_g = [jnp.where(at_level, p[level + 1][src:src + SUBLANES], a)
                   for p, a in zip(prods_per_head, a_g)]
        for h in range(n_heads):
            groups[h].append(a_g[h])
    return [jnp.concatenate(g, axis=0).astype(BF16) for g in groups]


def _hg_finish(q, k, v, g2, st, a):
    c = HG_CHUNK
    g_last = g2[c - 1:c, :]
    qg = (q * jnp.exp2(g2)).astype(BF16)
    vb = v.astype(BF16)
    o = _dot_nt(qg, st.astype(BF16)) + _dot(a, vb)
    kd = (k * jnp.exp2(g_last - g2)).astype(BF16)
    st_new = st * jnp.exp2(g_last) + _dot_tn(vb, kd)
    return o, st_new


def _causal_conv(xb, prev8, cw_ref, cb_ref):
    t, w = xb.shape
    x3 = xb.reshape(t // SUBLANES, SUBLANES, w)
    sub = lax.broadcasted_iota(jnp.int32, (1, SUBLANES, w), 1)

    def tap(j):
        return cw_ref[j:j + 1, :][None]

    out = cb_ref[...][None] + tap(CONV_W - 1) * x3
    for back in range(1, CONV_W):
        r = pltpu.roll(x3, back, axis=1)
        r_prev_group = jnp.concatenate([pltpu.roll(prev8, back, axis=0)[None], r[:-1]], axis=0)
        shifted = jnp.where(sub >= back, r, r_prev_group)
        out = out + tap(CONV_W - 1 - back) * shifted
    return out.reshape(t, w)


def _chunk_cumsum(y):
    t = y.shape[0]
    y3 = y.reshape(t // SUBLANES, SUBLANES, LANES)
    sub = lax.broadcasted_iota(jnp.int32, (1, SUBLANES, LANES), 1)
    d = 1
    while d < SUBLANES:
        y3 = y3 + jnp.where(sub >= d, pltpu.roll(y3, d, axis=1), 0.0)
        d *= 2
    per_chunk = HG_CHUNK // SUBLANES
    out = []
    for gi in range(t // SUBLANES):
        blk = y3[gi]
        if gi % per_chunk:
            blk = blk + out[-1][SUBLANES - 1:SUBLANES, :]
        out.append(blk)
    return jnp.concatenate(out, axis=0)


def _lru_scan(a, b, h0):
    t = a.shape[0]
    a3 = a.reshape(t // SUBLANES, SUBLANES, LANES)
    b3 = b.reshape(t // SUBLANES, SUBLANES, LANES)
    sub = lax.broadcasted_iota(jnp.int32, a3.shape, 1)
    d = 1
    while d < SUBLANES:
        m = sub >= d
        b3 = a3 * jnp.where(m, pltpu.roll(b3, d, axis=1), 0.0) + b3
        a3 = a3 * jnp.where(m, pltpu.roll(a3, d, axis=1), 1.0)
        d *= 2
    hs = []
    h = h0
    for gidx in range(t // SUBLANES):
        hg = a3[gidx] * h + b3[gidx]
        hs.append(hg)
        h = hg[SUBLANES - 1:SUBLANES, :]
    return jnp.concatenate(hs, axis=0), h


def _prompt_mixer_kernel(layer, n_t,
                         x_ref, mod_ref, hgl_ref, n1g_ref, win_ref, ong_ref, cw_ref, cb_ref, wax_ref,
                         ba_ref, bx_ref, lam_ref, wba_ref, wbb_ref, wout_ref, lvl_ref,
                         xo_ref, hg_ref, lru_ref, conv_ref,
                         q_s, k_s, v_s, g_s, og_s, oa_s, ob_s, prev8_s, st_s, hl_s, xc_s, yb_s, ga_s, gb_s, gr_s,
                         gi_s):
    tl = MIX_TL
    d = D_MODEL
    t_idx = pl.program_id(1)

    @pl.when(t_idx == 0)
    def _():
        st_s[...] = jnp.zeros(st_s.shape, F32)
        hl_s[...] = jnp.zeros(hl_s.shape, F32)
        prev8_s[...] = jnp.zeros(prev8_s.shape, F32)

    x = x_ref[0]
    mod = mod_ref[0]
    sh1, sc1, g1 = mod[:, 0:d], mod[:, d:2 * d], mod[:, 2 * d:3 * d]
    hb = (_rmsnorm(x, n1g_ref[...]) * (1.0 + sc1) + sh1).astype(BF16)

    def proj(j):
        return _dot(hb, win_ref[:, j * d:(j + 1) * d])

    def heads(val, dst):
        for h in range(HG_HEADS):
            dst[h] = val[:, h * LANES:(h + 1) * LANES]

    sp = _softplus(-lam_ref[...])
    ong = ong_ref[...]

    def lru_block(n):
        cols = slice(n * LRU_BS, (n + 1) * LRU_BS)
        gates = jnp.concatenate([gr_s[:, cols], gi_s[:, cols]], axis=1)
        a, bt = _lru_coeffs(xc_s[:, cols], gates, ba_ref[:, cols], bx_ref[:, cols], sp[:, cols])
        hseq, h_last = _lru_scan(a, bt, hl_s[:, cols])
        hl_s[:, cols] = h_last
        ob_s[:, cols] = hseq * yb_s[:, cols]

    def level_products(c):
        rows = slice(c * HG_CHUNK, (c + 1) * HG_CHUNK)
        return [_hg_level_products(q_s[h, rows, :], k_s[h, rows, :], g_s[h, rows, :]) for h in range(HG_HEADS)]

    def finish(c, a_mats):
        rows = slice(c * HG_CHUNK, (c + 1) * HG_CHUNK)
        for h in range(HG_HEADS):
            o, st_new = _hg_finish(q_s[h, rows, :], k_s[h, rows, :], v_s[h, rows, :], g_s[h, rows, :],
                                   st_s[h], a_mats[h])
            st_s[h] = st_new
            oa_s[h, rows, :] = _rmsnorm(o, og_s[h, rows, :])

    xb = proj(4)
    xc = _causal_conv(xb, prev8_s[...], cw_ref, cb_ref)
    xc_s[...] = xc
    prev8_s[...] = xb[tl - SUBLANES:tl, :]
    conv_ref[0] = xb[tl - (CONV_W - 1):tl, :]
    for n in range(LRU_BLOCKS):
        cols = slice(n * LRU_BS, (n + 1) * LRU_BS)
        gates = _dot(xc[:, cols].astype(BF16), wax_ref[n])
        gr_s[:, cols] = gates[:, :LRU_BS]
        gi_s[:, cols] = gates[:, LRU_BS:]
    yb_s[...] = _gelu_tanh(proj(5))
    lb = _lower_bound(hgl_ref[...], layer)
    f, k = _forget_key(proj(1), lb)
    heads(k, k_s)
    log2f = jnp.minimum(jnp.log2(f), 0.0)
    for h in range(HG_HEADS):
        g_s[h] = _chunk_cumsum(log2f[:, h * LANES:(h + 1) * LANES])
    lru_block(0)
    lru_block(1)
    heads(_silu(proj(0)), q_s)
    lru_block(2)
    lru_block(3)
    heads(proj(2), v_s)
    lru_block(4)
    lru_block(5)
    og = _silu(proj(3))
    for h in range(HG_HEADS):
        og_s[h] = og[:, h * LANES:(h + 1) * LANES] * ong
    lru_block(6)
    lru_block(7)
    lru_ref[0] = hl_s[...]
    ga_s[...] = jnp.tanh(0.5 * proj(6))
    gb_s[...] = jnp.tanh(0.5 * proj(7))
    for c in range(tl // HG_CHUNK):
        finish(c, _hg_assemble(level_products(c), lvl_ref))

    o_a = jnp.concatenate([oa_s[h] for h in range(HG_HEADS)], axis=1).astype(BF16)
    merged = (_half_tanh_gate(ga_s[...], _dot(o_a, wba_ref[...]))
              + _half_tanh_gate(gb_s[...], _dot(ob_s[...].astype(BF16), wbb_ref[...])))
    xo_ref[0] = x + g1 * _dot(merged.astype(BF16), wout_ref[...])

    @pl.when(t_idx == n_t - 1)
    def _():
        for h in range(HG_HEADS):
            hg_ref[0, h] = st_s[h].T


def _const_spec(shape, layer=None):
    nd = len(shape)
    if layer is None:
        return pl.BlockSpec(shape, lambda *_: (0,) * nd, pipeline_mode=pl.Buffered(1))
    return pl.BlockSpec((None,) + tuple(shape), lambda *_: (layer,) + (0,) * nd, pipeline_mode=pl.Buffered(1))


def _prompt_mixer_call(layer, x, mod_p, hg_lower, wl, lvl):
    bsz, seq, d = x.shape
    tl = MIX_TL
    n_t = seq // tl
    hshape = (HG_HEADS, tl, LANES)
    out_shapes = (
        jax.ShapeDtypeStruct((bsz, seq, d), F32),
        jax.ShapeDtypeStruct((bsz, HG_HEADS, HG_DK, HG_DV), F32),
        jax.ShapeDtypeStruct((bsz, 1, d), F32),
        jax.ShapeDtypeStruct((bsz, CONV_W - 1, d), F32),
    )
    in_specs = [
        pl.BlockSpec((1, tl, d), lambda b, t: (b, t, 0)),
        pl.BlockSpec((1, 1, 6 * d), lambda b, t: (b, 0, 0)),
        _const_spec((DEPTH, d)),
        _const_spec((1, d), layer),
        _const_spec((d, N_IN), layer),
        _const_spec((1, HG_DV), layer),
        _const_spec((CONV_W, d), layer),
        _const_spec((1, d), layer),
        _const_spec((LRU_BLOCKS, LRU_BS, 2 * LRU_BS), layer),
        _const_spec((1, d), layer),
        _const_spec((1, d), layer),
        _const_spec((1, d), layer),
        _const_spec((d, d), layer),
        _const_spec((d, d), layer),
        _const_spec((d, d), layer),
        _const_spec((HG_CHUNK, HG_CHUNK)),
    ]
    out_specs = (
        pl.BlockSpec((1, tl, d), lambda b, t: (b, t, 0)),
        pl.BlockSpec((1, HG_HEADS, HG_DK, HG_DV), lambda b, t: (b, 0, 0, 0)),
        pl.BlockSpec((1, 1, d), lambda b, t: (b, 0, 0)),
        pl.BlockSpec((1, CONV_W - 1, d), lambda b, t: (b, 0, 0)),
    )
    scratch = [
        pltpu.VMEM(hshape, F32),
        pltpu.VMEM(hshape, F32),
        pltpu.VMEM(hshape, F32),
        pltpu.VMEM(hshape, F32),
        pltpu.VMEM(hshape, F32),
        pltpu.VMEM(hshape, F32),
        pltpu.VMEM((tl, d), F32),
        pltpu.VMEM((SUBLANES, d), F32),
        pltpu.VMEM((HG_HEADS, HG_DV, HG_DK), F32),
        pltpu.VMEM((1, d), F32),
        pltpu.VMEM((tl, d), F32),
        pltpu.VMEM((tl, d), F32),
        pltpu.VMEM((tl, d), F32),
        pltpu.VMEM((tl, d), F32),
        pltpu.VMEM((tl, d), F32),
        pltpu.VMEM((tl, d), F32),
    ]
    return pl.pallas_call(
        functools.partial(_prompt_mixer_kernel, layer, n_t),
        out_shape=out_shapes,
        grid=(bsz, n_t),
        in_specs=in_specs,
        out_specs=out_specs,
        scratch_shapes=scratch,
        compiler_params=pltpu.CompilerParams(
            dimension_semantics=("arbitrary", "arbitrary"), vmem_limit_bytes=VMEM_LIMIT),
        name=f"prompt_mixer_{layer}",
    )(x, mod_p, hg_lower, wl["n1g"], wl["w_in"], wl["ong"], wl["cw"], wl["cb"], wl["wax"],
      wl["ba"], wl["bx"], wl["lam"], wl["wba"], wl["wbb"], wl["wout"], lvl)


def _ffn_kernel(final, x_ref, mod_ref, n2g_ref, w1_ref, w2_ref, fg_ref, o_ref):
    d = D_MODEL
    x = x_ref[0]
    mod = mod_ref[0]
    sh2, sc2, g2 = mod[:, 3 * d:4 * d], mod[:, 4 * d:5 * d], mod[:, 5 * d:6 * d]
    hb = (_rmsnorm(x, n2g_ref[...]) * (1.0 + sc2) + sh2).astype(BF16)
    acc = jnp.zeros(x.shape, F32)
    for c in range(D_FF // d):
        h1 = jnp.maximum(_dot(hb, w1_ref[:, c * d:(c + 1) * d].astype(BF16)), 0.0)
        acc = acc + _dot((h1 * h1).astype(BF16), w2_ref[c * d:(c + 1) * d, :].astype(BF16))
    xn = x + g2 * acc
    if final:
        xn = _rmsnorm(xn, fg_ref[...])
    o_ref[0] = xn


def _ffn_call(layer, x, mod, wl, final_g, tm, name, per_token_mod=False):
    final = layer == DEPTH - 1
    bsz, seq, d = x.shape
    if per_token_mod:
        assert bsz == 1 and mod.shape == (DEPTH, seq, 6 * d)
        mod_spec = pl.BlockSpec((1, tm, 6 * d), lambda b, t: (layer, t, 0))
    else:
        assert mod.shape == (bsz, 1, 6 * d)
        mod_spec = pl.BlockSpec((1, 1, 6 * d), lambda b, t: (b, 0, 0))
    return pl.pallas_call(
        functools.partial(_ffn_kernel, final),
        out_shape=jax.ShapeDtypeStruct((bsz, seq, d), F32),
        grid=(bsz, seq // tm),
        in_specs=[
            pl.BlockSpec((1, tm, d), lambda b, t: (b, t, 0)),
            mod_spec,
            _const_spec((1, d), layer),
            _const_spec((d, D_FF), layer),
            _const_spec((D_FF, d), layer),
            _const_spec((1, d)),
        ],
        out_specs=pl.BlockSpec((1, tm, d), lambda b, t: (b, t, 0)),
        compiler_params=pltpu.CompilerParams(
            dimension_semantics=("arbitrary", "arbitrary"), vmem_limit_bytes=VMEM_LIMIT),
        name=name,
    )(x, mod, wl["n2g"], wl["w1"], wl["w2"], final_g)


def _sample_inproj_kernel(x_ref, mod_ref, n1g_ref, w_ref, o_ref):
    d = D_MODEL
    mod = mod_ref[...]
    hb = (_rmsnorm(x_ref[...], n1g_ref[...]) * (1.0 + mod[:, d:2 * d]) + mod[:, 0:d]).astype(BF16)
    o_ref[...] = _dot(hb, w_ref[...])


def _sample_inproj_call(layer, x, mod_s, wl):
    n, d = x.shape
    return pl.pallas_call(
        _sample_inproj_kernel,
        out_shape=jax.ShapeDtypeStruct((n, N_IN), F32),
        grid=(N_IN // d,),
        in_specs=[
            pl.BlockSpec((n, d), lambda j: (0, 0)),
            pl.BlockSpec((None, n, 6 * d), lambda j: (layer, 0, 0)),
            pl.BlockSpec((None, 1, d), lambda j: (layer, 0, 0)),
            pl.BlockSpec((None, d, d), lambda j: (layer, 0, j)),
        ],
        out_specs=pl.BlockSpec((n, d), lambda j: (0, j)),
        compiler_params=pltpu.CompilerParams(
            dimension_semantics=("arbitrary",), vmem_limit_bytes=VMEM_LIMIT),
        name=f"sample_inproj_{layer}",
    )(x, mod_s, wl["n1g"], wl["w_in"])


N_SAMPLE_MIXER_INPUTS = 20

OUTER_ROWS = 16
OUTER_LEFT_SRC = (0, 0, 1, 0, 2, 1, 3, 4, 5)
OUTER_RIGHT_SRC = (0, 1, 0, 2, 0, 1)


def _outer_tables():
    g = SAMPLE_GROUP
    sel_l = np.zeros((g * OUTER_ROWS, 6 * g), np.float32)
    sel_r = np.zeros((g * OUTER_ROWS, 4 * g), np.float32)
    for j in range(g):
        for r, a in enumerate(OUTER_LEFT_SRC):
            sel_l[OUTER_ROWS * j + r, g * a + j] = 1.0
        for r, a in enumerate(OUTER_RIGHT_SRC):
            sel_r[OUTER_ROWS * j + r, g * a + j] = 1.0
    ones_rows = np.zeros((g, OUTER_ROWS, LANES), np.float32)
    ones_rows[:, len(OUTER_RIGHT_SRC):len(OUTER_LEFT_SRC)] = 1.0
    return (jnp.asarray(sel_l, BF16), jnp.asarray(sel_r, BF16),
            jnp.asarray(ones_rows.reshape(g * OUTER_ROWS, LANES), BF16))


def _split3(x):
    x1 = x.astype(BF16).astype(F32)
    r = x - x1
    x2 = r.astype(BF16).astype(F32)
    x3 = (r - x2).astype(BF16).astype(F32)
    return [x1, x2, x3]


def _sample_mixer_kernel(layer, n_groups, *refs):
    refs = refs[:N_SAMPLE_MIXER_INPUTS] + refs[len(refs) - 10:]
    (u_ref, x_ref, mod_ref, hg_ref, lru_ref, conv_ref, hgl_ref, ong_ref, cw_ref, cb_ref,
     wax_ref, ba_ref, bx_ref, lam_ref, wba_ref, wbb_ref, wout_ref, sell_ref, selr_ref, ones_ref,
     xo_ref, hgo_ref, lruo_ref, convo_ref,
     f_s, k_s, q_s, v_s, o_s, ob_s) = refs
    d = D_MODEL
    n = x_ref.shape[0]
    i = pl.program_id(0)
    if layer == 0:
        for other in range(1, DEPTH):
            hgo_ref[other] = jnp.zeros(hgo_ref.shape[1:], F32)
        hgo_ref = hgo_ref.at[0]

    @pl.when(i == 0)
    def _():
        q_s[...] = _silu(u_ref[:, 0:d])
        lb = _lower_bound(hgl_ref[...], layer)
        f, k = _forget_key(u_ref[:, d:2 * d], lb)
        f_s[...] = f
        k_s[...] = k
        v_s[...] = u_ref[:, 2 * d:3 * d]
        xb = u_ref[:, 4 * d:5 * d]
        cw = cw_ref[...]
        xc = cb_ref[...] + cw[3:4] * xb
        for j in range(CONV_W - 1):
            xc = xc + cw[j:j + 1] * conv_ref[j]
        convo_ref[0] = conv_ref[1]
        convo_ref[1] = conv_ref[2]
        convo_ref[2] = xb
        yb = _gelu_tanh(u_ref[:, 5 * d:6 * d])
        sp = _softplus(-lam_ref[...])
        for blk in range(LRU_BLOCKS):
            cols = slice(blk * LRU_BS, (blk + 1) * LRU_BS)
            xcn = xc[:, cols]
            gates = _dot(xcn.astype(BF16), wax_ref[blk])
            a, bt = _lru_coeffs(xcn, gates, ba_ref[:, cols], bx_ref[:, cols], sp[:, cols])
            hn = a * lru_ref[:, cols] + bt
            lruo_ref[:, cols] = hn
            ob_s[:, cols] = hn * yb[:, cols]

    rows = pl.ds(pl.multiple_of(i * SAMPLE_GROUP, SAMPLE_GROUP), SAMPLE_GROUP)
    ones_rows = ones_ref[...]

    def outer_products(h):
        cols = slice(h * LANES, (h + 1) * LANES)
        vparts = _split3(v_s[rows, cols])
        left = jnp.concatenate(_split3(k_s[rows, cols]) + _split3(f_s[rows, cols]), axis=0).astype(BF16)
        right = jnp.concatenate(vparts + [jnp.zeros_like(vparts[0])], axis=0).astype(BF16)
        left_t = _dot(sell_ref[...], left).T.astype(BF16)
        right_rows = jnp.concatenate([_dot(selr_ref[...], right).astype(BF16), ones_rows], axis=1)
        out = []
        for j in range(SAMPLE_GROUP):
            lo, hi = j * OUTER_ROWS, (j + 1) * OUTER_ROWS
            pieces = [right_rows[lo:hi]]
            if lo:
                pieces.insert(0, jnp.zeros((lo, 2 * HG_DV), BF16))
            if hi < SAMPLE_GROUP * OUTER_ROWS:
                pieces.append(jnp.zeros((SAMPLE_GROUP * OUTER_ROWS - hi, 2 * HG_DV), BF16))
            out.append(_dot(left_t, jnp.concatenate(pieces, axis=0)))
        return out

    kv_f = outer_products(0)
    for h in range(HG_HEADS):
        cols = slice(h * LANES, (h + 1) * LANES)
        kv_f_next = outer_products(h + 1) if h + 1 < HG_HEADS else None
        s_new = [hg_ref[j, h] * kv_f[j][:, HG_DV:] + kv_f[j][:, :HG_DV] for j in range(SAMPLE_GROUP)]
        for j in range(SAMPLE_GROUP):
            hgo_ref[j, h] = s_new[j]
        qg = q_s[rows, cols].astype(BF16)
        o_rows = [_dot(qg, s_new[j].astype(BF16))[j:j + 1, :] for j in range(SAMPLE_GROUP)]
        o_s[rows, cols] = jnp.concatenate(o_rows, axis=0)
        kv_f = kv_f_next

    @pl.when(i == n_groups - 1)
    def _():
        ong = ong_ref[...]
        parts = []
        for h in range(HG_HEADS):
            cols = slice(h * LANES, (h + 1) * LANES)
            parts.append(_rmsnorm(o_s[:, cols], ong) * _silu(u_ref[:, 3 * d + h * LANES:3 * d + (h + 1) * LANES]))
        o_a = jnp.concatenate(parts, axis=1).astype(BF16)
        merged = (_sigmoid(u_ref[:, 6 * d:7 * d]) * _dot(o_a, wba_ref[...])
                  + _sigmoid(u_ref[:, 7 * d:8 * d]) * _dot(ob_s[...].astype(BF16), wbb_ref[...]))
        g1 = mod_ref[:, 2 * d:3 * d]
        xo_ref[...] = x_ref[...] + g1 * _dot(merged.astype(BF16), wout_ref[...])


def _sample_mixer_call(layer, u, x, mod_s, st_hg_all, st_lru_all, st_conv_t, hg_lower, wl, hg_out_prev):
    n, d = x.shape
    grp = SAMPLE_GROUP
    n_groups = n // grp

    def full(shape):
        nd = len(shape)
        return pl.BlockSpec(shape, lambda i: (0,) * nd)

    def of_layer(shape):
        nd = len(shape)
        return pl.BlockSpec((None,) + tuple(shape), lambda i: (layer,) + (0,) * nd)

    st_spec = pl.BlockSpec((None, grp, HG_HEADS, HG_DK, HG_DV), lambda i: (layer, i, 0, 0, 0))
    out_shapes = (
        jax.ShapeDtypeStruct((n, d), F32),
        jax.ShapeDtypeStruct((DEPTH, n, HG_HEADS, HG_DK, HG_DV), F32),
        jax.ShapeDtypeStruct((n, d), F32),
        jax.ShapeDtypeStruct((CONV_W - 1, n, d), F32),
    )
    in_specs = [
        full((n, N_IN)), full((n, d)), of_layer((n, 6 * d)), st_spec,
        of_layer((n, d)), full((CONV_W - 1, n, d)),
        full((DEPTH, d)), of_layer((1, HG_DV)), of_layer((CONV_W, d)), of_layer((1, d)),
        of_layer((LRU_BLOCKS, LRU_BS, 2 * LRU_BS)), of_layer((1, d)), of_layer((1, d)), of_layer((1, d)),
        of_layer((d, d)), of_layer((d, d)), of_layer((d, d)),
    ]
    outer_tables = _outer_tables()
    in_specs += [full(t.shape) for t in outer_tables]
    assert len(in_specs) == N_SAMPLE_MIXER_INPUTS
    args = [u, x, mod_s, st_hg_all, st_lru_all, st_conv_t, hg_lower, wl["ong"], wl["cw"], wl["cb"], wl["wax"],
            wl["ba"], wl["bx"], wl["lam"], wl["wba"], wl["wbb"], wl["wout"], *outer_tables]
    aliases = {}
    if hg_out_prev is not None:
        in_specs.append(pl.BlockSpec(memory_space=pl.ANY))
        args.append(hg_out_prev)
        aliases = {N_SAMPLE_MIXER_INPUTS: 1}
    st_out_spec = st_spec if layer else pl.BlockSpec((DEPTH, grp, HG_HEADS, HG_DK, HG_DV),
                                                     lambda i: (0, i, 0, 0, 0))
    out_specs = (full((n, d)), st_out_spec, full((n, d)), full((CONV_W - 1, n, d)))
    scratch = [
        pltpu.VMEM((n, d), F32),
        pltpu.VMEM((n, d), F32),
        pltpu.VMEM((n, d), F32),
        pltpu.VMEM((n, d), F32),
        pltpu.VMEM((n, d), F32),
        pltpu.VMEM((n, d), F32),
    ]
    return pl.pallas_call(
        functools.partial(_sample_mixer_kernel, layer, n_groups),
        out_shape=out_shapes,
        grid=(n_groups,),
        in_specs=in_specs,
        out_specs=out_specs,
        scratch_shapes=scratch,
        input_output_aliases=aliases,
        compiler_params=pltpu.CompilerParams(
            dimension_semantics=("arbitrary",), vmem_limit_bytes=VMEM_LIMIT),
        name=f"sample_mixer_{layer}",
    )(*args)


def _stacked_weights(norm1_g, norm2_g, w_in, hg_onorm_g, lru_conv_w, lru_conv_b, lru_wa, lru_ba, lru_wx,
                     lru_bx, lru_lambda, w_branch_a, w_branch_b, w_out, w_ff1, w_ff2):
    d = D_MODEL
    return {
        "n1g": norm1_g.reshape(DEPTH, 1, d),
        "n2g": norm2_g.reshape(DEPTH, 1, d),
        "w_in": w_in.astype(BF16),
        "ong": hg_onorm_g.reshape(DEPTH, 1, HG_DV),
        "cw": lru_conv_w,
        "cb": lru_conv_b.reshape(DEPTH, 1, d),
        "wax": jnp.concatenate([lru_wa, lru_wx], axis=-1).astype(BF16),
        "ba": lru_ba.reshape(DEPTH, 1, d),
        "bx": lru_bx.reshape(DEPTH, 1, d),
        "lam": lru_lambda.reshape(DEPTH, 1, d),
        "wba": w_branch_a.astype(BF16),
        "wbb": w_branch_b.astype(BF16),
        "wout": w_out.astype(BF16),
        "w1": w_ff1,
        "w2": w_ff2,
    }


def kernel(x_prompt, x_sample, state_hgrn, state_rglru, state_conv, c_prompt, c_sample, w_mod, b_mod, norm1_g, norm2_g, w_in, hg_lower, hg_onorm_g, lru_conv_w, lru_conv_b, lru_wa, lru_ba, lru_wx, lru_bx, lru_lambda, w_branch_a, w_branch_b, w_out, w_ff1, w_ff2, final_norm_g):
    bsz, seq, d = x_prompt.shape
    n_s = x_sample.shape[0]
    assert d == D_MODEL and x_sample.shape[1] == 1 and seq % MIX_TL == 0 and seq % FFN_TM == 0
    assert n_s % SAMPLE_GROUP == 0 and n_s == LANES

    mod_p_all, mod_s = _mod_call(c_prompt, c_sample, w_mod, b_mod)
    lvl = _hg_tables()
    final_g = final_norm_g.reshape(1, d)

    xp = x_prompt
    xs = x_sample.reshape(n_s, d)
    hg_p, lru_p, conv_p, lru_s, conv_s = [], [], [], [], []
    hg_s = None
    wl = _stacked_weights(norm1_g, norm2_g, w_in, hg_onorm_g, lru_conv_w, lru_conv_b, lru_wa, lru_ba,
                          lru_wx, lru_bx, lru_lambda, w_branch_a, w_branch_b, w_out, w_ff1, w_ff2)
    for l in range(DEPTH):
        mod_p = mod_p_all[l].reshape(bsz, 1, 6 * d)

        xp, hg, lru, conv = _prompt_mixer_call(l, xp, mod_p, hg_lower, wl, lvl)
        xp = _ffn_call(l, xp, mod_p, wl, final_g, FFN_TM, f"prompt_ffn_{l}")
        hg_p.append(hg)
        lru_p.append(lru.reshape(bsz, d))
        conv_p.append(conv)

        u = _sample_inproj_call(l, xs, mod_s, wl)
        xs, hg_s, lru, conv = _sample_mixer_call(l, u, xs, mod_s, state_hgrn, state_rglru,
                                                 jnp.swapaxes(state_conv[l], 0, 1), hg_lower, wl, hg_s)
        xs = _ffn_call(l, xs.reshape(1, n_s, d), mod_s, wl, final_g, n_s, f"sample_ffn_{l}",
                       per_token_mod=True).reshape(n_s, d)
        lru_s.append(lru)
        conv_s.append(jnp.swapaxes(conv, 0, 1))

    return (xp, xs.reshape(n_s, 1, d),
            jnp.stack(hg_p), jnp.stack(lru_p), jnp.stack(conv_p),
            hg_s, jnp.stack(lru_s), jnp.stack(conv_s))
```

```python
import functools

import numpy as np
import jax
import jax.numpy as jnp
from jax import lax
from jax.experimental import pallas as pl
from jax.experimental.pallas import tpu as pltpu

F32 = jnp.float32
BF16 = jnp.bfloat16

D_MODEL = 1024
DEPTH = 2
HG_HEADS = 8
HG_DK = 128
HG_DV = 128
LB_FLOOR = 1e-30
LRU_BLOCKS = 8
LRU_BS = 128
LRU_C = 8.0
CONV_W = 4
D_FF = 4 * D_MODEL
EPS = 1e-6
N_IN = 8 * D_MODEL
F32_TINY = float(np.finfo(np.float32).tiny)

LANES = 128
SUBLANES = 8
HG_CHUNK = 128
HG_LEVELS = 7
MIX_TL = 256
FFN_TM = 512
SAMPLE_GROUP = 8
VMEM_LIMIT = 56 * 1024 * 1024


def _sigmoid(x):
    return 0.5 * jnp.tanh(0.5 * x) + 0.5


def _silu(x):
    h = 0.5 * x
    return h * jnp.tanh(h) + h


def _gelu_tanh(x):
    c = 0.7978845608028654
    h = 0.5 * x
    return h * jnp.tanh(x * (c + (c * 0.044715) * (x * x))) + h


def _half_tanh_gate(t, y):
    h = 0.5 * y
    return h * t + h


def _log1p(x):
    u = 1.0 + x
    return jnp.where(u == 1.0, x, jnp.log(u) * (x / (u - 1.0)))


def _softplus(z):
    return jnp.maximum(z, 0.0) + _log1p(jnp.exp(-jnp.abs(z)))


def _rmsnorm(x, g):
    ms = jnp.mean(x * x, axis=-1, keepdims=True)
    return x * lax.rsqrt(ms + EPS) * g


def _dot(a, b):
    return jnp.dot(a, b, preferred_element_type=F32)


def _dot_nt(a, b):
    return lax.dot_general(a, b, (((1,), (1,)), ((), ())), preferred_element_type=F32)


def _dot_tn(a, b):
    return lax.dot_general(a, b, (((0,), (0,)), ((), ())), preferred_element_type=F32)


def _lower_bound(hgl, layer):
    m = jnp.max(hgl, axis=0, keepdims=True)
    e = jnp.exp(hgl - m)
    p = e / jnp.sum(e, axis=0, keepdims=True)
    cum = p[0:1]
    for j in range(1, layer + 1):
        cum = cum + p[j:j + 1]
    return cum - p[0:1]


def _forget_key(fa, lb):
    e = jnp.exp(-jnp.abs(fa))
    r = 1.0 / (1.0 + e)
    er = e * r
    pos = fa >= 0.0
    sig_p = jnp.where(pos, r, er)
    sig_n = jnp.where(pos, er, r)
    f = jnp.maximum(lb, LB_FLOOR) + (1.0 - lb) * sig_p
    k = (1.0 - lb) * sig_n
    return f, k


def _lru_coeffs(xc, gates, ba, bx, sp):
    tr = jnp.tanh(0.5 * gates[:, :LRU_BS] + 0.5 * ba)
    ti = jnp.tanh(0.5 * gates[:, LRU_BS:] + 0.5 * bx)
    c1 = (-0.5 * LRU_C) * sp
    log_a = c1 * tr + c1
    a = jnp.exp(log_a)
    th = jnp.tanh(log_a)
    one_m_a2 = (-2.0 * th) / (1.0 - th)
    root = one_m_a2 * lax.rsqrt(jnp.maximum(one_m_a2, F32_TINY))
    return a, root * _half_tanh_gate(ti, xc)


def _mod_kernel(cp_ref, cs_ref, w_ref, b_ref, op_ref, os_ref):
    w = w_ref[0].astype(BF16)
    op_ref[0] = _dot(_silu(cp_ref[...]).astype(BF16), w) + b_ref[0]
    os_ref[0] = _dot(_silu(cs_ref[...]).astype(BF16), w) + b_ref[0]


def _mod_call(c_prompt, c_sample, w_mod, b_mod):
    n_p, n_s = c_prompt.shape[0], c_sample.shape[0]
    tn = 1536
    return pl.pallas_call(
        _mod_kernel,
        out_shape=(jax.ShapeDtypeStruct((DEPTH, n_p, 6 * D_MODEL), F32),
                   jax.ShapeDtypeStruct((DEPTH, n_s, 6 * D_MODEL), F32)),
        grid=(DEPTH, 6 * D_MODEL // tn),
        in_specs=[
            pl.BlockSpec((n_p, D_MODEL), lambda l, j: (0, 0)),
            pl.BlockSpec((n_s, D_MODEL), lambda l, j: (0, 0)),
            pl.BlockSpec((1, D_MODEL, tn), lambda l, j: (l, 0, j)),
            pl.BlockSpec((1, 1, tn), lambda l, j: (l, 0, j)),
        ],
        out_specs=(pl.BlockSpec((1, n_p, tn), lambda l, j: (l, 0, j)),
                   pl.BlockSpec((1, n_s, tn), lambda l, j: (l, 0, j))),
        compiler_params=pltpu.CompilerParams(
            dimension_semantics=("arbitrary", "arbitrary"), vmem_limit_bytes=VMEM_LIMIT),
        name="mod",
    )(c_prompt, c_sample, w_mod, b_mod.reshape(DEPTH, 1, 6 * D_MODEL))


def _hg_tables():
    t = np.arange(HG_CHUNK)
    x = t[:, None] ^ t[None, :]
    hb = np.floor(np.log2(np.maximum(x, 1))).astype(np.int32)
    lvl = np.where(t[:, None] > t[None, :], hb + 1, np.where(t[:, None] == t[None, :], 0, -1)).astype(np.int32)
    return jnp.asarray(lvl)


def _x_within_group(q, k, g2, level):
    n = q.shape[0]
    shp = (n // SUBLANES, SUBLANES, LANES)
    q3, k3, g3 = q.reshape(shp), k.reshape(shp), g2.reshape(shp)
    sub = lax.broadcasted_iota(jnp.int32, (1, SUBLANES, LANES), 1)
    qside = ((sub >> level) & 1) == 1
    if level == 0:
        scaled = jnp.where(qside, q3 * jnp.exp2(g3 - pltpu.roll(g3, 1, axis=1)), k3)
        return scaled.reshape(n, LANES)

    def row(i):
        return jnp.broadcast_to(g3[:, i:i + 1, :], shp)

    piv = jnp.where(sub < 4, row(1), row(5)) if level == 1 else row(3)
    e = jnp.exp2(-jnp.abs(g3 - piv))
    return (jnp.where(qside, q3, k3) * e).reshape(n, LANES)


def _x_across_groups(q, k, g2, level):
    n = q.shape[0]
    b = 1 << level
    blocks = []
    for base in range(0, n, 2 * b):
        lo, mid, hi = base, base + b, base + 2 * b
        gp = g2[mid - 1:mid, :]
        blocks.append(k[lo:mid] * jnp.exp2(gp - g2[lo:mid]))
        blocks.append(q[mid:hi] * jnp.exp2(g2[mid:hi] - gp))
    return jnp.concatenate(blocks, axis=0)


HG_FULL_LEVELS = 4


def _hg_level_products(q, k, g2):
    c = HG_CHUNK
    prods = [_dot_nt(q.astype(BF16), k.astype(BF16))]
    for level in range(HG_LEVELS):
        b = 1 << level
        if b < SUBLANES:
            x = _x_within_group(q, k, g2, level).astype(BF16)
        else:
            x = _x_across_groups(q, k, g2, level).astype(BF16)
        if level < HG_FULL_LEVELS:
            lhs = x
        else:
            lhs = jnp.concatenate([x[base + b:base + 2 * b] for base in range(0, c, 2 * b)], axis=0)
        prods.append(_dot_nt(lhs, x))
    return prods


def _hg_assemble(prods_per_head, lvl_ref):
    c = HG_CHUNK
    n_heads = len(prods_per_head)
    groups = [[] for _ in range(n_heads)]
    for gi in range(c // SUBLANES):
        r0 = gi * SUBLANES
        rows = slice(r0, r0 + SUBLANES)
        lv = lvl_ref[rows, :]
        on_diag = lv == 0
        a_g = [jnp.where(on_diag, p[0][rows], 0.0) for p in prods_per_head]
        for level in range(HG_LEVELS):
            b = 1 << level
            if level < HG_FULL_LEVELS:
                src = r0
            elif r0 & b:
                src = (r0 >> (level + 1)) * b + (r0 & (b - 1))
            else:
                continue
            at_level = lv == level + 1
            a_g = [jnp.where(at_level, p[level + 1][src:src + SUBLANES], a)
                   for p, a in zip(prods_per_head, a_g)]
        for h in range(n_heads):
            groups[h].append(a_g[h])
    return [jnp.concatenate(g, axis=0).astype(BF16) for g in groups]


def _hg_finish(q, k, v, g2, st, a):
    c = HG_CHUNK
    g_last = g2[c - 1:c, :]
    qg = (q * jnp.exp2(g2)).astype(BF16)
    vb = v.astype(BF16)
    o = _dot_nt(qg, st.astype(BF16)) + _dot(a, vb)
    kd = (k * jnp.exp2(g_last - g2)).astype(BF16)
    st_new = st * jnp.exp2(g_last) + _dot_tn(vb, kd)
    return o, st_new


def _causal_conv(xb, prev8, cw_ref, cb_ref):
    t, w = xb.shape
    x3 = xb.reshape(t // SUBLANES, SUBLANES, w)
    sub = lax.broadcasted_iota(jnp.int32, (1, SUBLANES, w), 1)

    def tap(j):
        return cw_ref[j:j + 1, :][None]

    out = cb_ref[...][None] + tap(CONV_W - 1) * x3
    for back in range(1, CONV_W):
        r = pltpu.roll(x3, back, axis=1)
        r_prev_group = jnp.concatenate([pltpu.roll(prev8, back, axis=0)[None], r[:-1]], axis=0)
        shifted = jnp.where(sub >= back, r, r_prev_group)
        out = out + tap(CONV_W - 1 - back) * shifted
    return out.reshape(t, w)


def _chunk_cumsum(y):
    t = y.shape[0]
    y3 = y.reshape(t // SUBLANES, SUBLANES, LANES)
    sub = lax.broadcasted_iota(jnp.int32, (1, SUBLANES, LANES), 1)
    d = 1
    while d < SUBLANES:
        y3 = y3 + jnp.where(sub >= d, pltpu.roll(y3, d, axis=1), 0.0)
        d *= 2
    per_chunk = HG_CHUNK // SUBLANES
    out = []
    for gi in range(t // SUBLANES):
        blk = y3[gi]
        if gi % per_chunk:
            blk = blk + out[-1][SUBLANES - 1:SUBLANES, :]
        out.append(blk)
    return jnp.concatenate(out, axis=0)


def _lru_scan(a, b, h0):
    t = a.shape[0]
    a3 = a.reshape(t // SUBLANES, SUBLANES, LANES)
    b3 = b.reshape(t // SUBLANES, SUBLANES, LANES)
    sub = lax.broadcasted_iota(jnp.int32, a3.shape, 1)
    d = 1
    while d < SUBLANES:
        m = sub >= d
        b3 = a3 * jnp.where(m, pltpu.roll(b3, d, axis=1), 0.0) + b3
        a3 = a3 * jnp.where(m, pltpu.roll(a3, d, axis=1), 1.0)
        d *= 2
    hs = []
    h = h0
    for gidx in range(t // SUBLANES):
        hg = a3[gidx] * h + b3[gidx]
        hs.append(hg)
        h = hg[SUBLANES - 1:SUBLANES, :]
    return jnp.concatenate(hs, axis=0), h


def _prompt_mixer_kernel(layer, n_t,
                         x_ref, mod_ref, hgl_ref, n1g_ref, win_ref, ong_ref, cw_ref, cb_ref, wax_ref,
                         ba_ref, bx_ref, lam_ref, wba_ref, wbb_ref, wout_ref, lvl_ref,
                         xo_ref, hg_ref, lru_ref, conv_ref,
                         q_s, k_s, v_s, g_s, og_s, oa_s, ob_s, prev8_s, st_s, hl_s, xc_s, yb_s, ga_s, gb_s, gr_s,
                         gi_s):
    tl = MIX_TL
    d = D_MODEL
    t_idx = pl.program_id(1)

    @pl.when(t_idx == 0)
    def _():
        st_s[...] = jnp.zeros(st_s.shape, F32)
        hl_s[...] = jnp.zeros(hl_s.shape, F32)
        prev8_s[...] = jnp.zeros(prev8_s.shape, F32)

    x = x_ref[0]
    mod = mod_ref[0]
    sh1, sc1, g1 = mod[:, 0:d], mod[:, d:2 * d], mod[:, 2 * d:3 * d]
    hb = (_rmsnorm(x, n1g_ref[...]) * (1.0 + sc1) + sh1).astype(BF16)

    def proj(j):
        return _dot(hb, win_ref[:, j * d:(j + 1) * d])

    def heads(val, dst):
        for h in range(HG_HEADS):
            dst[h] = val[:, h * LANES:(h + 1) * LANES]

    sp = _softplus(-lam_ref[...])
    ong = ong_ref[...]

    def lru_block(n):
        cols = slice(n * LRU_BS, (n + 1) * LRU_BS)
        gates = jnp.concatenate([gr_s[:, cols], gi_s[:, cols]], axis=1)
        a, bt = _lru_coeffs(xc_s[:, cols], gates, ba_ref[:, cols], bx_ref[:, cols], sp[:, cols])
        hseq, h_last = _lru_scan(a, bt, hl_s[:, cols])
        hl_s[:, cols] = h_last
        ob_s[:, cols] = hseq * yb_s[:, cols]

    def level_products(c):
        rows = slice(c * HG_CHUNK, (c + 1) * HG_CHUNK)
        return [_hg_level_products(q_s[h, rows, :], k_s[h, rows, :], g_s[h, rows, :]) for h in range(HG_HEADS)]

    def finish(c, a_mats):
        rows = slice(c * HG_CHUNK, (c + 1) * HG_CHUNK)
        for h in range(HG_HEADS):
            o, st_new = _hg_finish(q_s[h, rows, :], k_s[h, rows, :], v_s[h, rows, :], g_s[h, rows, :],
                                   st_s[h], a_mats[h])
            st_s[h] = st_new
            oa_s[h, rows, :] = _rmsnorm(o, og_s[h, rows, :])

    xb = proj(4)
    xc = _causal_conv(xb, prev8_s[...], cw_ref, cb_ref)
    xc_s[...] = xc
    prev8_s[...] = xb[tl - SUBLANES:tl, :]
    conv_ref[0] = xb[tl - (CONV_W - 1):tl, :]
    for n in range(LRU_BLOCKS):
        cols = slice(n * LRU_BS, (n + 1) * LRU_BS)
        gates = _dot(xc[:, cols].astype(BF16), wax_ref[n])
        gr_s[:, cols] = gates[:, :LRU_BS]
        gi_s[:, cols] = gates[:, LRU_BS:]
    yb_s[...] = _gelu_tanh(proj(5))
    lb = _lower_bound(hgl_ref[...], layer)
    f, k = _forget_key(proj(1), lb)
    heads(k, k_s)
    log2f = jnp.minimum(jnp.log2(f), 0.0)
    for h in range(HG_HEADS):
        g_s[h] = _chunk_cumsum(log2f[:, h * LANES:(h + 1) * LANES])
    lru_block(0)
    lru_block(1)
    heads(_silu(proj(0)), q_s)
    lru_block(2)
    lru_block(3)
    heads(proj(2), v_s)
    lru_block(4)
    lru_block(5)
    og = _silu(proj(3))
    for h in range(HG_HEADS):
        og_s[h] = og[:, h * LANES:(h + 1) * LANES] * ong
    lru_block(6)
    lru_block(7)
    lru_ref[0] = hl_s[...]
    ga_s[...] = jnp.tanh(0.5 * proj(6))
    gb_s[...] = jnp.tanh(0.5 * proj(7))
    for c in range(tl // HG_CHUNK):
        finish(c, _hg_assemble(level_products(c), lvl_ref))

    o_a = jnp.concatenate([oa_s[h] for h in range(HG_HEADS)], axis=1).astype(BF16)
    merged = (_half_tanh_gate(ga_s[...], _dot(o_a, wba_ref[...]))
              + _half_tanh_gate(gb_s[...], _dot(ob_s[...].astype(BF16), wbb_ref[...])))
    xo_ref[0] = x + g1 * _dot(merged.astype(BF16), wout_ref[...])

    @pl.when(t_idx == n_t - 1)
    def _():
        for h in range(HG_HEADS):
            hg_ref[0, h] = st_s[h].T


def _const_spec(shape, layer=None):
    nd = len(shape)
    if layer is None:
        return pl.BlockSpec(shape, lambda *_: (0,) * nd, pipeline_mode=pl.Buffered(1))
    return pl.BlockSpec((None,) + tuple(shape), lambda *_: (layer,) + (0,) * nd, pipeline_mode=pl.Buffered(1))


def _prompt_mixer_call(layer, x, mod_p, hg_lower, wl, w_in_bf16, lvl):
    bsz, seq, d = x.shape
    tl = MIX_TL
    n_t = seq // tl
    hshape = (HG_HEADS, tl, LANES)
    out_shapes = (
        jax.ShapeDtypeStruct((bsz, seq, d), F32),
        jax.ShapeDtypeStruct((bsz, HG_HEADS, HG_DK, HG_DV), F32),
        jax.ShapeDtypeStruct((bsz, 1, d), F32),
        jax.ShapeDtypeStruct((bsz, CONV_W - 1, d), F32),
    )
    in_specs = [
        pl.BlockSpec((1, tl, d), lambda b, t: (b, t, 0)),
        pl.BlockSpec((1, 1, 6 * d), lambda b, t: (b, 0, 0)),
        _const_spec((DEPTH, d)),
        _const_spec((1, d), layer),
        _const_spec((d, N_IN)),
        _const_spec((1, HG_DV), layer),
        _const_spec((CONV_W, d), layer),
        _const_spec((1, d), layer),
        _const_spec((LRU_BLOCKS, LRU_BS, 2 * LRU_BS), layer),
        _const_spec((1, d), layer),
        _const_spec((1, d), layer),
        _const_spec((1, d), layer),
        _const_spec((d, d), layer),
        _const_spec((d, d), layer),
        _const_spec((d, d), layer),
        _const_spec((HG_CHUNK, HG_CHUNK)),
    ]
    out_specs = (
        pl.BlockSpec((1, tl, d), lambda b, t: (b, t, 0)),
        pl.BlockSpec((1, HG_HEADS, HG_DK, HG_DV), lambda b, t: (b, 0, 0, 0)),
        pl.BlockSpec((1, 1, d), lambda b, t: (b, 0, 0)),
        pl.BlockSpec((1, CONV_W - 1, d), lambda b, t: (b, 0, 0)),
    )
    scratch = [
        pltpu.VMEM(hshape, F32),
        pltpu.VMEM(hshape, F32),
        pltpu.VMEM(hshape, F32),
        pltpu.VMEM(hshape, F32),
        pltpu.VMEM(hshape, F32),
        pltpu.VMEM(hshape, F32),
        pltpu.VMEM((tl, d), F32),
        pltpu.VMEM((SUBLANES, d), F32),
        pltpu.VMEM((HG_HEADS, HG_DV, HG_DK), F32),
        pltpu.VMEM((1, d), F32),
        pltpu.VMEM((tl, d), F32),
        pltpu.VMEM((tl, d), F32),
        pltpu.VMEM((tl, d), F32),
        pltpu.VMEM((tl, d), F32),
        pltpu.VMEM((tl, d), F32),
        pltpu.VMEM((tl, d), F32),
    ]
    return pl.pallas_call(
        functools.partial(_prompt_mixer_kernel, layer, n_t),
        out_shape=out_shapes,
        grid=(bsz, n_t),
        in_specs=in_specs,
        out_specs=out_specs,
        scratch_shapes=scratch,
        compiler_params=pltpu.CompilerParams(
            dimension_semantics=("arbitrary", "arbitrary"), vmem_limit_bytes=VMEM_LIMIT),
        name=f"prompt_mixer_{layer}",
    )(x, mod_p, hg_lower, wl["n1g"], w_in_bf16, wl["ong"], wl["cw"], wl["cb"], wl["wax"],
      wl["ba"], wl["bx"], wl["lam"], wl["wba"], wl["wbb"], wl["wout"], lvl)


def _ffn_kernel(final, x_ref, mod_ref, n2g_ref, w1_ref, w2_ref, fg_ref, o_ref):
    d = D_MODEL
    x = x_ref[0]
    mod = mod_ref[0]
    sh2, sc2, g2 = mod[:, 3 * d:4 * d], mod[:, 4 * d:5 * d], mod[:, 5 * d:6 * d]
    hb = (_rmsnorm(x, n2g_ref[...]) * (1.0 + sc2) + sh2).astype(BF16)
    acc = jnp.zeros(x.shape, F32)
    for c in range(D_FF // d):
        h1 = jnp.maximum(_dot(hb, w1_ref[:, c * d:(c + 1) * d].astype(BF16)), 0.0)
        acc = acc + _dot((h1 * h1).astype(BF16), w2_ref[c * d:(c + 1) * d, :].astype(BF16))
    xn = x + g2 * acc
    if final:
        xn = _rmsnorm(xn, fg_ref[...])
    o_ref[0] = xn


def _ffn_call(layer, x, mod, wl, final_g, tm, name, per_token_mod=False):
    final = layer == DEPTH - 1
    bsz, seq, d = x.shape
    if per_token_mod:
        assert bsz == 1 and mod.shape == (DEPTH, seq, 6 * d)
        mod_spec = pl.BlockSpec((1, tm, 6 * d), lambda b, t: (layer, t, 0))
    else:
        assert mod.shape == (bsz, 1, 6 * d)
        mod_spec = pl.BlockSpec((1, 1, 6 * d), lambda b, t: (b, 0, 0))
    return pl.pallas_call(
        functools.partial(_ffn_kernel, final),
        out_shape=jax.ShapeDtypeStruct((bsz, seq, d), F32),
        grid=(bsz, seq // tm),
        in_specs=[
            pl.BlockSpec((1, tm, d), lambda b, t: (b, t, 0)),
            mod_spec,
            _const_spec((1, d), layer),
            _const_spec((d, D_FF), layer),
            _const_spec((D_FF, d), layer),
            _const_spec((1, d)),
        ],
        out_specs=pl.BlockSpec((1, tm, d), lambda b, t: (b, t, 0)),
        compiler_params=pltpu.CompilerParams(
            dimension_semantics=("arbitrary", "arbitrary"), vmem_limit_bytes=VMEM_LIMIT),
        name=name,
    )(x, mod, wl["n2g"], wl["w1"], wl["w2"], final_g)


def _sample_inproj_kernel(x_ref, mod_ref, n1g_ref, w_ref, o_ref, wb_ref):
    d = D_MODEL
    mod = mod_ref[...]
    hb = (_rmsnorm(x_ref[...], n1g_ref[...]) * (1.0 + mod[:, d:2 * d]) + mod[:, 0:d]).astype(BF16)
    wb = w_ref[...].astype(BF16)
    wb_ref[...] = wb
    o_ref[...] = _dot(hb, wb)


def _sample_inproj_call(layer, x, mod_s, wl, w_in):
    n, d = x.shape
    return pl.pallas_call(
        _sample_inproj_kernel,
        out_shape=(jax.ShapeDtypeStruct((n, N_IN), F32), jax.ShapeDtypeStruct((d, N_IN), BF16)),
        grid=(N_IN // d,),
        in_specs=[
            pl.BlockSpec((n, d), lambda j: (0, 0)),
            pl.BlockSpec((None, n, 6 * d), lambda j: (layer, 0, 0)),
            pl.BlockSpec((None, 1, d), lambda j: (layer, 0, 0)),
            pl.BlockSpec((None, d, d), lambda j: (layer, 0, j)),
        ],
        out_specs=(pl.BlockSpec((n, d), lambda j: (0, j)), pl.BlockSpec((d, d), lambda j: (0, j))),
        compiler_params=pltpu.CompilerParams(
            dimension_semantics=("arbitrary",), vmem_limit_bytes=VMEM_LIMIT),
        name=f"sample_inproj_{layer}",
    )(x, mod_s, wl["n1g"], w_in)


N_SAMPLE_MIXER_INPUTS = 20

OUTER_ROWS = 16
OUTER_LEFT_SRC = (0, 0, 1, 0, 2, 1, 3, 4, 5)
OUTER_RIGHT_SRC = (0, 1, 0, 2, 0, 1)


def _outer_tables():
    g = SAMPLE_GROUP
    sel_l = np.zeros((g * OUTER_ROWS, 6 * g), np.float32)
    sel_r = np.zeros((g * OUTER_ROWS, 4 * g), np.float32)
    for j in range(g):
        for r, a in enumerate(OUTER_LEFT_SRC):
            sel_l[OUTER_ROWS * j + r, g * a + j] = 1.0
        for r, a in enumerate(OUTER_RIGHT_SRC):
            sel_r[OUTER_ROWS * j + r, g * a + j] = 1.0
    ones_rows = np.zeros((g, OUTER_ROWS, LANES), np.float32)
    ones_rows[:, len(OUTER_RIGHT_SRC):len(OUTER_LEFT_SRC)] = 1.0
    return (jnp.asarray(sel_l, BF16), jnp.asarray(sel_r, BF16),
            jnp.asarray(ones_rows.reshape(g * OUTER_ROWS, LANES), BF16))


def _split3(x):
    x1 = x.astype(BF16).astype(F32)
    r = x - x1
    x2 = r.astype(BF16).astype(F32)
    x3 = (r - x2).astype(BF16).astype(F32)
    return [x1, x2, x3]


def _sample_mixer_kernel(layer, n_groups, *refs):
    refs = refs[:N_SAMPLE_MIXER_INPUTS] + refs[len(refs) - 10:]
    (u_ref, x_ref, mod_ref, hg_ref, lru_ref, conv_ref, hgl_ref, ong_ref, cw_ref, cb_ref,
     wax_ref, ba_ref, bx_ref, lam_ref, wba_ref, wbb_ref, wout_ref, sell_ref, selr_ref, ones_ref,
     xo_ref, hgo_ref, lruo_ref, convo_ref,
     f_s, k_s, q_s, v_s, o_s, ob_s) = refs
    d = D_MODEL
    n = x_ref.shape[0]
    i = pl.program_id(0)
    if layer == 0:
        for other in range(1, DEPTH):
            hgo_ref[other] = jnp.zeros(hgo_ref.shape[1:], F32)
        hgo_ref = hgo_ref.at[0]

    @pl.when(i == 0)
    def _():
        q_s[...] = _silu(u_ref[:, 0:d])
        lb = _lower_bound(hgl_ref[...], layer)
        f, k = _forget_key(u_ref[:, d:2 * d], lb)
        f_s[...] = f
        k_s[...] = k
        v_s[...] = u_ref[:, 2 * d:3 * d]
        xb = u_ref[:, 4 * d:5 * d]
        cw = cw_ref[...]
        xc = cb_ref[...] + cw[3:4] * xb
        for j in range(CONV_W - 1):
            xc = xc + cw[j:j + 1] * conv_ref[j]
        convo_ref[0] = conv_ref[1]
        convo_ref[1] = conv_ref[2]
        convo_ref[2] = xb
        yb = _gelu_tanh(u_ref[:, 5 * d:6 * d])
        sp = _softplus(-lam_ref[...])
        for blk in range(LRU_BLOCKS):
            cols = slice(blk * LRU_BS, (blk + 1) * LRU_BS)
            xcn = xc[:, cols]
            gates = _dot(xcn.astype(BF16), wax_ref[blk])
            a, bt = _lru_coeffs(xcn, gates, ba_ref[:, cols], bx_ref[:, cols], sp[:, cols])
            hn = a * lru_ref[:, cols] + bt
            lruo_ref[:, cols] = hn
            ob_s[:, cols] = hn * yb[:, cols]

    rows = pl.ds(pl.multiple_of(i * SAMPLE_GROUP, SAMPLE_GROUP), SAMPLE_GROUP)
    ones_rows = ones_ref[...]

    def outer_products(h):
        cols = slice(h * LANES, (h + 1) * LANES)
        vparts = _split3(v_s[rows, cols])
        left = jnp.concatenate(_split3(k_s[rows, cols]) + _split3(f_s[rows, cols]), axis=0).astype(BF16)
        right = jnp.concatenate(vparts + [jnp.zeros_like(vparts[0])], axis=0).astype(BF16)
        left_t = _dot(sell_ref[...], left).T.astype(BF16)
        right_rows = jnp.concatenate([_dot(selr_ref[...], right).astype(BF16), ones_rows], axis=1)
        out = []
        for j in range(SAMPLE_GROUP):
            lo, hi = j * OUTER_ROWS, (j + 1) * OUTER_ROWS
            pieces = [right_rows[lo:hi]]
            if lo:
                pieces.insert(0, jnp.zeros((lo, 2 * HG_DV), BF16))
            if hi < SAMPLE_GROUP * OUTER_ROWS:
                pieces.append(jnp.zeros((SAMPLE_GROUP * OUTER_ROWS - hi, 2 * HG_DV), BF16))
            out.append(_dot(left_t, jnp.concatenate(pieces, axis=0)))
        return out

    kv_f = outer_products(0)
    for h in range(HG_HEADS):
        cols = slice(h * LANES, (h + 1) * LANES)
        kv_f_next = outer_products(h + 1) if h + 1 < HG_HEADS else None
        s_new = [hg_ref[j, h] * kv_f[j][:, HG_DV:] + kv_f[j][:, :HG_DV] for j in range(SAMPLE_GROUP)]
        for j in range(SAMPLE_GROUP):
            hgo_ref[j, h] = s_new[j]
        qg = q_s[rows, cols].astype(BF16)
        o_rows = [_dot(qg, s_new[j].astype(BF16))[j:j + 1, :] for j in range(SAMPLE_GROUP)]
        o_s[rows, cols] = jnp.concatenate(o_rows, axis=0)
        kv_f = kv_f_next

    @pl.when(i == n_groups - 1)
    def _():
        ong = ong_ref[...]
        parts = []
        for h in range(HG_HEADS):
            cols = slice(h * LANES, (h + 1) * LANES)
            parts.append(_rmsnorm(o_s[:, cols], ong) * _silu(u_ref[:, 3 * d + h * LANES:3 * d + (h + 1) * LANES]))
        o_a = jnp.concatenate(parts, axis=1).astype(BF16)
        merged = (_sigmoid(u_ref[:, 6 * d:7 * d]) * _dot(o_a, wba_ref[...])
                  + _sigmoid(u_ref[:, 7 * d:8 * d]) * _dot(ob_s[...].astype(BF16), wbb_ref[...]))
        g1 = mod_ref[:, 2 * d:3 * d]
        xo_ref[...] = x_ref[...] + g1 * _dot(merged.astype(BF16), wout_ref[...])


def _sample_mixer_call(layer, u, x, mod_s, st_hg_all, st_lru_all, st_conv_t, hg_lower, wl, hg_out_prev):
    n, d = x.shape
    grp = SAMPLE_GROUP
    n_groups = n // grp

    def full(shape):
        nd = len(shape)
        return pl.BlockSpec(shape, lambda i: (0,) * nd)

    def of_layer(shape):
        nd = len(shape)
        return pl.BlockSpec((None,) + tuple(shape), lambda i: (layer,) + (0,) * nd)

    st_spec = pl.BlockSpec((None, grp, HG_HEADS, HG_DK, HG_DV), lambda i: (layer, i, 0, 0, 0))
    out_shapes = (
        jax.ShapeDtypeStruct((n, d), F32),
        jax.ShapeDtypeStruct((DEPTH, n, HG_HEADS, HG_DK, HG_DV), F32),
        jax.ShapeDtypeStruct((n, d), F32),
        jax.ShapeDtypeStruct((CONV_W - 1, n, d), F32),
    )
    in_specs = [
        full((n, N_IN)), full((n, d)), of_layer((n, 6 * d)), st_spec,
        of_layer((n, d)), full((CONV_W - 1, n, d)),
        full((DEPTH, d)), of_layer((1, HG_DV)), of_layer((CONV_W, d)), of_layer((1, d)),
        of_layer((LRU_BLOCKS, LRU_BS, 2 * LRU_BS)), of_layer((1, d)), of_layer((1, d)), of_layer((1, d)),
        of_layer((d, d)), of_layer((d, d)), of_layer((d, d)),
    ]
    outer_tables = _outer_tables()
    in_specs += [full(t.shape) for t in outer_tables]
    assert len(in_specs) == N_SAMPLE_MIXER_INPUTS
    args = [u, x, mod_s, st_hg_all, st_lru_all, st_conv_t, hg_lower, wl["ong"], wl["cw"], wl["cb"], wl["wax"],
            wl["ba"], wl["bx"], wl["lam"], wl["wba"], wl["wbb"], wl["wout"], *outer_tables]
    aliases = {}
    if hg_out_prev is not None:
        in_specs.append(pl.BlockSpec(memory_space=pl.ANY))
        args.append(hg_out_prev)
        aliases = {N_SAMPLE_MIXER_INPUTS: 1}
    st_out_spec = st_spec if layer else pl.BlockSpec((DEPTH, grp, HG_HEADS, HG_DK, HG_DV),
                                                     lambda i: (0, i, 0, 0, 0))
    out_specs = (full((n, d)), st_out_spec, full((n, d)), full((CONV_W - 1, n, d)))
    scratch = [
        pltpu.VMEM((n, d), F32),
        pltpu.VMEM((n, d), F32),
        pltpu.VMEM((n, d), F32),
        pltpu.VMEM((n, d), F32),
        pltpu.VMEM((n, d), F32),
        pltpu.VMEM((n, d), F32),
    ]
    return pl.pallas_call(
        functools.partial(_sample_mixer_kernel, layer, n_groups),
        out_shape=out_shapes,
        grid=(n_groups,),
        in_specs=in_specs,
        out_specs=out_specs,
        scratch_shapes=scratch,
        input_output_aliases=aliases,
        compiler_params=pltpu.CompilerParams(
            dimension_semantics=("arbitrary",), vmem_limit_bytes=VMEM_LIMIT),
        name=f"sample_mixer_{layer}",
    )(*args)


def _stacked_weights(norm1_g, norm2_g, hg_onorm_g, lru_conv_w, lru_conv_b, lru_wa, lru_ba, lru_wx,
                     lru_bx, lru_lambda, w_branch_a, w_branch_b, w_out, w_ff1, w_ff2):
    d = D_MODEL
    return {
        "n1g": norm1_g.reshape(DEPTH, 1, d),
        "n2g": norm2_g.reshape(DEPTH, 1, d),
        "ong": hg_onorm_g.reshape(DEPTH, 1, HG_DV),
        "cw": lru_conv_w,
        "cb": lru_conv_b.reshape(DEPTH, 1, d),
        "wax": jnp.concatenate([lru_wa, lru_wx], axis=-1).astype(BF16),
        "ba": lru_ba.reshape(DEPTH, 1, d),
        "bx": lru_bx.reshape(DEPTH, 1, d),
        "lam": lru_lambda.reshape(DEPTH, 1, d),
        "wba": w_branch_a.astype(BF16),
        "wbb": w_branch_b.astype(BF16),
        "wout": w_out.astype(BF16),
        "w1": w_ff1,
        "w2": w_ff2,
    }


def kernel(x_prompt, x_sample, state_hgrn, state_rglru, state_conv, c_prompt, c_sample, w_mod, b_mod, norm1_g, norm2_g, w_in, hg_lower, hg_onorm_g, lru_conv_w, lru_conv_b, lru_wa, lru_ba, lru_wx, lru_bx, lru_lambda, w_branch_a, w_branch_b, w_out, w_ff1, w_ff2, final_norm_g):
    bsz, seq, d = x_prompt.shape
    n_s = x_sample.shape[0]
    assert d == D_MODEL and x_sample.shape[1] == 1 and seq % MIX_TL == 0 and seq % FFN_TM == 0
    assert n_s % SAMPLE_GROUP == 0 and n_s == LANES

    mod_p_all, mod_s = _mod_call(c_prompt, c_sample, w_mod, b_mod)
    lvl = _hg_tables()
    final_g = final_norm_g.reshape(1, d)

    xp = x_prompt
    xs = x_sample.reshape(n_s, d)
    hg_p, lru_p, conv_p, lru_s, conv_s = [], [], [], [], []
    hg_s = None
    wl = _stacked_weights(norm1_g, norm2_g, hg_onorm_g, lru_conv_w, lru_conv_b, lru_wa, lru_ba,
                          lru_wx, lru_bx, lru_lambda, w_branch_a, w_branch_b, w_out, w_ff1, w_ff2)
    for l in range(DEPTH):
        mod_p = mod_p_all[l].reshape(bsz, 1, 6 * d)

        u, w_in_bf16 = _sample_inproj_call(l, xs, mod_s, wl, w_in)

        xp, hg, lru, conv = _prompt_mixer_call(l, xp, mod_p, hg_lower, wl, w_in_bf16, lvl)
        xp = _ffn_call(l, xp, mod_p, wl, final_g, FFN_TM, f"prompt_ffn_{l}")
        hg_p.append(hg)
        lru_p.append(lru.reshape(bsz, d))
        conv_p.append(conv)

        xs, hg_s, lru, conv = _sample_mixer_call(l, u, xs, mod_s, state_hgrn, state_rglru,
                                                 jnp.swapaxes(state_conv[l], 0, 1), hg_lower, wl, hg_s)
        xs = _ffn_call(l, xs.reshape(1, n_s, d), mod_s, wl, final_g, n_s, f"sample_ffn_{l}",
                       per_token_mod=True).reshape(n_s, d)
        lru_s.append(lru)
        conv_s.append(jnp.swapaxes(conv, 0, 1))

    return (xp, xs.reshape(n_s, 1, d),
            jnp.stack(hg_p), jnp.stack(lru_p), jnp.stack(conv_p),
            hg_s, jnp.stack(lru_s), jnp.stack(conv_s))
```

```python
import functools

import numpy as np
import jax
import jax.numpy as jnp
from jax import lax
from jax.experimental import pallas as pl
from jax.experimental.pallas import tpu as pltpu

F32 = jnp.float32
BF16 = jnp.bfloat16

D_MODEL = 1024
DEPTH = 2
HG_HEADS = 8
HG_DK = 128
HG_DV = 128
LB_FLOOR = 1e-30
LRU_BLOCKS = 8
LRU_BS = 128
LRU_C = 8.0
CONV_W = 4
D_FF = 4 * D_MODEL
EPS = 1e-6
N_IN = 8 * D_MODEL
F32_TINY = float(np.finfo(np.float32).tiny)

LANES = 128
SUBLANES = 8
HG_CHUNK = 128
HG_LEVELS = 7
MIX_TL = 256
FFN_TM = 512
SAMPLE_GROUP = 8
VMEM_LIMIT = 56 * 1024 * 1024


def _sigmoid(x):
    return 0.5 * jnp.tanh(0.5 * x) + 0.5


def _silu(x):
    h = 0.5 * x
    return h * jnp.tanh(h) + h


def _gelu_tanh(x):
    c = 0.7978845608028654
    h = 0.5 * x
    return h * jnp.tanh(x * (c + (c * 0.044715) * (x * x))) + h


def _half_tanh_gate(t, y):
    h = 0.5 * y
    return h * t + h


def _log1p(x):
    u = 1.0 + x
    return jnp.where(u == 1.0, x, jnp.log(u) * (x / (u - 1.0)))


def _softplus(z):
    return jnp.maximum(z, 0.0) + _log1p(jnp.exp(-jnp.abs(z)))


def _rmsnorm(x, g):
    ms = jnp.mean(x * x, axis=-1, keepdims=True)
    return x * lax.rsqrt(ms + EPS) * g


def _dot(a, b):
    return jnp.dot(a, b, preferred_element_type=F32)


def _dot_nt(a, b):
    return lax.dot_general(a, b, (((1,), (1,)), ((), ())), preferred_element_type=F32)


def _dot_tn(a, b):
    return lax.dot_general(a, b, (((0,), (0,)), ((), ())), preferred_element_type=F32)


def _lower_bound(hgl, layer):
    m = jnp.max(hgl, axis=0, keepdims=True)
    e = jnp.exp(hgl - m)
    p = e / jnp.sum(e, axis=0, keepdims=True)
    cum = p[0:1]
    for j in range(1, layer + 1):
        cum = cum + p[j:j + 1]
    return cum - p[0:1]


def _forget_key(fa, lb):
    e = jnp.exp(-jnp.abs(fa))
    r = 1.0 / (1.0 + e)
    er = e * r
    pos = fa >= 0.0
    sig_p = jnp.where(pos, r, er)
    sig_n = jnp.where(pos, er, r)
    f = jnp.maximum(lb, LB_FLOOR) + (1.0 - lb) * sig_p
    k = (1.0 - lb) * sig_n
    return f, k


def _lru_coeffs(xc, gates, ba, bx, sp):
    tr = jnp.tanh(0.5 * gates[:, :LRU_BS] + 0.5 * ba)
    ti = jnp.tanh(0.5 * gates[:, LRU_BS:] + 0.5 * bx)
    c1 = (-0.5 * LRU_C) * sp
    log_a = c1 * tr + c1
    a = jnp.exp(log_a)
    th = jnp.tanh(log_a)
    one_m_a2 = (-2.0 * th) / (1.0 - th)
    root = one_m_a2 * lax.rsqrt(jnp.maximum(one_m_a2, F32_TINY))
    return a, root * _half_tanh_gate(ti, xc)


def _mod_kernel(cp_ref, cs_ref, w_ref, b_ref, op_ref, os_ref):
    w = w_ref[0].astype(BF16)
    op_ref[0] = _dot(_silu(cp_ref[...]).astype(BF16), w) + b_ref[0]
    os_ref[0] = _dot(_silu(cs_ref[...]).astype(BF16), w) + b_ref[0]


def _mod_call(c_prompt, c_sample, w_mod, b_mod):
    n_p, n_s = c_prompt.shape[0], c_sample.shape[0]
    tn = 1536
    return pl.pallas_call(
        _mod_kernel,
        out_shape=(jax.ShapeDtypeStruct((DEPTH, n_p, 6 * D_MODEL), F32),
                   jax.ShapeDtypeStruct((DEPTH, n_s, 6 * D_MODEL), F32)),
        grid=(DEPTH, 6 * D_MODEL // tn),
        in_specs=[
            pl.BlockSpec((n_p, D_MODEL), lambda l, j: (0, 0)),
            pl.BlockSpec((n_s, D_MODEL), lambda l, j: (0, 0)),
            pl.BlockSpec((1, D_MODEL, tn), lambda l, j: (l, 0, j)),
            pl.BlockSpec((1, 1, tn), lambda l, j: (l, 0, j)),
        ],
        out_specs=(pl.BlockSpec((1, n_p, tn), lambda l, j: (l, 0, j)),
                   pl.BlockSpec((1, n_s, tn), lambda l, j: (l, 0, j))),
        compiler_params=pltpu.CompilerParams(
            dimension_semantics=("arbitrary", "arbitrary"), vmem_limit_bytes=VMEM_LIMIT),
        name="mod",
    )(c_prompt, c_sample, w_mod, b_mod.reshape(DEPTH, 1, 6 * D_MODEL))


def _hg_tables():
    t = np.arange(HG_CHUNK)
    x = t[:, None] ^ t[None, :]
    hb = np.floor(np.log2(np.maximum(x, 1))).astype(np.int32)
    lvl = np.where(t[:, None] > t[None, :], hb + 1, np.where(t[:, None] == t[None, :], 0, -1)).astype(np.int32)
    return jnp.asarray(lvl)


def _x_within_group(q, k, g2, level):
    n = q.shape[0]
    shp = (n // SUBLANES, SUBLANES, LANES)
    q3, k3, g3 = q.reshape(shp), k.reshape(shp), g2.reshape(shp)
    sub = lax.broadcasted_iota(jnp.int32, (1, SUBLANES, LANES), 1)
    qside = ((sub >> level) & 1) == 1
    if level == 0:
        scaled = jnp.where(qside, q3 * jnp.exp2(g3 - pltpu.roll(g3, 1, axis=1)), k3)
        return scaled.reshape(n, LANES)

    def row(i):
        return jnp.broadcast_to(g3[:, i:i + 1, :], shp)

    piv = jnp.where(sub < 4, row(1), row(5)) if level == 1 else row(3)
    e = jnp.exp2(-jnp.abs(g3 - piv))
    return (jnp.where(qside, q3, k3) * e).reshape(n, LANES)


def _x_across_groups(q, k, g2, level):
    n = q.shape[0]
    b = 1 << level
    blocks = []
    for base in range(0, n, 2 * b):
        lo, mid, hi = base, base + b, base + 2 * b
        gp = g2[mid - 1:mid, :]
        blocks.append(k[lo:mid] * jnp.exp2(gp - g2[lo:mid]))
        blocks.append(q[mid:hi] * jnp.exp2(g2[mid:hi] - gp))
    return jnp.concatenate(blocks, axis=0)


HG_FULL_LEVELS = 4


def _hg_level_products(q, k, g2):
    c = HG_CHUNK
    prods = [_dot_nt(q.astype(BF16), k.astype(BF16))]
    for level in range(HG_LEVELS):
        b = 1 << level
        if b < SUBLANES:
            x = _x_within_group(q, k, g2, level).astype(BF16)
        else:
            x = _x_across_groups(q, k, g2, level).astype(BF16)
        if level < HG_FULL_LEVELS:
            lhs = x
        else:
            lhs = jnp.concatenate([x[base + b:base + 2 * b] for base in range(0, c, 2 * b)], axis=0)
        prods.append(_dot_nt(lhs, x))
    return prods


def _hg_assemble(prods_per_head, lvl_ref):
    c = HG_CHUNK
    n_heads = len(prods_per_head)
    groups = [[] for _ in range(n_heads)]
    for gi in range(c // SUBLANES):
        r0 = gi * SUBLANES
        rows = slice(r0, r0 + SUBLANES)
        lv = lvl_ref[rows, :]
        on_diag = lv == 0
        a_g = [jnp.where(on_diag, p[0][rows], 0.0) for p in prods_per_head]
        for level in range(HG_LEVELS):
            b = 1 << level
            if level < HG_FULL_LEVELS:
                src = r0
            elif r0 & b:
                src = (r0 >> (level + 1)) * b + (r0 & (b - 1))
            else:
                continue
            at_level = lv == level + 1
            a_g = [jnp.where(at_level, p[level + 1][src:src + SUBLANES], a)
                   for p, a in zip(prods_per_head, a_g)]
        for h in range(n_heads):
            groups[h].append(a_g[h])
    return [jnp.concatenate(g, axis=0).astype(BF16) for g in groups]


def _hg_finish(q, k, v, g2, st, a):
    c = HG_CHUNK
    g_last = g2[c - 1:c, :]
    qg = (q * jnp.exp2(g2)).astype(BF16)
    vb = v.astype(BF16)
    o = _dot_nt(qg, st.astype(BF16)) + _dot(a, vb)
    kd = (k * jnp.exp2(g_last - g2)).astype(BF16)
    st_new = st * jnp.exp2(g_last) + _dot_tn(vb, kd)
    return o, st_new


def _causal_conv(xb, prev8, cw_ref, cb_ref):
    t, w = xb.shape
    x3 = xb.reshape(t // SUBLANES, SUBLANES, w)
    sub = lax.broadcasted_iota(jnp.int32, (1, SUBLANES, w), 1)

    def tap(j):
        return cw_ref[j:j + 1, :][None]

    out = cb_ref[...][None] + tap(CONV_W - 1) * x3
    for back in range(1, CONV_W):
        r = pltpu.roll(x3, back, axis=1)
        r_prev_group = jnp.concatenate([pltpu.roll(prev8, back, axis=0)[None], r[:-1]], axis=0)
        shifted = jnp.where(sub >= back, r, r_prev_group)
        out = out + tap(CONV_W - 1 - back) * shifted
    return out.reshape(t, w)


def _chunk_cumsum(y):
    t = y.shape[0]
    y3 = y.reshape(t // SUBLANES, SUBLANES, LANES)
    sub = lax.broadcasted_iota(jnp.int32, (1, SUBLANES, LANES), 1)
    d = 1
    while d < SUBLANES:
        y3 = y3 + jnp.where(sub >= d, pltpu.roll(y3, d, axis=1), 0.0)
        d *= 2
    per_chunk = HG_CHUNK // SUBLANES
    out = []
    for gi in range(t // SUBLANES):
        blk = y3[gi]
        if gi % per_chunk:
            blk = blk + out[-1][SUBLANES - 1:SUBLANES, :]
        out.append(blk)
    return jnp.concatenate(out, axis=0)


def _lru_scan(a, b, h0):
    t = a.shape[0]
    a3 = a.reshape(t // SUBLANES, SUBLANES, LANES)
    b3 = b.reshape(t // SUBLANES, SUBLANES, LANES)
    sub = lax.broadcasted_iota(jnp.int32, a3.shape, 1)
    d = 1
    while d < SUBLANES:
        m = sub >= d
        b3 = a3 * jnp.where(m, pltpu.roll(b3, d, axis=1), 0.0) + b3
        a3 = a3 * jnp.where(m, pltpu.roll(a3, d, axis=1), 1.0)
        d *= 2
    hs = []
    h = h0
    for gidx in range(t // SUBLANES):
        hg = a3[gidx] * h + b3[gidx]
        hs.append(hg)
        h = hg[SUBLANES - 1:SUBLANES, :]
    return jnp.concatenate(hs, axis=0), h


def _prompt_mixer_kernel(layer, n_t,
                         x_ref, mod_ref, hgl_ref, n1g_ref, win_ref, ong_ref, cw_ref, cb_ref, wax_ref,
                         ba_ref, bx_ref, lam_ref, wba_ref, wbb_ref, wout_ref, lvl_ref,
                         xo_ref, hg_ref, lru_ref, conv_ref,
                         q_s, k_s, v_s, g_s, og_s, oa_s, ob_s, prev8_s, st_s, hl_s):
    tl = MIX_TL
    d = D_MODEL
    t_idx = pl.program_id(1)

    @pl.when(t_idx == 0)
    def _():
        st_s[...] = jnp.zeros(st_s.shape, F32)
        hl_s[...] = jnp.zeros(hl_s.shape, F32)
        prev8_s[...] = jnp.zeros(prev8_s.shape, F32)

    x = x_ref[0]
    mod = mod_ref[0]
    sh1, sc1, g1 = mod[:, 0:d], mod[:, d:2 * d], mod[:, 2 * d:3 * d]
    hb = (_rmsnorm(x, n1g_ref[...]) * (1.0 + sc1) + sh1).astype(BF16)

    def proj(j):
        return _dot(hb, win_ref[:, j * d:(j + 1) * d])

    def heads(val, dst):
        for h in range(HG_HEADS):
            dst[h] = val[:, h * LANES:(h + 1) * LANES]

    sp = _softplus(-lam_ref[...])
    ong = ong_ref[...]

    def level_products(c):
        rows = slice(c * HG_CHUNK, (c + 1) * HG_CHUNK)
        return [_hg_level_products(q_s[h, rows, :], k_s[h, rows, :], g_s[h, rows, :]) for h in range(HG_HEADS)]

    def finish(c, a_mats):
        rows = slice(c * HG_CHUNK, (c + 1) * HG_CHUNK)
        for h in range(HG_HEADS):
            o, st_new = _hg_finish(q_s[h, rows, :], k_s[h, rows, :], v_s[h, rows, :], g_s[h, rows, :],
                                   st_s[h], a_mats[h])
            st_s[h] = st_new
            oa_s[h, rows, :] = _rmsnorm(o, og_s[h, rows, :])

    xb = proj(4)
    xc = _causal_conv(xb, prev8_s[...], cw_ref, cb_ref)
    prev8_s[...] = xb[tl - SUBLANES:tl, :]
    conv_ref[0] = xb[tl - (CONV_W - 1):tl, :]
    yb = _gelu_tanh(proj(5))
    for n in range(LRU_BLOCKS):
        cols = slice(n * LRU_BS, (n + 1) * LRU_BS)
        xcn = xc[:, cols]
        gates = _dot(xcn.astype(BF16), wax_ref[n])
        a, bt = _lru_coeffs(xcn, gates, ba_ref[:, cols], bx_ref[:, cols], sp[:, cols])
        hseq, h_last = _lru_scan(a, bt, hl_s[:, cols])
        hl_s[:, cols] = h_last
        ob_s[:, cols] = hseq * yb[:, cols]
    lru_ref[0] = hl_s[...]

    lb = _lower_bound(hgl_ref[...], layer)
    f, k = _forget_key(proj(1), lb)
    heads(k, k_s)
    log2f = jnp.minimum(jnp.log2(f), 0.0)
    for h in range(HG_HEADS):
        g_s[h] = _chunk_cumsum(log2f[:, h * LANES:(h + 1) * LANES])
    heads(_silu(proj(0)), q_s)
    heads(proj(2), v_s)
    og = _silu(proj(3))
    for h in range(HG_HEADS):
        og_s[h] = og[:, h * LANES:(h + 1) * LANES] * ong
    for c in range(tl // HG_CHUNK):
        finish(c, _hg_assemble(level_products(c), lvl_ref))

    o_a = jnp.concatenate([oa_s[h] for h in range(HG_HEADS)], axis=1).astype(BF16)
    merged = (_half_tanh_gate(jnp.tanh(0.5 * proj(6)), _dot(o_a, wba_ref[...]))
              + _half_tanh_gate(jnp.tanh(0.5 * proj(7)), _dot(ob_s[...].astype(BF16), wbb_ref[...])))
    xo_ref[0] = x + g1 * _dot(merged.astype(BF16), wout_ref[...])

    @pl.when(t_idx == n_t - 1)
    def _():
        for h in range(HG_HEADS):
            hg_ref[0, h] = st_s[h].T


def _const_spec(shape, layer=None):
    nd = len(shape)
    if layer is None:
        return pl.BlockSpec(shape, lambda *_: (0,) * nd, pipeline_mode=pl.Buffered(1))
    return pl.BlockSpec((None,) + tuple(shape), lambda *_: (layer,) + (0,) * nd, pipeline_mode=pl.Buffered(1))


def _prompt_mixer_call(layer, x, mod_p, hg_lower, wl, w_in_bf16, lvl):
    bsz, seq, d = x.shape
    tl = MIX_TL
    n_t = seq // tl
    hshape = (HG_HEADS, tl, LANES)
    out_shapes = (
        jax.ShapeDtypeStruct((bsz, seq, d), F32),
        jax.ShapeDtypeStruct((bsz, HG_HEADS, HG_DK, HG_DV), F32),
        jax.ShapeDtypeStruct((bsz, 1, d), F32),
        jax.ShapeDtypeStruct((bsz, CONV_W - 1, d), F32),
    )
    in_specs = [
        pl.BlockSpec((1, tl, d), lambda b, t: (b, t, 0)),
        pl.BlockSpec((1, 1, 6 * d), lambda b, t: (b, 0, 0)),
        _const_spec((DEPTH, d)),
        _const_spec((1, d), layer),
        _const_spec((d, N_IN)),
        _const_spec((1, HG_DV), layer),
        _const_spec((CONV_W, d), layer),
        _const_spec((1, d), layer),
        _const_spec((LRU_BLOCKS, LRU_BS, 2 * LRU_BS), layer),
        _const_spec((1, d), layer),
        _const_spec((1, d), layer),
        _const_spec((1, d), layer),
        _const_spec((d, d), layer),
        _const_spec((d, d), layer),
        _const_spec((d, d), layer),
        _const_spec((HG_CHUNK, HG_CHUNK)),
    ]
    out_specs = (
        pl.BlockSpec((1, tl, d), lambda b, t: (b, t, 0)),
        pl.BlockSpec((1, HG_HEADS, HG_DK, HG_DV), lambda b, t: (b, 0, 0, 0)),
        pl.BlockSpec((1, 1, d), lambda b, t: (b, 0, 0)),
        pl.BlockSpec((1, CONV_W - 1, d), lambda b, t: (b, 0, 0)),
    )
    scratch = [
        pltpu.VMEM(hshape, F32),
        pltpu.VMEM(hshape, F32),
        pltpu.VMEM(hshape, F32),
        pltpu.VMEM(hshape, F32),
        pltpu.VMEM(hshape, F32),
        pltpu.VMEM(hshape, F32),
        pltpu.VMEM((tl, d), F32),
        pltpu.VMEM((SUBLANES, d), F32),
        pltpu.VMEM((HG_HEADS, HG_DV, HG_DK), F32),
        pltpu.VMEM((1, d), F32),
    ]
    return pl.pallas_call(
        functools.partial(_prompt_mixer_kernel, layer, n_t),
        out_shape=out_shapes,
        grid=(bsz, n_t),
        in_specs=in_specs,
        out_specs=out_specs,
        scratch_shapes=scratch,
        compiler_params=pltpu.CompilerParams(
            dimension_semantics=("arbitrary", "arbitrary"), vmem_limit_bytes=VMEM_LIMIT),
        name=f"prompt_mixer_{layer}",
    )(x, mod_p, hg_lower, wl["n1g"], w_in_bf16, wl["ong"], wl["cw"], wl["cb"], wl["wax"],
      wl["ba"], wl["bx"], wl["lam"], wl["wba"], wl["wbb"], wl["wout"], lvl)


def _ffn_kernel(final, x_ref, mod_ref, n2g_ref, w1_ref, w2_ref, fg_ref, o_ref):
    d = D_MODEL
    x = x_ref[0]
    mod = mod_ref[0]
    sh2, sc2, g2 = mod[:, 3 * d:4 * d], mod[:, 4 * d:5 * d], mod[:, 5 * d:6 * d]
    hb = (_rmsnorm(x, n2g_ref[...]) * (1.0 + sc2) + sh2).astype(BF16)
    acc = jnp.zeros(x.shape, F32)
    for c in range(D_FF // d):
        h1 = jnp.maximum(_dot(hb, w1_ref[:, c * d:(c + 1) * d].astype(BF16)), 0.0)
        acc = acc + _dot((h1 * h1).astype(BF16), w2_ref[c * d:(c + 1) * d, :].astype(BF16))
    xn = x + g2 * acc
    if final:
        xn = _rmsnorm(xn, fg_ref[...])
    o_ref[0] = xn


def _ffn_call(layer, x, mod, wl, final_g, tm, name, per_token_mod=False):
    final = layer == DEPTH - 1
    bsz, seq, d = x.shape
    if per_token_mod:
        assert bsz == 1 and mod.shape == (DEPTH, seq, 6 * d)
        mod_spec = pl.BlockSpec((1, tm, 6 * d), lambda b, t: (layer, t, 0))
    else:
        assert mod.shape == (bsz, 1, 6 * d)
        mod_spec = pl.BlockSpec((1, 1, 6 * d), lambda b, t: (b, 0, 0))
    return pl.pallas_call(
        functools.partial(_ffn_kernel, final),
        out_shape=jax.ShapeDtypeStruct((bsz, seq, d), F32),
        grid=(bsz, seq // tm),
        in_specs=[
            pl.BlockSpec((1, tm, d), lambda b, t: (b, t, 0)),
            mod_spec,
            _const_spec((1, d), layer),
            _const_spec((d, D_FF), layer),
            _const_spec((D_FF, d), layer),
            _const_spec((1, d)),
        ],
        out_specs=pl.BlockSpec((1, tm, d), lambda b, t: (b, t, 0)),
        compiler_params=pltpu.CompilerParams(
            dimension_semantics=("arbitrary", "arbitrary"), vmem_limit_bytes=VMEM_LIMIT),
        name=name,
    )(x, mod, wl["n2g"], wl["w1"], wl["w2"], final_g)


def _sample_inproj_kernel(x_ref, mod_ref, n1g_ref, w_ref, o_ref, wb_ref):
    d = D_MODEL
    mod = mod_ref[...]
    hb = (_rmsnorm(x_ref[...], n1g_ref[...]) * (1.0 + mod[:, d:2 * d]) + mod[:, 0:d]).astype(BF16)
    wb = w_ref[...].astype(BF16)
    wb_ref[...] = wb
    o_ref[...] = _dot(hb, wb)


def _sample_inproj_call(layer, x, mod_s, wl, w_in):
    n, d = x.shape
    return pl.pallas_call(
        _sample_inproj_kernel,
        out_shape=(jax.ShapeDtypeStruct((n, N_IN), F32), jax.ShapeDtypeStruct((d, N_IN), BF16)),
        grid=(N_IN // d,),
        in_specs=[
            pl.BlockSpec((n, d), lambda j: (0, 0)),
            pl.BlockSpec((None, n, 6 * d), lambda j: (layer, 0, 0)),
            pl.BlockSpec((None, 1, d), lambda j: (layer, 0, 0)),
            pl.BlockSpec((None, d, d), lambda j: (layer, 0, j)),
        ],
        out_specs=(pl.BlockSpec((n, d), lambda j: (0, j)), pl.BlockSpec((d, d), lambda j: (0, j))),
        compiler_params=pltpu.CompilerParams(
            dimension_semantics=("arbitrary",), vmem_limit_bytes=VMEM_LIMIT),
        name=f"sample_inproj_{layer}",
    )(x, mod_s, wl["n1g"], w_in)


N_SAMPLE_MIXER_INPUTS = 20

OUTER_ROWS = 16
OUTER_LEFT_SRC = (0, 0, 1, 0, 2, 1, 3, 4, 5)
OUTER_RIGHT_SRC = (0, 1, 0, 2, 0, 1)


def _outer_tables():
    g = SAMPLE_GROUP
    sel_l = np.zeros((g * OUTER_ROWS, 6 * g), np.float32)
    sel_r = np.zeros((g * OUTER_ROWS, 4 * g), np.float32)
    for j in range(g):
        for r, a in enumerate(OUTER_LEFT_SRC):
            sel_l[OUTER_ROWS * j + r, g * a + j] = 1.0
        for r, a in enumerate(OUTER_RIGHT_SRC):
            sel_r[OUTER_ROWS * j + r, g * a + j] = 1.0
    ones_rows = np.zeros((g, OUTER_ROWS, LANES), np.float32)
    ones_rows[:, len(OUTER_RIGHT_SRC):len(OUTER_LEFT_SRC)] = 1.0
    return (jnp.asarray(sel_l, BF16), jnp.asarray(sel_r, BF16),
            jnp.asarray(ones_rows.reshape(g * OUTER_ROWS, LANES), BF16))


def _split3(x):
    x1 = x.astype(BF16).astype(F32)
    r = x - x1
    x2 = r.astype(BF16).astype(F32)
    x3 = (r - x2).astype(BF16).astype(F32)
    return [x1, x2, x3]


def _sample_mixer_kernel(layer, n_groups, *refs):
    refs = refs[:N_SAMPLE_MIXER_INPUTS] + refs[len(refs) - 10:]
    (u_ref, x_ref, mod_ref, hg_ref, lru_ref, conv_ref, hgl_ref, ong_ref, cw_ref, cb_ref,
     wax_ref, ba_ref, bx_ref, lam_ref, wba_ref, wbb_ref, wout_ref, sell_ref, selr_ref, ones_ref,
     xo_ref, hgo_ref, lruo_ref, convo_ref,
     f_s, k_s, q_s, v_s, o_s, ob_s) = refs
    d = D_MODEL
    n = x_ref.shape[0]
    i = pl.program_id(0)
    if layer == 0:
        for other in range(1, DEPTH):
            hgo_ref[other] = jnp.zeros(hgo_ref.shape[1:], F32)
        hgo_ref = hgo_ref.at[0]

    @pl.when(i == 0)
    def _():
        q_s[...] = _silu(u_ref[:, 0:d])
        lb = _lower_bound(hgl_ref[...], layer)
        f, k = _forget_key(u_ref[:, d:2 * d], lb)
        f_s[...] = f
        k_s[...] = k
        v_s[...] = u_ref[:, 2 * d:3 * d]
        xb = u_ref[:, 4 * d:5 * d]
        cw = cw_ref[...]
        xc = cb_ref[...] + cw[3:4] * xb
        for j in range(CONV_W - 1):
            xc = xc + cw[j:j + 1] * conv_ref[j]
        convo_ref[0] = conv_ref[1]
        convo_ref[1] = conv_ref[2]
        convo_ref[2] = xb
        yb = _gelu_tanh(u_ref[:, 5 * d:6 * d])
        sp = _softplus(-lam_ref[...])
        for blk in range(LRU_BLOCKS):
            cols = slice(blk * LRU_BS, (blk + 1) * LRU_BS)
            xcn = xc[:, cols]
            gates = _dot(xcn.astype(BF16), wax_ref[blk])
            a, bt = _lru_coeffs(xcn, gates, ba_ref[:, cols], bx_ref[:, cols], sp[:, cols])
            hn = a * lru_ref[:, cols] + bt
            lruo_ref[:, cols] = hn
            ob_s[:, cols] = hn * yb[:, cols]

    rows = pl.ds(pl.multiple_of(i * SAMPLE_GROUP, SAMPLE_GROUP), SAMPLE_GROUP)
    ones_rows = ones_ref[...]

    def outer_products(h):
        cols = slice(h * LANES, (h + 1) * LANES)
        vparts = _split3(v_s[rows, cols])
        left = jnp.concatenate(_split3(k_s[rows, cols]) + _split3(f_s[rows, cols]), axis=0).astype(BF16)
        right = jnp.concatenate(vparts + [jnp.zeros_like(vparts[0])], axis=0).astype(BF16)
        left_t = _dot(sell_ref[...], left).T.astype(BF16)
        right_rows = jnp.concatenate([_dot(selr_ref[...], right).astype(BF16), ones_rows], axis=1)
        out = []
        for j in range(SAMPLE_GROUP):
            lo, hi = j * OUTER_ROWS, (j + 1) * OUTER_ROWS
            pieces = [right_rows[lo:hi]]
            if lo:
                pieces.insert(0, jnp.zeros((lo, 2 * HG_DV), BF16))
            if hi < SAMPLE_GROUP * OUTER_ROWS:
                pieces.append(jnp.zeros((SAMPLE_GROUP * OUTER_ROWS - hi, 2 * HG_DV), BF16))
            out.append(_dot(left_t, jnp.concatenate(pieces, axis=0)))
        return out

    kv_f = outer_products(0)
    for h in range(HG_HEADS):
        cols = slice(h * LANES, (h + 1) * LANES)
        kv_f_next = outer_products(h + 1) if h + 1 < HG_HEADS else None
        s_new = [hg_ref[j, h] * kv_f[j][:, HG_DV:] + kv_f[j][:, :HG_DV] for j in range(SAMPLE_GROUP)]
        for j in range(SAMPLE_GROUP):
            hgo_ref[j, h] = s_new[j]
        qg = q_s[rows, cols].astype(BF16)
        o_rows = [_dot(qg, s_new[j].astype(BF16))[j:j + 1, :] for j in range(SAMPLE_GROUP)]
        o_s[rows, cols] = jnp.concatenate(o_rows, axis=0)
        kv_f = kv_f_next

    @pl.when(i == n_groups - 1)
    def _():
        ong = ong_ref[...]
        parts = []
        for h in range(HG_HEADS):
            cols = slice(h * LANES, (h + 1) * LANES)
            parts.append(_rmsnorm(o_s[:, cols], ong) * _silu(u_ref[:, 3 * d + h * LANES:3 * d + (h + 1) * LANES]))
        o_a = jnp.concatenate(parts, axis=1).astype(BF16)
        merged = (_sigmoid(u_ref[:, 6 * d:7 * d]) * _dot(o_a, wba_ref[...])
                  + _sigmoid(u_ref[:, 7 * d:8 * d]) * _dot(ob_s[...].astype(BF16), wbb_ref[...]))
        g1 = mod_ref[:, 2 * d:3 * d]
        xo_ref[...] = x_ref[...] + g1 * _dot(merged.astype(BF16), wout_ref[...])


def _sample_mixer_call(layer, u, x, mod_s, st_hg_all, st_lru_all, st_conv_t, hg_lower, wl, hg_out_prev):
    n, d = x.shape
    grp = SAMPLE_GROUP
    n_groups = n // grp

    def full(shape):
        nd = len(shape)
        return pl.BlockSpec(shape, lambda i: (0,) * nd)

    def of_layer(shape):
        nd = len(shape)
        return pl.BlockSpec((None,) + tuple(shape), lambda i: (layer,) + (0,) * nd)

    st_spec = pl.BlockSpec((None, grp, HG_HEADS, HG_DK, HG_DV), lambda i: (layer, i, 0, 0, 0))
    out_shapes = (
        jax.ShapeDtypeStruct((n, d), F32),
        jax.ShapeDtypeStruct((DEPTH, n, HG_HEADS, HG_DK, HG_DV), F32),
        jax.ShapeDtypeStruct((n, d), F32),
        jax.ShapeDtypeStruct((CONV_W - 1, n, d), F32),
    )
    in_specs = [
        full((n, N_IN)), full((n, d)), of_layer((n, 6 * d)), st_spec,
        of_layer((n, d)), full((CONV_W - 1, n, d)),
        full((DEPTH, d)), of_layer((1, HG_DV)), of_layer((CONV_W, d)), of_layer((1, d)),
        of_layer((LRU_BLOCKS, LRU_BS, 2 * LRU_BS)), of_layer((1, d)), of_layer((1, d)), of_layer((1, d)),
        of_layer((d, d)), of_layer((d, d)), of_layer((d, d)),
    ]
    outer_tables = _outer_tables()
    in_specs += [full(t.shape) for t in outer_tables]
    assert len(in_specs) == N_SAMPLE_MIXER_INPUTS
    args = [u, x, mod_s, st_hg_all, st_lru_all, st_conv_t, hg_lower, wl["ong"], wl["cw"], wl["cb"], wl["wax"],
            wl["ba"], wl["bx"], wl["lam"], wl["wba"], wl["wbb"], wl["wout"], *outer_tables]
    aliases = {}
    if hg_out_prev is not None:
        in_specs.append(pl.BlockSpec(memory_space=pl.ANY))
        args.append(hg_out_prev)
        aliases = {N_SAMPLE_MIXER_INPUTS: 1}
    st_out_spec = st_spec if layer else pl.BlockSpec((DEPTH, grp, HG_HEADS, HG_DK, HG_DV),
                                                     lambda i: (0, i, 0, 0, 0))
    out_specs = (full((n, d)), st_out_spec, full((n, d)), full((CONV_W - 1, n, d)))
    scratch = [
        pltpu.VMEM((n, d), F32),
        pltpu.VMEM((n, d), F32),
        pltpu.VMEM((n, d), F32),
        pltpu.VMEM((n, d), F32),
        pltpu.VMEM((n, d), F32),
        pltpu.VMEM((n, d), F32),
    ]
    return pl.pallas_call(
        functools.partial(_sample_mixer_kernel, layer, n_groups),
        out_shape=out_shapes,
        grid=(n_groups,),
        in_specs=in_specs,
        out_specs=out_specs,
        scratch_shapes=scratch,
        input_output_aliases=aliases,
        compiler_params=pltpu.CompilerParams(
            dimension_semantics=("arbitrary",), vmem_limit_bytes=VMEM_LIMIT),
        name=f"sample_mixer_{layer}",
    )(*args)


def _stacked_weights(norm1_g, norm2_g, hg_onorm_g, lru_conv_w, lru_conv_b, lru_wa, lru_ba, lru_wx,
                     lru_bx, lru_lambda, w_branch_a, w_branch_b, w_out, w_ff1, w_ff2):
    d = D_MODEL
    return {
        "n1g": norm1_g.reshape(DEPTH, 1, d),
        "n2g": norm2_g.reshape(DEPTH, 1, d),
        "ong": hg_onorm_g.reshape(DEPTH, 1, HG_DV),
        "cw": lru_conv_w,
        "cb": lru_conv_b.reshape(DEPTH, 1, d),
        "wax": jnp.concatenate([lru_wa, lru_wx], axis=-1).astype(BF16),
        "ba": lru_ba.reshape(DEPTH, 1, d),
        "bx": lru_bx.reshape(DEPTH, 1, d),
        "lam": lru_lambda.reshape(DEPTH, 1, d),
        "wba": w_branch_a.astype(BF16),
        "wbb": w_branch_b.astype(BF16),
        "wout": w_out.astype(BF16),
        "w1": w_ff1,
        "w2": w_ff2,
    }


def kernel(x_prompt, x_sample, state_hgrn, state_rglru, state_conv, c_prompt, c_sample, w_mod, b_mod, norm1_g, norm2_g, w_in, hg_lower, hg_onorm_g, lru_conv_w, lru_conv_b, lru_wa, lru_ba, lru_wx, lru_bx, lru_lambda, w_branch_a, w_branch_b, w_out, w_ff1, w_ff2, final_norm_g):
    bsz, seq, d = x_prompt.shape
    n_s = x_sample.shape[0]
    assert d == D_MODEL and x_sample.shape[1] == 1 and seq % MIX_TL == 0 and seq % FFN_TM == 0
    assert n_s % SAMPLE_GROUP == 0 and n_s == LANES

    mod_p_all, mod_s = _mod_call(c_prompt, c_sample, w_mod, b_mod)
    lvl = _hg_tables()
    final_g = final_norm_g.reshape(1, d)

    xp = x_prompt
    xs = x_sample.reshape(n_s, d)
    hg_p, lru_p, conv_p, lru_s, conv_s = [], [], [], [], []
    hg_s = None
    wl = _stacked_weights(norm1_g, norm2_g, hg_onorm_g, lru_conv_w, lru_conv_b, lru_wa, lru_ba,
                          lru_wx, lru_bx, lru_lambda, w_branch_a, w_branch_b, w_out, w_ff1, w_ff2)
    for l in range(DEPTH):
        mod_p = mod_p_all[l].reshape(bsz, 1, 6 * d)

        u, w_in_bf16 = _sample_inproj_call(l, xs, mod_s, wl, w_in)

        xp, hg, lru, conv = _prompt_mixer_call(l, xp, mod_p, hg_lower, wl, w_in_bf16, lvl)
        xp = _ffn_call(l, xp, mod_p, wl, final_g, FFN_TM, f"prompt_ffn_{l}")
        hg_p.append(hg)
        lru_p.append(lru.reshape(bsz, d))
        conv_p.append(conv)

        xs, hg_s, lru, conv = _sample_mixer_call(l, u, xs, mod_s, state_hgrn, state_rglru,
                                                 jnp.swapaxes(state_conv[l], 0, 1), hg_lower, wl, hg_s)
        xs = _ffn_call(l, xs.reshape(1, n_s, d), mod_s, wl, final_g, n_s, f"sample_ffn_{l}",
                       per_token_mod=True).reshape(n_s, d)
        lru_s.append(lru)
        conv_s.append(jnp.swapaxes(conv, 0, 1))

    return (xp, xs.reshape(n_s, 1, d),
            jnp.stack(hg_p), jnp.stack(lru_p), jnp.stack(conv_p),
            hg_s, jnp.stack(lru_s), jnp.stack(conv_s))
```

```python
import functools

import numpy as np
import jax
import jax.numpy as jnp
from jax import lax
from jax.experimental import pallas as pl
from jax.experimental.pallas import tpu as pltpu

F32 = jnp.float32
BF16 = jnp.bfloat16

D_MODEL = 1024
DEPTH = 2
HG_HEADS = 8
HG_DK = 128
HG_DV = 128
LB_FLOOR = 1e-30
LRU_BLOCKS = 8
LRU_BS = 128
LRU_C = 8.0
CONV_W = 4
D_FF = 4 * D_MODEL
EPS = 1e-6
N_IN = 8 * D_MODEL
F32_TINY = float(np.finfo(np.float32).tiny)

LANES = 128
SUBLANES = 8
HG_CHUNK = 128
HG_LEVELS = 7
MIX_TL = 256
FFN_TM = 512
SAMPLE_GROUP = 8
VMEM_LIMIT = 56 * 1024 * 1024


def _sigmoid(x):
    return 0.5 * jnp.tanh(0.5 * x) + 0.5


def _silu(x):
    h = 0.5 * x
    return h * jnp.tanh(h) + h


def _gelu_tanh(x):
    c = 0.7978845608028654
    h = 0.5 * x
    return h * jnp.tanh(x * (c + (c * 0.044715) * (x * x))) + h


def _half_tanh_gate(t, y):
    h = 0.5 * y
    return h * t + h


def _log1p(x):
    u = 1.0 + x
    return jnp.where(u == 1.0, x, jnp.log(u) * (x / (u - 1.0)))


def _softplus(z):
    return jnp.maximum(z, 0.0) + _log1p(jnp.exp(-jnp.abs(z)))


def _rmsnorm(x, g):
    ms = jnp.mean(x * x, axis=-1, keepdims=True)
    return x * lax.rsqrt(ms + EPS) * g


def _dot(a, b):
    return jnp.dot(a, b, preferred_element_type=F32)


def _dot_nt(a, b):
    return lax.dot_general(a, b, (((1,), (1,)), ((), ())), preferred_element_type=F32)


def _dot_tn(a, b):
    return lax.dot_general(a, b, (((0,), (0,)), ((), ())), preferred_element_type=F32)


def _lower_bound(hgl, layer):
    m = jnp.max(hgl, axis=0, keepdims=True)
    e = jnp.exp(hgl - m)
    p = e / jnp.sum(e, axis=0, keepdims=True)
    cum = p[0:1]
    for j in range(1, layer + 1):
        cum = cum + p[j:j + 1]
    return cum - p[0:1]


def _forget_key(fa, lb):
    e = jnp.exp(-jnp.abs(fa))
    r = 1.0 / (1.0 + e)
    er = e * r
    pos = fa >= 0.0
    sig_p = jnp.where(pos, r, er)
    sig_n = jnp.where(pos, er, r)
    f = jnp.maximum(lb, LB_FLOOR) + (1.0 - lb) * sig_p
    k = (1.0 - lb) * sig_n
    return f, k


def _lru_coeffs(xc, gates, ba, bx, sp):
    tr = jnp.tanh(0.5 * gates[:, :LRU_BS] + 0.5 * ba)
    ti = jnp.tanh(0.5 * gates[:, LRU_BS:] + 0.5 * bx)
    c1 = (-0.5 * LRU_C) * sp
    log_a = c1 * tr + c1
    a = jnp.exp(log_a)
    th = jnp.tanh(log_a)
    one_m_a2 = (-2.0 * th) / (1.0 - th)
    root = one_m_a2 * lax.rsqrt(jnp.maximum(one_m_a2, F32_TINY))
    return a, root * _half_tanh_gate(ti, xc)


def _mod_kernel(cp_ref, cs_ref, w_ref, b_ref, op_ref, os_ref):
    w = w_ref[0].astype(BF16)
    op_ref[0] = _dot(_silu(cp_ref[...]).astype(BF16), w) + b_ref[0]
    os_ref[0] = _dot(_silu(cs_ref[...]).astype(BF16), w) + b_ref[0]


def _mod_call(c_prompt, c_sample, w_mod, b_mod):
    n_p, n_s = c_prompt.shape[0], c_sample.shape[0]
    tn = 1536
    return pl.pallas_call(
        _mod_kernel,
        out_shape=(jax.ShapeDtypeStruct((DEPTH, n_p, 6 * D_MODEL), F32),
                   jax.ShapeDtypeStruct((DEPTH, n_s, 6 * D_MODEL), F32)),
        grid=(DEPTH, 6 * D_MODEL // tn),
        in_specs=[
            pl.BlockSpec((n_p, D_MODEL), lambda l, j: (0, 0)),
            pl.BlockSpec((n_s, D_MODEL), lambda l, j: (0, 0)),
            pl.BlockSpec((1, D_MODEL, tn), lambda l, j: (l, 0, j)),
            pl.BlockSpec((1, 1, tn), lambda l, j: (l, 0, j)),
        ],
        out_specs=(pl.BlockSpec((1, n_p, tn), lambda l, j: (l, 0, j)),
                   pl.BlockSpec((1, n_s, tn), lambda l, j: (l, 0, j))),
        compiler_params=pltpu.CompilerParams(
            dimension_semantics=("arbitrary", "arbitrary"), vmem_limit_bytes=VMEM_LIMIT),
        name="mod",
    )(c_prompt, c_sample, w_mod, b_mod.reshape(DEPTH, 1, 6 * D_MODEL))


def _hg_tables():
    t = np.arange(HG_CHUNK)
    x = t[:, None] ^ t[None, :]
    hb = np.floor(np.log2(np.maximum(x, 1))).astype(np.int32)
    lvl = np.where(t[:, None] > t[None, :], hb + 1, np.where(t[:, None] == t[None, :], 0, -1)).astype(np.int32)
    return jnp.asarray(lvl)


def _x_within_group(q, k, g2, level):
    n = q.shape[0]
    shp = (n // SUBLANES, SUBLANES, LANES)
    q3, k3, g3 = q.reshape(shp), k.reshape(shp), g2.reshape(shp)
    sub = lax.broadcasted_iota(jnp.int32, (1, SUBLANES, LANES), 1)
    qside = ((sub >> level) & 1) == 1
    if level == 0:
        scaled = jnp.where(qside, q3 * jnp.exp2(g3 - pltpu.roll(g3, 1, axis=1)), k3)
        return scaled.reshape(n, LANES)

    def row(i):
        return jnp.broadcast_to(g3[:, i:i + 1, :], shp)

    piv = jnp.where(sub < 4, row(1), row(5)) if level == 1 else row(3)
    e = jnp.exp2(-jnp.abs(g3 - piv))
    return (jnp.where(qside, q3, k3) * e).reshape(n, LANES)


def _x_across_groups(q, k, g2, level):
    n = q.shape[0]
    b = 1 << level
    blocks = []
    for base in range(0, n, 2 * b):
        lo, mid, hi = base, base + b, base + 2 * b
        gp = g2[mid - 1:mid, :]
        blocks.append(k[lo:mid] * jnp.exp2(gp - g2[lo:mid]))
        blocks.append(q[mid:hi] * jnp.exp2(g2[mid:hi] - gp))
    return jnp.concatenate(blocks, axis=0)


HG_FULL_LEVELS = 4


def _hg_level_products(q, k, g2):
    c = HG_CHUNK
    prods = [_dot_nt(q.astype(BF16), k.astype(BF16))]
    for level in range(HG_LEVELS):
        b = 1 << level
        if b < SUBLANES:
            x = _x_within_group(q, k, g2, level).astype(BF16)
        else:
            x = _x_across_groups(q, k, g2, level).astype(BF16)
        if level < HG_FULL_LEVELS:
            lhs = x
        else:
            lhs = jnp.concatenate([x[base + b:base + 2 * b] for base in range(0, c, 2 * b)], axis=0)
        prods.append(_dot_nt(lhs, x))
    return prods


def _hg_assemble(prods_per_head, lvl_ref):
    c = HG_CHUNK
    n_heads = len(prods_per_head)
    groups = [[] for _ in range(n_heads)]
    for gi in range(c // SUBLANES):
        r0 = gi * SUBLANES
        rows = slice(r0, r0 + SUBLANES)
        lv = lvl_ref[rows, :]
        on_diag = lv == 0
        a_g = [jnp.where(on_diag, p[0][rows], 0.0) for p in prods_per_head]
        for level in range(HG_LEVELS):
            b = 1 << level
            if level < HG_FULL_LEVELS:
                src = r0
            elif r0 & b:
                src = (r0 >> (level + 1)) * b + (r0 & (b - 1))
            else:
                continue
            at_level = lv == level + 1
            a_g = [jnp.where(at_level, p[level + 1][src:src + SUBLANES], a)
                   for p, a in zip(prods_per_head, a_g)]
        for h in range(n_heads):
            groups[h].append(a_g[h])
    return [jnp.concatenate(g, axis=0).astype(BF16) for g in groups]


def _hg_finish(q, k, v, g2, st, a):
    c = HG_CHUNK
    g_last = g2[c - 1:c, :]
    qg = (q * jnp.exp2(g2)).astype(BF16)
    vb = v.astype(BF16)
    o = _dot_nt(qg, st.astype(BF16)) + _dot(a, vb)
    kd = (k * jnp.exp2(g_last - g2)).astype(BF16)
    st_new = st * jnp.exp2(g_last) + _dot_tn(vb, kd)
    return o, st_new


def _causal_conv(xb, prev8, cw_ref, cb_ref):
    t, w = xb.shape
    x3 = xb.reshape(t // SUBLANES, SUBLANES, w)
    sub = lax.broadcasted_iota(jnp.int32, (1, SUBLANES, w), 1)

    def tap(j):
        return cw_ref[j:j + 1, :][None]

    out = cb_ref[...][None] + tap(CONV_W - 1) * x3
    for back in range(1, CONV_W):
        r = pltpu.roll(x3, back, axis=1)
        r_prev_group = jnp.concatenate([pltpu.roll(prev8, back, axis=0)[None], r[:-1]], axis=0)
        shifted = jnp.where(sub >= back, r, r_prev_group)
        out = out + tap(CONV_W - 1 - back) * shifted
    return out.reshape(t, w)


def _chunk_cumsum(y):
    t = y.shape[0]
    y3 = y.reshape(t // SUBLANES, SUBLANES, LANES)
    sub = lax.broadcasted_iota(jnp.int32, (1, SUBLANES, LANES), 1)
    d = 1
    while d < SUBLANES:
        y3 = y3 + jnp.where(sub >= d, pltpu.roll(y3, d, axis=1), 0.0)
        d *= 2
    per_chunk = HG_CHUNK // SUBLANES
    out = []
    for gi in range(t // SUBLANES):
        blk = y3[gi]
        if gi % per_chunk:
            blk = blk + out[-1][SUBLANES - 1:SUBLANES, :]
        out.append(blk)
    return jnp.concatenate(out, axis=0)


def _lru_scan(a, b, h0):
    t = a.shape[0]
    a3 = a.reshape(t // SUBLANES, SUBLANES, LANES)
    b3 = b.reshape(t // SUBLANES, SUBLANES, LANES)
    sub = lax.broadcasted_iota(jnp.int32, a3.shape, 1)
    d = 1
    while d < SUBLANES:
        m = sub >= d
        b3 = a3 * jnp.where(m, pltpu.roll(b3, d, axis=1), 0.0) + b3
        a3 = a3 * jnp.where(m, pltpu.roll(a3, d, axis=1), 1.0)
        d *= 2
    hs = []
    h = h0
    for gidx in range(t // SUBLANES):
        hg = a3[gidx] * h + b3[gidx]
        hs.append(hg)
        h = hg[SUBLANES - 1:SUBLANES, :]
    return jnp.concatenate(hs, axis=0), h


def _prompt_mixer_kernel(layer, n_t,
                         x_ref, mod_ref, hgl_ref, n1g_ref, win_ref, ong_ref, cw_ref, cb_ref, wax_ref,
                         ba_ref, bx_ref, lam_ref, wba_ref, wbb_ref, wout_ref, lvl_ref,
                         xo_ref, hg_ref, lru_ref, conv_ref,
                         q_s, k_s, v_s, g_s, og_s, oa_s, ob_s, prev8_s, st_s, hl_s, xc_s, yb_s, ga_s, gb_s, gr_s,
                         gi_s):
    tl = MIX_TL
    d = D_MODEL
    t_idx = pl.program_id(1)

    @pl.when(t_idx == 0)
    def _():
        st_s[...] = jnp.zeros(st_s.shape, F32)
        hl_s[...] = jnp.zeros(hl_s.shape, F32)
        prev8_s[...] = jnp.zeros(prev8_s.shape, F32)

    x = x_ref[0]
    mod = mod_ref[0]
    sh1, sc1, g1 = mod[:, 0:d], mod[:, d:2 * d], mod[:, 2 * d:3 * d]
    hb = (_rmsnorm(x, n1g_ref[...]) * (1.0 + sc1) + sh1).astype(BF16)

    def proj(j):
        return _dot(hb, win_ref[:, j * d:(j + 1) * d])

    def heads(val, dst):
        for h in range(HG_HEADS):
            dst[h] = val[:, h * LANES:(h + 1) * LANES]

    sp = _softplus(-lam_ref[...])
    ong = ong_ref[...]

    def lru_block(n):
        cols = slice(n * LRU_BS, (n + 1) * LRU_BS)
        gates = jnp.concatenate([gr_s[:, cols], gi_s[:, cols]], axis=1)
        a, bt = _lru_coeffs(xc_s[:, cols], gates, ba_ref[:, cols], bx_ref[:, cols], sp[:, cols])
        hseq, h_last = _lru_scan(a, bt, hl_s[:, cols])
        hl_s[:, cols] = h_last
        ob_s[:, cols] = hseq * yb_s[:, cols]

    def level_products(c):
        rows = slice(c * HG_CHUNK, (c + 1) * HG_CHUNK)
        return [_hg_level_products(q_s[h, rows, :], k_s[h, rows, :], g_s[h, rows, :]) for h in range(HG_HEADS)]

    def finish(c, a_mats):
        rows = slice(c * HG_CHUNK, (c + 1) * HG_CHUNK)
        for h in range(HG_HEADS):
            o, st_new = _hg_finish(q_s[h, rows, :], k_s[h, rows, :], v_s[h, rows, :], g_s[h, rows, :],
                                   st_s[h], a_mats[h])
            st_s[h] = st_new
            oa_s[h, rows, :] = _rmsnorm(o, og_s[h, rows, :])

    xb = proj(4)
    xc = _causal_conv(xb, prev8_s[...], cw_ref, cb_ref)
    xc_s[...] = xc
    prev8_s[...] = xb[tl - SUBLANES:tl, :]
    conv_ref[0] = xb[tl - (CONV_W - 1):tl, :]
    for n in range(LRU_BLOCKS):
        cols = slice(n * LRU_BS, (n + 1) * LRU_BS)
        gates = _dot(xc[:, cols].astype(BF16), wax_ref[n])
        gr_s[:, cols] = gates[:, :LRU_BS]
        gi_s[:, cols] = gates[:, LRU_BS:]
    yb_s[...] = _gelu_tanh(proj(5))
    lb = _lower_bound(hgl_ref[...], layer)
    f, k = _forget_key(proj(1), lb)
    heads(k, k_s)
    log2f = jnp.minimum(jnp.log2(f), 0.0)
    for h in range(HG_HEADS):
        g_s[h] = _chunk_cumsum(log2f[:, h * LANES:(h + 1) * LANES])
    lru_block(0)
    lru_block(1)
    heads(_silu(proj(0)), q_s)
    lru_block(2)
    lru_block(3)
    heads(proj(2), v_s)
    lru_block(4)
    lru_block(5)
    og = _silu(proj(3))
    for h in range(HG_HEADS):
        og_s[h] = og[:, h * LANES:(h + 1) * LANES] * ong
    lru_block(6)
    lru_block(7)
    lru_ref[0] = hl_s[...]
    ga_s[...] = jnp.tanh(0.5 * proj(6))
    gb_s[...] = jnp.tanh(0.5 * proj(7))
    for c in range(tl // HG_CHUNK):
        finish(c, _hg_assemble(level_products(c), lvl_ref))

    o_a = jnp.concatenate([oa_s[h] for h in range(HG_HEADS)], axis=1).astype(BF16)
    merged = (_half_tanh_gate(ga_s[...], _dot(o_a, wba_ref[...]))
              + _half_tanh_gate(gb_s[...], _dot(ob_s[...].astype(BF16), wbb_ref[...])))
    xo_ref[0] = x + g1 * _dot(merged.astype(BF16), wout_ref[...])

    @pl.when(t_idx == n_t - 1)
    def _():
        for h in range(HG_HEADS):
            hg_ref[0, h] = st_s[h].T


def _const_spec(shape, layer=None):
    nd = len(shape)
    if layer is None:
        return pl.BlockSpec(shape, lambda *_: (0,) * nd, pipeline_mode=pl.Buffered(1))
    return pl.BlockSpec((None,) + tuple(shape), lambda *_: (layer,) + (0,) * nd, pipeline_mode=pl.Buffered(1))


def _prompt_mixer_call(layer, x, mod_p, hg_lower, wl, w_in_bf16, lvl):
    bsz, seq, d = x.shape
    tl = MIX_TL
    n_t = seq // tl
    hshape = (HG_HEADS, tl, LANES)
    out_shapes = (
        jax.ShapeDtypeStruct((bsz, seq, d), F32),
        jax.ShapeDtypeStruct((bsz, HG_HEADS, HG_DK, HG_DV), F32),
        jax.ShapeDtypeStruct((bsz, 1, d), F32),
        jax.ShapeDtypeStruct((bsz, CONV_W - 1, d), F32),
    )
    in_specs = [
        pl.BlockSpec((1, tl, d), lambda b, t: (b, t, 0)),
        pl.BlockSpec((1, 1, 6 * d), lambda b, t: (b, 0, 0)),
        _const_spec((DEPTH, d)),
        _const_spec((1, d), layer),
        _const_spec((d, N_IN)),
        _const_spec((1, HG_DV), layer),
        _const_spec((CONV_W, d), layer),
        _const_spec((1, d), layer),
        _const_spec((LRU_BLOCKS, LRU_BS, 2 * LRU_BS), layer),
        _const_spec((1, d), layer),
        _const_spec((1, d), layer),
        _const_spec((1, d), layer),
        _const_spec((d, d), layer),
        _const_spec((d, d), layer),
        _const_spec((d, d), layer),
        _const_spec((HG_CHUNK, HG_CHUNK)),
    ]
    out_specs = (
        pl.BlockSpec((1, tl, d), lambda b, t: (b, t, 0)),
        pl.BlockSpec((1, HG_HEADS, HG_DK, HG_DV), lambda b, t: (b, 0, 0, 0)),
        pl.BlockSpec((1, 1, d), lambda b, t: (b, 0, 0)),
        pl.BlockSpec((1, CONV_W - 1, d), lambda b, t: (b, 0, 0)),
    )
    scratch = [
        pltpu.VMEM(hshape, F32),
        pltpu.VMEM(hshape, F32),
        pltpu.VMEM(hshape, F32),
        pltpu.VMEM(hshape, F32),
        pltpu.VMEM(hshape, F32),
        pltpu.VMEM(hshape, F32),
        pltpu.VMEM((tl, d), F32),
        pltpu.VMEM((SUBLANES, d), F32),
        pltpu.VMEM((HG_HEADS, HG_DV, HG_DK), F32),
        pltpu.VMEM((1, d), F32),
        pltpu.VMEM((tl, d), F32),
        pltpu.VMEM((tl, d), F32),
        pltpu.VMEM((tl, d), F32),
        pltpu.VMEM((tl, d), F32),
        pltpu.VMEM((tl, d), F32),
        pltpu.VMEM((tl, d), F32),
    ]
    return pl.pallas_call(
        functools.partial(_prompt_mixer_kernel, layer, n_t),
        out_shape=out_shapes,
        grid=(bsz, n_t),
        in_specs=in_specs,
        out_specs=out_specs,
        scratch_shapes=scratch,
        compiler_params=pltpu.CompilerParams(
            dimension_semantics=("arbitrary", "arbitrary"), vmem_limit_bytes=VMEM_LIMIT),
        name=f"prompt_mixer_{layer}",
    )(x, mod_p, hg_lower, wl["n1g"], w_in_bf16, wl["ong"], wl["cw"], wl["cb"], wl["wax"],
      wl["ba"], wl["bx"], wl["lam"], wl["wba"], wl["wbb"], wl["wout"], lvl)


def _ffn_kernel(final, x_ref, mod_ref, n2g_ref, w1_ref, w2_ref, fg_ref, o_ref):
    d = D_MODEL
    x = x_ref[0]
    mod = mod_ref[0]
    sh2, sc2, g2 = mod[:, 3 * d:4 * d], mod[:, 4 * d:5 * d], mod[:, 5 * d:6 * d]
    hb = (_rmsnorm(x, n2g_ref[...]) * (1.0 + sc2) + sh2).astype(BF16)
    acc = jnp.zeros(x.shape, F32)
    for c in range(D_FF // d):
        h1 = jnp.maximum(_dot(hb, w1_ref[:, c * d:(c + 1) * d].astype(BF16)), 0.0)
        acc = acc + _dot((h1 * h1).astype(BF16), w2_ref[c * d:(c + 1) * d, :].astype(BF16))
    xn = x + g2 * acc
    if final:
        xn = _rmsnorm(xn, fg_ref[...])
    o_ref[0] = xn


def _ffn_call(layer, x, mod, wl, final_g, tm, name, per_token_mod=False):
    final = layer == DEPTH - 1
    bsz, seq, d = x.shape
    if per_token_mod:
        assert bsz == 1 and mod.shape == (DEPTH, seq, 6 * d)
        mod_spec = pl.BlockSpec((1, tm, 6 * d), lambda b, t: (layer, t, 0))
    else:
        assert mod.shape == (bsz, 1, 6 * d)
        mod_spec = pl.BlockSpec((1, 1, 6 * d), lambda b, t: (b, 0, 0))
    return pl.pallas_call(
        functools.partial(_ffn_kernel, final),
        out_shape=jax.ShapeDtypeStruct((bsz, seq, d), F32),
        grid=(bsz, seq // tm),
        in_specs=[
            pl.BlockSpec((1, tm, d), lambda b, t: (b, t, 0)),
            mod_spec,
            _const_spec((1, d), layer),
            _const_spec((d, D_FF), layer),
            _const_spec((D_FF, d), layer),
            _const_spec((1, d)),
        ],
        out_specs=pl.BlockSpec((1, tm, d), lambda b, t: (b, t, 0)),
        compiler_params=pltpu.CompilerParams(
            dimension_semantics=("arbitrary", "arbitrary"), vmem_limit_bytes=VMEM_LIMIT),
        name=name,
    )(x, mod, wl["n2g"], wl["w1"], wl["w2"], final_g)


def _sample_inproj_kernel(x_ref, mod_ref, n1g_ref, w_ref, o_ref, wb_ref):
    d = D_MODEL
    mod = mod_ref[...]
    hb = (_rmsnorm(x_ref[...], n1g_ref[...]) * (1.0 + mod[:, d:2 * d]) + mod[:, 0:d]).astype(BF16)
    wb = w_ref[...].astype(BF16)
    wb_ref[...] = wb
    o_ref[...] = _dot(hb, wb)


def _sample_inproj_call(layer, x, mod_s, wl, w_in):
    n, d = x.shape
    return pl.pallas_call(
        _sample_inproj_kernel,
        out_shape=(jax.ShapeDtypeStruct((n, N_IN), F32), jax.ShapeDtypeStruct((d, N_IN), BF16)),
        grid=(N_IN // d,),
        in_specs=[
            pl.BlockSpec((n, d), lambda j: (0, 0)),
            pl.BlockSpec((None, n, 6 * d), lambda j: (layer, 0, 0)),
            pl.BlockSpec((None, 1, d), lambda j: (layer, 0, 0)),
            pl.BlockSpec((None, d, d), lambda j: (layer, 0, j)),
        ],
        out_specs=(pl.BlockSpec((n, d), lambda j: (0, j)), pl.BlockSpec((d, d), lambda j: (0, j))),
        compiler_params=pltpu.CompilerParams(
            dimension_semantics=("arbitrary",), vmem_limit_bytes=VMEM_LIMIT),
        name=f"sample_inproj_{layer}",
    )(x, mod_s, wl["n1g"], w_in)


N_SAMPLE_MIXER_INPUTS = 20

OUTER_ROWS = 16
OUTER_LEFT_SRC = (0, 0, 1, 0, 2, 1, 3, 4, 5)
OUTER_RIGHT_SRC = (0, 1, 0, 2, 0, 1)


def _outer_tables():
    g = SAMPLE_GROUP
    sel_l = np.zeros((g * OUTER_ROWS, 6 * g), np.float32)
    sel_r = np.zeros((g * OUTER_ROWS, 4 * g), np.float32)
    for j in range(g):
        for r, a in enumerate(OUTER_LEFT_SRC):
            sel_l[OUTER_ROWS * j + r, g * a + j] = 1.0
        for r, a in enumerate(OUTER_RIGHT_SRC):
            sel_r[OUTER_ROWS * j + r, g * a + j] = 1.0
    ones_rows = np.zeros((g, OUTER_ROWS, LANES), np.float32)
    ones_rows[:, len(OUTER_RIGHT_SRC):len(OUTER_LEFT_SRC)] = 1.0
    return (jnp.asarray(sel_l, BF16), jnp.asarray(sel_r, BF16),
            jnp.asarray(ones_rows.reshape(g * OUTER_ROWS, LANES), BF16))


def _split3(x):
    x1 = x.astype(BF16).astype(F32)
    r = x - x1
    x2 = r.astype(BF16).astype(F32)
    x3 = (r - x2).astype(BF16).astype(F32)
    return [x1, x2, x3]


def _sample_mixer_kernel(layer, n_groups, *refs):
    refs = refs[:N_SAMPLE_MIXER_INPUTS] + refs[len(refs) - 10:]
    (u_ref, x_ref, mod_ref, hg_ref, lru_ref, conv_ref, hgl_ref, ong_ref, cw_ref, cb_ref,
     wax_ref, ba_ref, bx_ref, lam_ref, wba_ref, wbb_ref, wout_ref, sell_ref, selr_ref, ones_ref,
     xo_ref, hgo_ref, lruo_ref, convo_ref,
     f_s, k_s, q_s, v_s, o_s, ob_s) = refs
    d = D_MODEL
    n = x_ref.shape[0]
    i = pl.program_id(0)
    if layer == 0:
        for other in range(1, DEPTH):
            hgo_ref[other] = jnp.zeros(hgo_ref.shape[1:], F32)
        hgo_ref = hgo_ref.at[0]

    @pl.when(i == 0)
    def _():
        q_s[...] = _silu(u_ref[:, 0:d])
        lb = _lower_bound(hgl_ref[...], layer)
        f, k = _forget_key(u_ref[:, d:2 * d], lb)
        f_s[...] = f
        k_s[...] = k
        v_s[...] = u_ref[:, 2 * d:3 * d]
        xb = u_ref[:, 4 * d:5 * d]
        cw = cw_ref[...]
        xc = cb_ref[...] + cw[3:4] * xb
        for j in range(CONV_W - 1):
            xc = xc + cw[j:j + 1] * conv_ref[j]
        convo_ref[0] = conv_ref[1]
        convo_ref[1] = conv_ref[2]
        convo_ref[2] = xb
        yb = _gelu_tanh(u_ref[:, 5 * d:6 * d])
        sp = _softplus(-lam_ref[...])
        for blk in range(LRU_BLOCKS):
            cols = slice(blk * LRU_BS, (blk + 1) * LRU_BS)
            xcn = xc[:, cols]
            gates = _dot(xcn.astype(BF16), wax_ref[blk])
            a, bt = _lru_coeffs(xcn, gates, ba_ref[:, cols], bx_ref[:, cols], sp[:, cols])
            hn = a * lru_ref[:, cols] + bt
            lruo_ref[:, cols] = hn
            ob_s[:, cols] = hn * yb[:, cols]

    rows = pl.ds(pl.multiple_of(i * SAMPLE_GROUP, SAMPLE_GROUP), SAMPLE_GROUP)
    ones_rows = ones_ref[...]

    def outer_products(h):
        cols = slice(h * LANES, (h + 1) * LANES)
        vparts = _split3(v_s[rows, cols])
        left = jnp.concatenate(_split3(k_s[rows, cols]) + _split3(f_s[rows, cols]), axis=0).astype(BF16)
        right = jnp.concatenate(vparts + [jnp.zeros_like(vparts[0])], axis=0).astype(BF16)
        left_t = _dot(sell_ref[...], left).T.astype(BF16)
        right_rows = jnp.concatenate([_dot(selr_ref[...], right).astype(BF16), ones_rows], axis=1)
        out = []
        for j in range(SAMPLE_GROUP):
            lo, hi = j * OUTER_ROWS, (j + 1) * OUTER_ROWS
            pieces = [right_rows[lo:hi]]
            if lo:
                pieces.insert(0, jnp.zeros((lo, 2 * HG_DV), BF16))
            if hi < SAMPLE_GROUP * OUTER_ROWS:
                pieces.append(jnp.zeros((SAMPLE_GROUP * OUTER_ROWS - hi, 2 * HG_DV), BF16))
            out.append(_dot(left_t, jnp.concatenate(pieces, axis=0)))
        return out

    kv_f = outer_products(0)
    for h in range(HG_HEADS):
        cols = slice(h * LANES, (h + 1) * LANES)
        kv_f_next = outer_products(h + 1) if h + 1 < HG_HEADS else None
        s_new = [hg_ref[j, h] * kv_f[j][:, HG_DV:] + kv_f[j][:, :HG_DV] for j in range(SAMPLE_GROUP)]
        for j in range(SAMPLE_GROUP):
            hgo_ref[j, h] = s_new[j]
        qg = q_s[rows, cols].astype(BF16)
        o_rows = [_dot(qg, s_new[j].astype(BF16))[j:j + 1, :] for j in range(SAMPLE_GROUP)]
        o_s[rows, cols] = jnp.concatenate(o_rows, axis=0)
        kv_f = kv_f_next

    @pl.when(i == n_groups - 1)
    def _():
        ong = ong_ref[...]
        parts = []
        for h in range(HG_HEADS):
            cols = slice(h * LANES, (h + 1) * LANES)
            parts.append(_rmsnorm(o_s[:, cols], ong) * _silu(u_ref[:, 3 * d + h * LANES:3 * d + (h + 1) * LANES]))
        o_a = jnp.concatenate(parts, axis=1).astype(BF16)
        merged = (_sigmoid(u_ref[:, 6 * d:7 * d]) * _dot(o_a, wba_ref[...])
                  + _sigmoid(u_ref[:, 7 * d:8 * d]) * _dot(ob_s[...].astype(BF16), wbb_ref[...]))
        g1 = mod_ref[:, 2 * d:3 * d]
        xo_ref[...] = x_ref[...] + g1 * _dot(merged.astype(BF16), wout_ref[...])


def _sample_mixer_call(layer, u, x, mod_s, st_hg_all, st_lru_all, st_conv_t, hg_lower, wl, hg_out_prev):
    n, d = x.shape
    grp = SAMPLE_GROUP
    n_groups = n // grp

    def full(shape):
        nd = len(shape)
        return pl.BlockSpec(shape, lambda i: (0,) * nd)

    def of_layer(shape):
        nd = len(shape)
        return pl.BlockSpec((None,) + tuple(shape), lambda i: (layer,) + (0,) * nd)

    st_spec = pl.BlockSpec((None, grp, HG_HEADS, HG_DK, HG_DV), lambda i: (layer, i, 0, 0, 0))
    out_shapes = (
        jax.ShapeDtypeStruct((n, d), F32),
        jax.ShapeDtypeStruct((DEPTH, n, HG_HEADS, HG_DK, HG_DV), F32),
        jax.ShapeDtypeStruct((n, d), F32),
        jax.ShapeDtypeStruct((CONV_W - 1, n, d), F32),
    )
    in_specs = [
        full((n, N_IN)), full((n, d)), of_layer((n, 6 * d)), st_spec,
        of_layer((n, d)), full((CONV_W - 1, n, d)),
        full((DEPTH, d)), of_layer((1, HG_DV)), of_layer((CONV_W, d)), of_layer((1, d)),
        of_layer((LRU_BLOCKS, LRU_BS, 2 * LRU_BS)), of_layer((1, d)), of_layer((1, d)), of_layer((1, d)),
        of_layer((d, d)), of_layer((d, d)), of_layer((d, d)),
    ]
    outer_tables = _outer_tables()
    in_specs += [full(t.shape) for t in outer_tables]
    assert len(in_specs) == N_SAMPLE_MIXER_INPUTS
    args = [u, x, mod_s, st_hg_all, st_lru_all, st_conv_t, hg_lower, wl["ong"], wl["cw"], wl["cb"], wl["wax"],
            wl["ba"], wl["bx"], wl["lam"], wl["wba"], wl["wbb"], wl["wout"], *outer_tables]
    aliases = {}
    if hg_out_prev is not None:
        in_specs.append(pl.BlockSpec(memory_space=pl.ANY))
        args.append(hg_out_prev)
        aliases = {N_SAMPLE_MIXER_INPUTS: 1}
    st_out_spec = st_spec if layer else pl.BlockSpec((DEPTH, grp, HG_HEADS, HG_DK, HG_DV),
                                                     lambda i: (0, i, 0, 0, 0))
    out_specs = (full((n, d)), st_out_spec, full((n, d)), full((CONV_W - 1, n, d)))
    scratch = [
        pltpu.VMEM((n, d), F32),
        pltpu.VMEM((n, d), F32),
        pltpu.VMEM((n, d), F32),
        pltpu.VMEM((n, d), F32),
        pltpu.VMEM((n, d), F32),
        pltpu.VMEM((n, d), F32),
    ]
    return pl.pallas_call(
        functools.partial(_sample_mixer_kernel, layer, n_groups),
        out_shape=out_shapes,
        grid=(n_groups,),
        in_specs=in_specs,
        out_specs=out_specs,
        scratch_shapes=scratch,
        input_output_aliases=aliases,
        compiler_params=pltpu.CompilerParams(
            dimension_semantics=("arbitrary",), vmem_limit_bytes=VMEM_LIMIT),
        name=f"sample_mixer_{layer}",
    )(*args)


def _stacked_weights(norm1_g, norm2_g, hg_onorm_g, lru_conv_w, lru_conv_b, lru_wa, lru_ba, lru_wx,
                     lru_bx, lru_lambda, w_branch_a, w_branch_b, w_out, w_ff1, w_ff2):
    d = D_MODEL
    return {
        "n1g": norm1_g.reshape(DEPTH, 1, d),
        "n2g": norm2_g.reshape(DEPTH, 1, d),
        "ong": hg_onorm_g.reshape(DEPTH, 1, HG_DV),
        "cw": lru_conv_w,
        "cb": lru_conv_b.reshape(DEPTH, 1, d),
        "wax": jnp.concatenate([lru_wa, lru_wx], axis=-1).astype(BF16),
        "ba": lru_ba.reshape(DEPTH, 1, d),
        "bx": lru_bx.reshape(DEPTH, 1, d),
        "lam": lru_lambda.reshape(DEPTH, 1, d),
        "wba": w_branch_a.astype(BF16),
        "wbb": w_branch_b.astype(BF16),
        "wout": w_out.astype(BF16),
        "w1": w_ff1,
        "w2": w_ff2,
    }


def kernel(x_prompt, x_sample, state_hgrn, state_rglru, state_conv, c_prompt, c_sample, w_mod, b_mod, norm1_g, norm2_g, w_in, hg_lower, hg_onorm_g, lru_conv_w, lru_conv_b, lru_wa, lru_ba, lru_wx, lru_bx, lru_lambda, w_branch_a, w_branch_b, w_out, w_ff1, w_ff2, final_norm_g):
    bsz, seq, d = x_prompt.shape
    n_s = x_sample.shape[0]
    assert d == D_MODEL and x_sample.shape[1] == 1 and seq % MIX_TL == 0 and seq % FFN_TM == 0
    assert n_s % SAMPLE_GROUP == 0 and n_s == LANES

    mod_p_all, mod_s = _mod_call(c_prompt, c_sample, w_mod, b_mod)
    lvl = _hg_tables()
    final_g = final_norm_g.reshape(1, d)

    xp = x_prompt
    xs = x_sample.reshape(n_s, d)
    hg_p, lru_p, conv_p, lru_s, conv_s = [], [], [], [], []
    hg_s = None
    wl = _stacked_weights(norm1_g, norm2_g, hg_onorm_g, lru_conv_w, lru_conv_b, lru_wa, lru_ba,
                          lru_wx, lru_bx, lru_lambda, w_branch_a, w_branch_b, w_out, w_ff1, w_ff2)
    for l in range(DEPTH):
        mod_p = mod_p_all[l].reshape(bsz, 1, 6 * d)

        u, w_in_bf16 = _sample_inproj_call(l, xs, mod_s, wl, w_in)

        xp, hg, lru, conv = _prompt_mixer_call(l, xp, mod_p, hg_lower, wl, w_in_bf16, lvl)
        xp = _ffn_call(l, xp, mod_p, wl, final_g, FFN_TM, f"prompt_ffn_{l}")
        hg_p.append(hg)
        lru_p.append(lru.reshape(bsz, d))
        conv_p.append(conv)

        xs, hg_s, lru, conv = _sample_mixer_call(l, u, xs, mod_s, state_hgrn, state_rglru,
                                                 jnp.swapaxes(state_conv[l], 0, 1), hg_lower, wl, hg_s)
        xs = _ffn_call(l, xs.reshape(1, n_s, d), mod_s, wl, final_g, n_s, f"sample_ffn_{l}",
                       per_token_mod=True).reshape(n_s, d)
        lru_s.append(lru)
        conv_s.append(jnp.swapaxes(conv, 0, 1))

    return (xp, xs.reshape(n_s, 1, d),
            jnp.stack(hg_p), jnp.stack(lru_p), jnp.stack(conv_p),
            hg_s, jnp.stack(lru_s), jnp.stack(conv_s))
```

```python
import functools

import numpy as np
import jax
import jax.numpy as jnp
from jax import lax
from jax.experimental import pallas as pl
from jax.experimental.pallas import tpu as pltpu

F32 = jnp.float32
BF16 = jnp.bfloat16

D_MODEL = 1024
DEPTH = 2
HG_HEADS = 8
HG_DK = 128
HG_DV = 128
LB_FLOOR = 1e-30
LRU_BLOCKS = 8
LRU_BS = 128
LRU_C = 8.0
CONV_W = 4
D_FF = 4 * D_MODEL
EPS = 1e-6
N_IN = 8 * D_MODEL
F32_TINY = float(np.finfo(np.float32).tiny)

LANES = 128
SUBLANES = 8
HG_CHUNK = 128
HG_LEVELS = 7
MIX_TL = 256
FFN_TM = 512
SAMPLE_GROUP = 8
VMEM_LIMIT = 56 * 1024 * 1024


def _sigmoid(x):
    return 0.5 * jnp.tanh(0.5 * x) + 0.5


def _silu(x):
    h = 0.5 * x
    return h * jnp.tanh(h) + h


def _gelu_tanh(x):
    c = 0.7978845608028654
    h = 0.5 * x
    return h * jnp.tanh(x * (c + (c * 0.044715) * (x * x))) + h


def _half_tanh_gate(t, y):
    h = 0.5 * y
    return h * t + h


def _log1p(x):
    u = 1.0 + x
    return jnp.where(u == 1.0, x, jnp.log(u) * (x / (u - 1.0)))


def _softplus(z):
    return jnp.maximum(z, 0.0) + _log1p(jnp.exp(-jnp.abs(z)))


def _rmsnorm(x, g):
    ms = jnp.mean(x * x, axis=-1, keepdims=True)
    return x * lax.rsqrt(ms + EPS) * g


def _dot(a, b):
    return jnp.dot(a, b, preferred_element_type=F32)


def _dot_nt(a, b):
    return lax.dot_general(a, b, (((1,), (1,)), ((), ())), preferred_element_type=F32)


def _dot_tn(a, b):
    return lax.dot_general(a, b, (((0,), (0,)), ((), ())), preferred_element_type=F32)


def _lower_bound(hgl, layer):
    m = jnp.max(hgl, axis=0, keepdims=True)
    e = jnp.exp(hgl - m)
    p = e / jnp.sum(e, axis=0, keepdims=True)
    cum = p[0:1]
    for j in range(1, layer + 1):
        cum = cum + p[j:j + 1]
    return cum - p[0:1]


def _forget_key(fa, lb):
    e = jnp.exp(-jnp.abs(fa))
    r = 1.0 / (1.0 + e)
    er = e * r
    pos = fa >= 0.0
    sig_p = jnp.where(pos, r, er)
    sig_n = jnp.where(pos, er, r)
    f = jnp.maximum(lb, LB_FLOOR) + (1.0 - lb) * sig_p
    k = (1.0 - lb) * sig_n
    return f, k


def _lru_coeffs(xc, gates, ba, bx, sp):
    tr = jnp.tanh(0.5 * gates[:, :LRU_BS] + 0.5 * ba)
    ti = jnp.tanh(0.5 * gates[:, LRU_BS:] + 0.5 * bx)
    c1 = (-0.5 * LRU_C) * sp
    log_a = c1 * tr + c1
    a = jnp.exp(log_a)
    th = jnp.tanh(log_a)
    one_m_a2 = (-2.0 * th) / (1.0 - th)
    root = one_m_a2 * lax.rsqrt(jnp.maximum(one_m_a2, F32_TINY))
    return a, root * _half_tanh_gate(ti, xc)


def _mod_kernel(cp_ref, cs_ref, w_ref, b_ref, op_ref, os_ref):
    w = w_ref[0].astype(BF16)
    op_ref[0] = _dot(_silu(cp_ref[...]).astype(BF16), w) + b_ref[0]
    os_ref[0] = _dot(_silu(cs_ref[...]).astype(BF16), w) + b_ref[0]


def _mod_call(c_prompt, c_sample, w_mod, b_mod):
    n_p, n_s = c_prompt.shape[0], c_sample.shape[0]
    tn = 1536
    return pl.pallas_call(
        _mod_kernel,
        out_shape=(jax.ShapeDtypeStruct((DEPTH, n_p, 6 * D_MODEL), F32),
                   jax.ShapeDtypeStruct((DEPTH, n_s, 6 * D_MODEL), F32)),
        grid=(DEPTH, 6 * D_MODEL // tn),
        in_specs=[
            pl.BlockSpec((n_p, D_MODEL), lambda l, j: (0, 0)),
            pl.BlockSpec((n_s, D_MODEL), lambda l, j: (0, 0)),
            pl.BlockSpec((1, D_MODEL, tn), lambda l, j: (l, 0, j)),
            pl.BlockSpec((1, 1, tn), lambda l, j: (l, 0, j)),
        ],
        out_specs=(pl.BlockSpec((1, n_p, tn), lambda l, j: (l, 0, j)),
                   pl.BlockSpec((1, n_s, tn), lambda l, j: (l, 0, j))),
        compiler_params=pltpu.CompilerParams(
            dimension_semantics=("arbitrary", "arbitrary"), vmem_limit_bytes=VMEM_LIMIT),
        name="mod",
    )(c_prompt, c_sample, w_mod, b_mod.reshape(DEPTH, 1, 6 * D_MODEL))


def _hg_tables():
    t = np.arange(HG_CHUNK)
    x = t[:, None] ^ t[None, :]
    hb = np.floor(np.log2(np.maximum(x, 1))).astype(np.int32)
    lvl = np.where(t[:, None] > t[None, :], hb + 1, np.where(t[:, None] == t[None, :], 0, -1)).astype(np.int32)
    return jnp.asarray(lvl)


def _x_within_group(q, k, g2, level):
    n = q.shape[0]
    shp = (n // SUBLANES, SUBLANES, LANES)
    q3, k3, g3 = q.reshape(shp), k.reshape(shp), g2.reshape(shp)
    sub = lax.broadcasted_iota(jnp.int32, (1, SUBLANES, LANES), 1)
    qside = ((sub >> level) & 1) == 1
    if level == 0:
        scaled = jnp.where(qside, q3 * jnp.exp2(g3 - pltpu.roll(g3, 1, axis=1)), k3)
        return scaled.reshape(n, LANES)

    def row(i):
        return jnp.broadcast_to(g3[:, i:i + 1, :], shp)

    piv = jnp.where(sub < 4, row(1), row(5)) if level == 1 else row(3)
    e = jnp.exp2(-jnp.abs(g3 - piv))
    return (jnp.where(qside, q3, k3) * e).reshape(n, LANES)


def _x_across_groups(q, k, g2, level):
    n = q.shape[0]
    b = 1 << level
    blocks = []
    for base in range(0, n, 2 * b):
        lo, mid, hi = base, base + b, base + 2 * b
        gp = g2[mid - 1:mid, :]
        blocks.append(k[lo:mid] * jnp.exp2(gp - g2[lo:mid]))
        blocks.append(q[mid:hi] * jnp.exp2(g2[mid:hi] - gp))
    return jnp.concatenate(blocks, axis=0)


HG_FULL_LEVELS = 4


def _hg_level_products(q, k, g2):
    c = HG_CHUNK
    prods = [_dot_nt(q.astype(BF16), k.astype(BF16))]
    for level in range(HG_LEVELS):
        b = 1 << level
        if b < SUBLANES:
            x = _x_within_group(q, k, g2, level).astype(BF16)
        else:
            x = _x_across_groups(q, k, g2, level).astype(BF16)
        if level < HG_FULL_LEVELS:
            lhs = x
        else:
            lhs = jnp.concatenate([x[base + b:base + 2 * b] for base in range(0, c, 2 * b)], axis=0)
        prods.append(_dot_nt(lhs, x))
    return prods


def _hg_assemble(prods_per_head, lvl_ref):
    c = HG_CHUNK
    n_heads = len(prods_per_head)
    groups = [[] for _ in range(n_heads)]
    for gi in range(c // SUBLANES):
        r0 = gi * SUBLANES
        rows = slice(r0, r0 + SUBLANES)
        lv = lvl_ref[rows, :]
        on_diag = lv == 0
        a_g = [jnp.where(on_diag, p[0][rows], 0.0) for p in prods_per_head]
        for level in range(HG_LEVELS):
            b = 1 << level
            if level < HG_FULL_LEVELS:
                src = r0
            elif r0 & b:
                src = (r0 >> (level + 1)) * b + (r0 & (b - 1))
            else:
                continue
            at_level = lv == level + 1
            a_g = [jnp.where(at_level, p[level + 1][src:src + SUBLANES], a)
                   for p, a in zip(prods_per_head, a_g)]
        for h in range(n_heads):
            groups[h].append(a_g[h])
    return [jnp.concatenate(g, axis=0).astype(BF16) for g in groups]


def _hg_finish(q, k, v, g2, st, a):
    c = HG_CHUNK
    g_last = g2[c - 1:c, :]
    qg = (q * jnp.exp2(g2)).astype(BF16)
    vb = v.astype(BF16)
    o = _dot_nt(qg, st.astype(BF16)) + _dot(a, vb)
    kd = (k * jnp.exp2(g_last - g2)).astype(BF16)
    st_new = st * jnp.exp2(g_last) + _dot_tn(vb, kd)
    return o, st_new


def _causal_conv(xb, prev8, cw_ref, cb_ref):
    t, w = xb.shape
    x3 = xb.reshape(t // SUBLANES, SUBLANES, w)
    sub = lax.broadcasted_iota(jnp.int32, (1, SUBLANES, w), 1)

    def tap(j):
        return cw_ref[j:j + 1, :][None]

    out = cb_ref[...][None] + tap(CONV_W - 1) * x3
    for back in range(1, CONV_W):
        r = pltpu.roll(x3, back, axis=1)
        r_prev_group = jnp.concatenate([pltpu.roll(prev8, back, axis=0)[None], r[:-1]], axis=0)
        shifted = jnp.where(sub >= back, r, r_prev_group)
        out = out + tap(CONV_W - 1 - back) * shifted
    return out.reshape(t, w)


def _chunk_cumsum(y):
    t = y.shape[0]
    y3 = y.reshape(t // SUBLANES, SUBLANES, LANES)
    sub = lax.broadcasted_iota(jnp.int32, (1, SUBLANES, LANES), 1)
    d = 1
    while d < SUBLANES:
        y3 = y3 + jnp.where(sub >= d, pltpu.roll(y3, d, axis=1), 0.0)
        d *= 2
    per_chunk = HG_CHUNK // SUBLANES
    out = []
    for gi in range(t // SUBLANES):
        blk = y3[gi]
        if gi % per_chunk:
            blk = blk + out[-1][SUBLANES - 1:SUBLANES, :]
        out.append(blk)
    return jnp.concatenate(out, axis=0)


def _lru_scan(a, b, h0):
    t = a.shape[0]
    a3 = a.reshape(t // SUBLANES, SUBLANES, LANES)
    b3 = b.reshape(t // SUBLANES, SUBLANES, LANES)
    sub = lax.broadcasted_iota(jnp.int32, a3.shape, 1)
    d = 1
    while d < SUBLANES:
        m = sub >= d
        b3 = a3 * jnp.where(m, pltpu.roll(b3, d, axis=1), 0.0) + b3
        a3 = a3 * jnp.where(m, pltpu.roll(a3, d, axis=1), 1.0)
        d *= 2
    hs = []
    h = h0
    for gidx in range(t // SUBLANES):
        hg = a3[gidx] * h + b3[gidx]
        hs.append(hg)
        h = hg[SUBLANES - 1:SUBLANES, :]
    return jnp.concatenate(hs, axis=0), h


def _prompt_mixer_kernel(layer, n_t,
                         x_ref, mod_ref, hgl_ref, n1g_ref, win_ref, ong_ref, cw_ref, cb_ref, wax_ref,
                         ba_ref, bx_ref, lam_ref, wba_ref, wbb_ref, wout_ref, lvl_ref,
                         xo_ref, hg_ref, lru_ref, conv_ref,
                         q_s, k_s, v_s, g_s, og_s, oa_s, ob_s, prev8_s, st_s, hl_s, xc_s, yb_s, ga_s, gb_s, gr_s,
                         gi_s):
    tl = MIX_TL
    d = D_MODEL
    t_idx = pl.program_id(1)

    @pl.when(t_idx == 0)
    def _():
        st_s[...] = jnp.zeros(st_s.shape, F32)
        hl_s[...] = jnp.zeros(hl_s.shape, F32)
        prev8_s[...] = jnp.zeros(prev8_s.shape, F32)

    x = x_ref[0]
    mod = mod_ref[0]
    sh1, sc1, g1 = mod[:, 0:d], mod[:, d:2 * d], mod[:, 2 * d:3 * d]
    hb = (_rmsnorm(x, n1g_ref[...]) * (1.0 + sc1) + sh1).astype(BF16)

    def proj(j):
        return _dot(hb, win_ref[:, j * d:(j + 1) * d])

    def heads(val, dst):
        for h in range(HG_HEADS):
            dst[h] = val[:, h * LANES:(h + 1) * LANES]

    sp = _softplus(-lam_ref[...])
    ong = ong_ref[...]

    def lru_block(n):
        cols = slice(n * LRU_BS, (n + 1) * LRU_BS)
        gates = jnp.concatenate([gr_s[:, cols], gi_s[:, cols]], axis=1)
        a, bt = _lru_coeffs(xc_s[:, cols], gates, ba_ref[:, cols], bx_ref[:, cols], sp[:, cols])
        hseq, h_last = _lru_scan(a, bt, hl_s[:, cols])
        hl_s[:, cols] = h_last
        ob_s[:, cols] = hseq * yb_s[:, cols]

    def level_products(c, head_ids):
        rows = slice(c * HG_CHUNK, (c + 1) * HG_CHUNK)
        return [_hg_level_products(q_s[h, rows, :], k_s[h, rows, :], g_s[h, rows, :]) for h in head_ids]

    def finish(c, head_ids, a_mats):
        rows = slice(c * HG_CHUNK, (c + 1) * HG_CHUNK)
        for h, a in zip(head_ids, a_mats):
            o, st_new = _hg_finish(q_s[h, rows, :], k_s[h, rows, :], v_s[h, rows, :], g_s[h, rows, :],
                                   st_s[h], a)
            st_s[h] = st_new
            oa_s[h, rows, :] = _rmsnorm(o, og_s[h, rows, :])

    xb = proj(4)
    xc = _causal_conv(xb, prev8_s[...], cw_ref, cb_ref)
    xc_s[...] = xc
    prev8_s[...] = xb[tl - SUBLANES:tl, :]
    conv_ref[0] = xb[tl - (CONV_W - 1):tl, :]
    for n in range(LRU_BLOCKS):
        cols = slice(n * LRU_BS, (n + 1) * LRU_BS)
        gates = _dot(xc[:, cols].astype(BF16), wax_ref[n])
        gr_s[:, cols] = gates[:, :LRU_BS]
        gi_s[:, cols] = gates[:, LRU_BS:]
    yb_s[...] = _gelu_tanh(proj(5))
    lb = _lower_bound(hgl_ref[...], layer)
    f, k = _forget_key(proj(1), lb)
    heads(k, k_s)
    log2f = jnp.minimum(jnp.log2(f), 0.0)
    for h in range(HG_HEADS):
        g_s[h] = _chunk_cumsum(log2f[:, h * LANES:(h + 1) * LANES])
    lru_block(0)
    lru_block(1)
    heads(_silu(proj(0)), q_s)
    lru_block(2)
    lru_block(3)
    heads(proj(2), v_s)
    lru_block(4)
    lru_block(5)
    og = _silu(proj(3))
    for h in range(HG_HEADS):
        og_s[h] = og[:, h * LANES:(h + 1) * LANES] * ong
    lru_block(6)
    lru_block(7)
    lru_ref[0] = hl_s[...]
    ga_s[...] = jnp.tanh(0.5 * proj(6))
    gb_s[...] = jnp.tanh(0.5 * proj(7))
    half = HG_HEADS // 2
    for c in range(tl // HG_CHUNK):
        for head_ids in (range(0, half), range(half, HG_HEADS)):
            finish(c, head_ids, _hg_assemble(level_products(c, head_ids), lvl_ref))

    o_a = jnp.concatenate([oa_s[h] for h in range(HG_HEADS)], axis=1).astype(BF16)
    merged = (_half_tanh_gate(ga_s[...], _dot(o_a, wba_ref[...]))
              + _half_tanh_gate(gb_s[...], _dot(ob_s[...].astype(BF16), wbb_ref[...])))
    xo_ref[0] = x + g1 * _dot(merged.astype(BF16), wout_ref[...])

    @pl.when(t_idx == n_t - 1)
    def _():
        for h in range(HG_HEADS):
            hg_ref[0, h] = st_s[h].T


def _const_spec(shape, layer=None):
    nd = len(shape)
    if layer is None:
        return pl.BlockSpec(shape, lambda *_: (0,) * nd, pipeline_mode=pl.Buffered(1))
    return pl.BlockSpec((None,) + tuple(shape), lambda *_: (layer,) + (0,) * nd, pipeline_mode=pl.Buffered(1))


def _prompt_mixer_call(layer, x, mod_p, hg_lower, wl, w_in_bf16, lvl):
    bsz, seq, d = x.shape
    tl = MIX_TL
    n_t = seq // tl
    hshape = (HG_HEADS, tl, LANES)
    out_shapes = (
        jax.ShapeDtypeStruct((bsz, seq, d), F32),
        jax.ShapeDtypeStruct((bsz, HG_HEADS, HG_DK, HG_DV), F32),
        jax.ShapeDtypeStruct((bsz, 1, d), F32),
        jax.ShapeDtypeStruct((bsz, CONV_W - 1, d), F32),
    )
    in_specs = [
        pl.BlockSpec((1, tl, d), lambda b, t: (b, t, 0)),
        pl.BlockSpec((1, 1, 6 * d), lambda b, t: (b, 0, 0)),
        _const_spec((DEPTH, d)),
        _const_spec((1, d), layer),
        _const_spec((d, N_IN)),
        _const_spec((1, HG_DV), layer),
        _const_spec((CONV_W, d), layer),
        _const_spec((1, d), layer),
        _const_spec((LRU_BLOCKS, LRU_BS, 2 * LRU_BS), layer),
        _const_spec((1, d), layer),
        _const_spec((1, d), layer),
        _const_spec((1, d), layer),
        _const_spec((d, d), layer),
        _const_spec((d, d), layer),
        _const_spec((d, d), layer),
        _const_spec((HG_CHUNK, HG_CHUNK)),
    ]
    out_specs = (
        pl.BlockSpec((1, tl, d), lambda b, t: (b, t, 0)),
        pl.BlockSpec((1, HG_HEADS, HG_DK, HG_DV), lambda b, t: (b, 0, 0, 0)),
        pl.BlockSpec((1, 1, d), lambda b, t: (b, 0, 0)),
        pl.BlockSpec((1, CONV_W - 1, d), lambda b, t: (b, 0, 0)),
    )
    scratch = [
        pltpu.VMEM(hshape, F32),
        pltpu.VMEM(hshape, F32),
        pltpu.VMEM(hshape, F32),
        pltpu.VMEM(hshape, F32),
        pltpu.VMEM(hshape, F32),
        pltpu.VMEM(hshape, F32),
        pltpu.VMEM((tl, d), F32),
        pltpu.VMEM((SUBLANES, d), F32),
        pltpu.VMEM((HG_HEADS, HG_DV, HG_DK), F32),
        pltpu.VMEM((1, d), F32),
        pltpu.VMEM((tl, d), F32),
        pltpu.VMEM((tl, d), F32),
        pltpu.VMEM((tl, d), F32),
        pltpu.VMEM((tl, d), F32),
        pltpu.VMEM((tl, d), F32),
        pltpu.VMEM((tl, d), F32),
    ]
    return pl.pallas_call(
        functools.partial(_prompt_mixer_kernel, layer, n_t),
        out_shape=out_shapes,
        grid=(bsz, n_t),
        in_specs=in_specs,
        out_specs=out_specs,
        scratch_shapes=scratch,
        compiler_params=pltpu.CompilerParams(
            dimension_semantics=("arbitrary", "arbitrary"), vmem_limit_bytes=VMEM_LIMIT),
        name=f"prompt_mixer_{layer}",
    )(x, mod_p, hg_lower, wl["n1g"], w_in_bf16, wl["ong"], wl["cw"], wl["cb"], wl["wax"],
      wl["ba"], wl["bx"], wl["lam"], wl["wba"], wl["wbb"], wl["wout"], lvl)


def _ffn_kernel(final, x_ref, mod_ref, n2g_ref, w1_ref, w2_ref, fg_ref, o_ref):
    d = D_MODEL
    x = x_ref[0]
    mod = mod_ref[0]
    sh2, sc2, g2 = mod[:, 3 * d:4 * d], mod[:, 4 * d:5 * d], mod[:, 5 * d:6 * d]
    hb = (_rmsnorm(x, n2g_ref[...]) * (1.0 + sc2) + sh2).astype(BF16)
    acc = jnp.zeros(x.shape, F32)
    for c in range(D_FF // d):
        h1 = jnp.maximum(_dot(hb, w1_ref[:, c * d:(c + 1) * d].astype(BF16)), 0.0)
        acc = acc + _dot((h1 * h1).astype(BF16), w2_ref[c * d:(c + 1) * d, :].astype(BF16))
    xn = x + g2 * acc
    if final:
        xn = _rmsnorm(xn, fg_ref[...])
    o_ref[0] = xn


def _ffn_call(layer, x, mod, wl, final_g, tm, name, per_token_mod=False):
    final = layer == DEPTH - 1
    bsz, seq, d = x.shape
    if per_token_mod:
        assert bsz == 1 and mod.shape == (DEPTH, seq, 6 * d)
        mod_spec = pl.BlockSpec((1, tm, 6 * d), lambda b, t: (layer, t, 0))
    else:
        assert mod.shape == (bsz, 1, 6 * d)
        mod_spec = pl.BlockSpec((1, 1, 6 * d), lambda b, t: (b, 0, 0))
    return pl.pallas_call(
        functools.partial(_ffn_kernel, final),
        out_shape=jax.ShapeDtypeStruct((bsz, seq, d), F32),
        grid=(bsz, seq // tm),
        in_specs=[
            pl.BlockSpec((1, tm, d), lambda b, t: (b, t, 0)),
            mod_spec,
            _const_spec((1, d), layer),
            _const_spec((d, D_FF), layer),
            _const_spec((D_FF, d), layer),
            _const_spec((1, d)),
        ],
        out_specs=pl.BlockSpec((1, tm, d), lambda b, t: (b, t, 0)),
        compiler_params=pltpu.CompilerParams(
            dimension_semantics=("arbitrary", "arbitrary"), vmem_limit_bytes=VMEM_LIMIT),
        name=name,
    )(x, mod, wl["n2g"], wl["w1"], wl["w2"], final_g)


def _sample_inproj_kernel(x_ref, mod_ref, n1g_ref, w_ref, o_ref, wb_ref):
    d = D_MODEL
    mod = mod_ref[...]
    hb = (_rmsnorm(x_ref[...], n1g_ref[...]) * (1.0 + mod[:, d:2 * d]) + mod[:, 0:d]).astype(BF16)
    wb = w_ref[...].astype(BF16)
    wb_ref[...] = wb
    o_ref[...] = _dot(hb, wb)


def _sample_inproj_call(layer, x, mod_s, wl, w_in):
    n, d = x.shape
    return pl.pallas_call(
        _sample_inproj_kernel,
        out_shape=(jax.ShapeDtypeStruct((n, N_IN), F32), jax.ShapeDtypeStruct((d, N_IN), BF16)),
        grid=(N_IN // d,),
        in_specs=[
            pl.BlockSpec((n, d), lambda j: (0, 0)),
            pl.BlockSpec((None, n, 6 * d), lambda j: (layer, 0, 0)),
            pl.BlockSpec((None, 1, d), lambda j: (layer, 0, 0)),
            pl.BlockSpec((None, d, d), lambda j: (layer, 0, j)),
        ],
        out_specs=(pl.BlockSpec((n, d), lambda j: (0, j)), pl.BlockSpec((d, d), lambda j: (0, j))),
        compiler_params=pltpu.CompilerParams(
            dimension_semantics=("arbitrary",), vmem_limit_bytes=VMEM_LIMIT),
        name=f"sample_inproj_{layer}",
    )(x, mod_s, wl["n1g"], w_in)


N_SAMPLE_MIXER_INPUTS = 20

OUTER_ROWS = 16
OUTER_LEFT_SRC = (0, 0, 1, 0, 2, 1, 3, 4, 5)
OUTER_RIGHT_SRC = (0, 1, 0, 2, 0, 1)


def _outer_tables():
    g = SAMPLE_GROUP
    sel_l = np.zeros((g * OUTER_ROWS, 6 * g), np.float32)
    sel_r = np.zeros((g * OUTER_ROWS, 4 * g), np.float32)
    for j in range(g):
        for r, a in enumerate(OUTER_LEFT_SRC):
            sel_l[OUTER_ROWS * j + r, g * a + j] = 1.0
        for r, a in enumerate(OUTER_RIGHT_SRC):
            sel_r[OUTER_ROWS * j + r, g * a + j] = 1.0
    ones_rows = np.zeros((g, OUTER_ROWS, LANES), np.float32)
    ones_rows[:, len(OUTER_RIGHT_SRC):len(OUTER_LEFT_SRC)] = 1.0
    return (jnp.asarray(sel_l, BF16), jnp.asarray(sel_r, BF16),
            jnp.asarray(ones_rows.reshape(g * OUTER_ROWS, LANES), BF16))


def _split3(x):
    x1 = x.astype(BF16).astype(F32)
    r = x - x1
    x2 = r.astype(BF16).astype(F32)
    x3 = (r - x2).astype(BF16).astype(F32)
    return [x1, x2, x3]


def _sample_mixer_kernel(layer, n_groups, *refs):
    refs = refs[:N_SAMPLE_MIXER_INPUTS] + refs[len(refs) - 10:]
    (u_ref, x_ref, mod_ref, hg_ref, lru_ref, conv_ref, hgl_ref, ong_ref, cw_ref, cb_ref,
     wax_ref, ba_ref, bx_ref, lam_ref, wba_ref, wbb_ref, wout_ref, sell_ref, selr_ref, ones_ref,
     xo_ref, hgo_ref, lruo_ref, convo_ref,
     f_s, k_s, q_s, v_s, o_s, ob_s) = refs
    d = D_MODEL
    n = x_ref.shape[0]
    i = pl.program_id(0)
    if layer == 0:
        for other in range(1, DEPTH):
            hgo_ref[other] = jnp.zeros(hgo_ref.shape[1:], F32)
        hgo_ref = hgo_ref.at[0]

    @pl.when(i == 0)
    def _():
        q_s[...] = _silu(u_ref[:, 0:d])
        lb = _lower_bound(hgl_ref[...], layer)
        f, k = _forget_key(u_ref[:, d:2 * d], lb)
        f_s[...] = f
        k_s[...] = k
        v_s[...] = u_ref[:, 2 * d:3 * d]
        xb = u_ref[:, 4 * d:5 * d]
        cw = cw_ref[...]
        xc = cb_ref[...] + cw[3:4] * xb
        for j in range(CONV_W - 1):
            xc = xc + cw[j:j + 1] * conv_ref[j]
        convo_ref[0] = conv_ref[1]
        convo_ref[1] = conv_ref[2]
        convo_ref[2] = xb
        yb = _gelu_tanh(u_ref[:, 5 * d:6 * d])
        sp = _softplus(-lam_ref[...])
        for blk in range(LRU_BLOCKS):
            cols = slice(blk * LRU_BS, (blk + 1) * LRU_BS)
            xcn = xc[:, cols]
            gates = _dot(xcn.astype(BF16), wax_ref[blk])
            a, bt = _lru_coeffs(xcn, gates, ba_ref[:, cols], bx_ref[:, cols], sp[:, cols])
            hn = a * lru_ref[:, cols] + bt
            lruo_ref[:, cols] = hn
            ob_s[:, cols] = hn * yb[:, cols]

    rows = pl.ds(pl.multiple_of(i * SAMPLE_GROUP, SAMPLE_GROUP), SAMPLE_GROUP)
    ones_rows = ones_ref[...]

    def outer_products(h):
        cols = slice(h * LANES, (h + 1) * LANES)
        vparts = _split3(v_s[rows, cols])
        left = jnp.concatenate(_split3(k_s[rows, cols]) + _split3(f_s[rows, cols]), axis=0).astype(BF16)
        right = jnp.concatenate(vparts + [jnp.zeros_like(vparts[0])], axis=0).astype(BF16)
        left_t = _dot(sell_ref[...], left).T.astype(BF16)
        right_rows = jnp.concatenate([_dot(selr_ref[...], right).astype(BF16), ones_rows], axis=1)
        out = []
        for j in range(SAMPLE_GROUP):
            lo, hi = j * OUTER_ROWS, (j + 1) * OUTER_ROWS
            pieces = [right_rows[lo:hi]]
            if lo:
                pieces.insert(0, jnp.zeros((lo, 2 * HG_DV), BF16))
            if hi < SAMPLE_GROUP * OUTER_ROWS:
                pieces.append(jnp.zeros((SAMPLE_GROUP * OUTER_ROWS - hi, 2 * HG_DV), BF16))
            out.append(_dot(left_t, jnp.concatenate(pieces, axis=0)))
        return out

    kv_f = outer_products(0)
    for h in range(HG_HEADS):
        cols = slice(h * LANES, (h + 1) * LANES)
        kv_f_next = outer_products(h + 1) if h + 1 < HG_HEADS else None
        s_new = [hg_ref[j, h] * kv_f[j][:, HG_DV:] + kv_f[j][:, :HG_DV] for j in range(SAMPLE_GROUP)]
        for j in range(SAMPLE_GROUP):
            hgo_ref[j, h] = s_new[j]
        qg = q_s[rows, cols].astype(BF16)
        o_rows = [_dot(qg, s_new[j].astype(BF16))[j:j + 1, :] for j in range(SAMPLE_GROUP)]
        o_s[rows, cols] = jnp.concatenate(o_rows, axis=0)
        kv_f = kv_f_next

    @pl.when(i == n_groups - 1)
    def _():
        ong = ong_ref[...]
        parts = []
        for h in range(HG_HEADS):
            cols = slice(h * LANES, (h + 1) * LANES)
            parts.append(_rmsnorm(o_s[:, cols], ong) * _silu(u_ref[:, 3 * d + h * LANES:3 * d + (h + 1) * LANES]))
        o_a = jnp.concatenate(parts, axis=1).astype(BF16)
        merged = (_sigmoid(u_ref[:, 6 * d:7 * d]) * _dot(o_a, wba_ref[...])
                  + _sigmoid(u_ref[:, 7 * d:8 * d]) * _dot(ob_s[...].astype(BF16), wbb_ref[...]))
        g1 = mod_ref[:, 2 * d:3 * d]
        xo_ref[...] = x_ref[...] + g1 * _dot(merged.astype(BF16), wout_ref[...])


def _sample_mixer_call(layer, u, x, mod_s, st_hg_all, st_lru_all, st_conv_t, hg_lower, wl, hg_out_prev):
    n, d = x.shape
    grp = SAMPLE_GROUP
    n_groups = n // grp

    def full(shape):
        nd = len(shape)
        return pl.BlockSpec(shape, lambda i: (0,) * nd)

    def of_layer(shape):
        nd = len(shape)
        return pl.BlockSpec((None,) + tuple(shape), lambda i: (layer,) + (0,) * nd)

    st_spec = pl.BlockSpec((None, grp, HG_HEADS, HG_DK, HG_DV), lambda i: (layer, i, 0, 0, 0))
    out_shapes = (
        jax.ShapeDtypeStruct((n, d), F32),
        jax.ShapeDtypeStruct((DEPTH, n, HG_HEADS, HG_DK, HG_DV), F32),
        jax.ShapeDtypeStruct((n, d), F32),
        jax.ShapeDtypeStruct((CONV_W - 1, n, d), F32),
    )
    in_specs = [
        full((n, N_IN)), full((n, d)), of_layer((n, 6 * d)), st_spec,
        of_layer((n, d)), full((CONV_W - 1, n, d)),
        full((DEPTH, d)), of_layer((1, HG_DV)), of_layer((CONV_W, d)), of_layer((1, d)),
        of_layer((LRU_BLOCKS, LRU_BS, 2 * LRU_BS)), of_layer((1, d)), of_layer((1, d)), of_layer((1, d)),
        of_layer((d, d)), of_layer((d, d)), of_layer((d, d)),
    ]
    outer_tables = _outer_tables()
    in_specs += [full(t.shape) for t in outer_tables]
    assert len(in_specs) == N_SAMPLE_MIXER_INPUTS
    args = [u, x, mod_s, st_hg_all, st_lru_all, st_conv_t, hg_lower, wl["ong"], wl["cw"], wl["cb"], wl["wax"],
            wl["ba"], wl["bx"], wl["lam"], wl["wba"], wl["wbb"], wl["wout"], *outer_tables]
    aliases = {}
    if hg_out_prev is not None:
        in_specs.append(pl.BlockSpec(memory_space=pl.ANY))
        args.append(hg_out_prev)
        aliases = {N_SAMPLE_MIXER_INPUTS: 1}
    st_out_spec = st_spec if layer else pl.BlockSpec((DEPTH, grp, HG_HEADS, HG_DK, HG_DV),
                                                     lambda i: (0, i, 0, 0, 0))
    out_specs = (full((n, d)), st_out_spec, full((n, d)), full((CONV_W - 1, n, d)))
    scratch = [
        pltpu.VMEM((n, d), F32),
        pltpu.VMEM((n, d), F32),
        pltpu.VMEM((n, d), F32),
        pltpu.VMEM((n, d), F32),
        pltpu.VMEM((n, d), F32),
        pltpu.VMEM((n, d), F32),
    ]
    return pl.pallas_call(
        functools.partial(_sample_mixer_kernel, layer, n_groups),
        out_shape=out_shapes,
        grid=(n_groups,),
        in_specs=in_specs,
        out_specs=out_specs,
        scratch_shapes=scratch,
        input_output_aliases=aliases,
        compiler_params=pltpu.CompilerParams(
            dimension_semantics=("arbitrary",), vmem_limit_bytes=VMEM_LIMIT),
        name=f"sample_mixer_{layer}",
    )(*args)


def _stacked_weights(norm1_g, norm2_g, hg_onorm_g, lru_conv_w, lru_conv_b, lru_wa, lru_ba, lru_wx,
                     lru_bx, lru_lambda, w_branch_a, w_branch_b, w_out, w_ff1, w_ff2):
    d = D_MODEL
    return {
        "n1g": norm1_g.reshape(DEPTH, 1, d),
        "n2g": norm2_g.reshape(DEPTH, 1, d),
        "ong": hg_onorm_g.reshape(DEPTH, 1, HG_DV),
        "cw": lru_conv_w,
        "cb": lru_conv_b.reshape(DEPTH, 1, d),
        "wax": jnp.concatenate([lru_wa, lru_wx], axis=-1).astype(BF16),
        "ba": lru_ba.reshape(DEPTH, 1, d),
        "bx": lru_bx.reshape(DEPTH, 1, d),
        "lam": lru_lambda.reshape(DEPTH, 1, d),
        "wba": w_branch_a.astype(BF16),
        "wbb": w_branch_b.astype(BF16),
        "wout": w_out.astype(BF16),
        "w1": w_ff1,
        "w2": w_ff2,
    }


def kernel(x_prompt, x_sample, state_hgrn, state_rglru, state_conv, c_prompt, c_sample, w_mod, b_mod, norm1_g, norm2_g, w_in, hg_lower, hg_onorm_g, lru_conv_w, lru_conv_b, lru_wa, lru_ba, lru_wx, lru_bx, lru_lambda, w_branch_a, w_branch_b, w_out, w_ff1, w_ff2, final_norm_g):
    bsz, seq, d = x_prompt.shape
    n_s = x_sample.shape[0]
    assert d == D_MODEL and x_sample.shape[1] == 1 and seq % MIX_TL == 0 and seq % FFN_TM == 0
    assert n_s % SAMPLE_GROUP == 0 and n_s == LANES

    mod_p_all, mod_s = _mod_call(c_prompt, c_sample, w_mod, b_mod)
    lvl = _hg_tables()
    final_g = final_norm_g.reshape(1, d)

    xp = x_prompt
    xs = x_sample.reshape(n_s, d)
    hg_p, lru_p, conv_p, lru_s, conv_s = [], [], [], [], []
    hg_s = None
    wl = _stacked_weights(norm1_g, norm2_g, hg_onorm_g, lru_conv_w, lru_conv_b, lru_wa, lru_ba,
                          lru_wx, lru_bx, lru_lambda, w_branch_a, w_branch_b, w_out, w_ff1, w_ff2)
    for l in range(DEPTH):
        mod_p = mod_p_all[l].reshape(bsz, 1, 6 * d)

        u, w_in_bf16 = _sample_inproj_call(l, xs, mod_s, wl, w_in)

        xp, hg, lru, conv = _prompt_mixer_call(l, xp, mod_p, hg_lower, wl, w_in_bf16, lvl)
        xp = _ffn_call(l, xp, mod_p, wl, final_g, FFN_TM, f"prompt_ffn_{l}")
        hg_p.append(hg)
        lru_p.append(lru.reshape(bsz, d))
        conv_p.append(conv)

        xs, hg_s, lru, conv = _sample_mixer_call(l, u, xs, mod_s, state_hgrn, state_rglru,
                                                 jnp.swapaxes(state_conv[l], 0, 1), hg_lower, wl, hg_s)
        xs = _ffn_call(l, xs.reshape(1, n_s, d), mod_s, wl, final_g, n_s, f"sample_ffn_{l}",
                       per_token_mod=True).reshape(n_s, d)
        lru_s.append(lru)
        conv_s.append(jnp.swapaxes(conv, 0, 1))

    return (xp, xs.reshape(n_s, 1, d),
            jnp.stack(hg_p), jnp.stack(lru_p), jnp.stack(conv_p),
            hg_s, jnp.stack(lru_s), jnp.stack(conv_s))
```
